```python
import numpy as np
import jax
import jax.numpy as jnp
from jax import lax

D_MODEL = 1024
BATCH = 8
SEQ = 2048
DEPTH = 4

N_MIXERS = 3
HEAD_DIM = 64
N_HEADS = D_MODEL // HEAD_DIM
D_FF = 2816
RMS_EPS = 1e-6
ROPE_THETA = 10000.0
BLOCK_Q = 128
NEG = -1e30

NSA_GROUPS = 4
NSA_REP = N_HEADS // NSA_GROUPS
NSA_CMP_LEN = 32
NSA_CMP_STRIDE = 16
NSA_CMP_HIDDEN = 2 * HEAD_DIM
NSA_SLC_LEN = 64
NSA_TOPK = 16
NSA_WINDOW = 512
NSA_SLC_QBLOCK = 32
NSA_FORCE_BONUS = 1e4

SWA_KV_HEADS = 2
SWA_REP = N_HEADS // SWA_KV_HEADS
SWA_WINDOW = 128

FOX_HEADS = N_HEADS

N_A = len(range(0, DEPTH, N_MIXERS))
N_B = len(range(1, DEPTH, N_MIXERS))
N_C = len(range(2, DEPTH, N_MIXERS))

Q_WIDTH = N_HEADS * HEAD_DIM
NSA_KV_WIDTH = 3 * 2 * NSA_GROUPS * HEAD_DIM
NSA_IN = Q_WIDTH + NSA_KV_WIDTH + 3 * N_HEADS
SWA_IN = Q_WIDTH + 2 * SWA_KV_HEADS * HEAD_DIM
FOX_IN = 3 * FOX_HEADS * HEAD_DIM + FOX_HEADS

kernel_name = 'hybrid_nsa_swa_fox_macaron'


def rmsnorm(x, g):
    xf = x.astype(jnp.float32)
    y = xf * lax.rsqrt(jnp.mean(xf * xf, axis=-1, keepdims=True) + RMS_EPS)
    return (y * g.astype(jnp.float32)).astype(x.dtype)


def swiglu(x, w_gu, w_down):
    g, u = jnp.split(x @ w_gu, 2, axis=-1)
    return (jax.nn.silu(g) * u) @ w_down


def rope_tables(S):
    inv = ROPE_THETA ** (-jnp.arange(0, HEAD_DIM, 2, dtype=jnp.float32) / HEAD_DIM)
    ang = jnp.arange(S, dtype=jnp.float32)[:, None] * inv[None, :]
    return jnp.cos(ang), jnp.sin(ang)


def apply_rope(x, cos, sin):
    x1, x2 = jnp.split(x.astype(jnp.float32), 2, axis=-1)
    c = cos[None, :, None, :]
    s = sin[None, :, None, :]
    return jnp.concatenate([x1 * c - x2 * s, x2 * c + x1 * s], axis=-1).astype(x.dtype)


def banded_attention(q, k, v, window, sinks=None):
    B, S, G, R, dh = q.shape
    span = window + BLOCK_Q
    kp = jnp.pad(k, ((0, 0), (window, 0), (0, 0), (0, 0)))
    vp = jnp.pad(v, ((0, 0), (window, 0), (0, 0), (0, 0)))
    scale = dh ** -0.5

    def step(i):
        start = i * BLOCK_Q
        qi = lax.dynamic_slice_in_dim(q, start, BLOCK_Q, axis=1)
        ki = lax.dynamic_slice_in_dim(kp, start, span, axis=1)
        vi = lax.dynamic_slice_in_dim(vp, start, span, axis=1)
        s = jnp.einsum('bqgrd,bkgd->bgrqk', qi, ki).astype(jnp.float32) * scale
        qpos = start + jnp.arange(BLOCK_Q)
        kpos = start - window + jnp.arange(span)
        diff = qpos[:, None] - kpos[None, :]
        mask = (diff >= 0) & (diff < window) & (kpos[None, :] >= 0)
        s = jnp.where(mask, s, NEG)
        if sinks is None:
            p = jax.nn.softmax(s, axis=-1)
        else:
            sk = sinks.astype(jnp.float32)[None, :, :, None, None]
            m = jnp.maximum(jnp.max(s, axis=-1, keepdims=True), sk)
            e = jnp.exp(s - m)
            p = e / (jnp.sum(e, axis=-1, keepdims=True) + jnp.exp(sk - m))
        return jnp.einsum('bgrqk,bkgd->bqgrd', p.astype(vi.dtype), vi)

    o = lax.map(step, jnp.arange(S // BLOCK_Q))
    return jnp.moveaxis(o, 0, 1).reshape(B, S, G, R, dh)


def compress_blocks(x, blk_idx, pe, w1, w2):
    B, _, G, dh = x.shape
    n_cmp, L = blk_idx.shape
    xb = x[:, blk_idx] + pe[:, None, :]
    xb = xb.transpose(0, 1, 3, 2, 4).reshape(B, n_cmp, G, L * dh)
    return jax.nn.gelu(xb @ w1) @ w2


def cmp_to_slc_overlap(n_cmp, n_slc):
    cs = np.arange(n_cmp) * NSA_CMP_STRIDE
    ce = cs + NSA_CMP_LEN
    ss = np.arange(n_slc) * NSA_SLC_LEN
    se = ss + NSA_SLC_LEN
    ov = np.clip(np.minimum(ce[:, None], se[None, :]) - np.maximum(cs[:, None], ss[None, :]), 0, None)
    return jnp.asarray(ov / NSA_CMP_LEN, dtype=jnp.float32)


def selected_block_attention(q, k, v, sel):
    B, S, G, R, dh = q.shape
    n_sel = sel.shape[-1]
    n_slc = S // NSA_SLC_LEN
    m = n_sel * NSA_SLC_LEN
    kblk = k.reshape(B, n_slc, NSA_SLC_LEN, G, dh).transpose(0, 3, 1, 2, 4)
    vblk = v.reshape(B, n_slc, NSA_SLC_LEN, G, dh).transpose(0, 3, 1, 2, 4)
    gather = jax.vmap(jax.vmap(lambda blocks, ids: blocks[ids]))
    offs = jnp.arange(NSA_SLC_LEN)
    scale = dh ** -0.5

    def step(i):
        start = i * NSA_SLC_QBLOCK
        qi = lax.dynamic_slice_in_dim(q, start, NSA_SLC_QBLOCK, axis=1)
        ids = lax.dynamic_slice_in_dim(sel, start, NSA_SLC_QBLOCK, axis=2)
        kg = gather(kblk, ids).reshape(B, G, NSA_SLC_QBLOCK, m, dh)
        vg = gather(vblk, ids).reshape(B, G, NSA_SLC_QBLOCK, m, dh)
        kpos = (ids[..., None] * NSA_SLC_LEN + offs).reshape(B, G, NSA_SLC_QBLOCK, m)
        qpos = start + jnp.arange(NSA_SLC_QBLOCK)
        mask = (kpos <= qpos[:, None])[:, :, None]
        s = jnp.einsum('bqgrd,bgqmd->bgrqm', qi, kg).astype(jnp.float32) * scale
        p = jax.nn.softmax(jnp.where(mask, s, NEG), axis=-1)
        return jnp.einsum('bgrqm,bgqmd->bqgrd', p.astype(vg.dtype), vg)

    o = lax.map(step, jnp.arange(S // NSA_SLC_QBLOCK))
    return jnp.moveaxis(o, 0, 1).reshape(B, S, G, R, dh)


def nsa_mixer(h, cos, sin, w_in, ck_pe, ck_w1, ck_w2, cv_pe, cv_w1, cv_w2, w_out):
    B, S, _ = h.shape
    G, R, dh = NSA_GROUPS, NSA_REP, HEAD_DIM
    scale = dh ** -0.5
    proj = h @ w_in
    q = apply_rope(proj[..., :Q_WIDTH].reshape(B, S, N_HEADS, dh), cos, sin).reshape(B, S, G, R, dh)
    kv = proj[..., Q_WIDTH:Q_WIDTH + NSA_KV_WIDTH].reshape(B, S, 3, 2, G, dh)
    gates = jax.nn.sigmoid(proj[..., Q_WIDTH + NSA_KV_WIDTH:].astype(jnp.float32)).reshape(B, S, 3, G, R, 1)
    k = apply_rope(kv[:, :, :, 0].reshape(B, S, 3 * G, dh), cos, sin).reshape(B, S, 3, G, dh)
    v = kv[:, :, :, 1]
    t = jnp.arange(S)

    n_cmp = (S - NSA_CMP_LEN) // NSA_CMP_STRIDE + 1
    blk_idx = jnp.arange(n_cmp)[:, None] * NSA_CMP_STRIDE + jnp.arange(NSA_CMP_LEN)[None, :]
    k_cmp = compress_blocks(k[:, :, 0], blk_idx, ck_pe, ck_w1, ck_w2)
    v_cmp = compress_blocks(v[:, :, 0], blk_idx, cv_pe, cv_w1, cv_w2)
    s_cmp = jnp.einsum('bsgrd,bngd->bgrsn', q, k_cmp).astype(jnp.float32) * scale
    cmp_valid = (jnp.arange(n_cmp) * NSA_CMP_STRIDE + NSA_CMP_LEN - 1)[None, :] <= t[:, None]
    p_cmp = jax.nn.softmax(jnp.where(cmp_valid, s_cmp, NEG), axis=-1) * cmp_valid
    o_cmp = jnp.einsum('bgrsn,bngd->bsgrd', p_cmp.astype(v_cmp.dtype), v_cmp)

    n_slc = S // NSA_SLC_LEN
    k_sel = min(NSA_TOPK, n_slc)
    imp = jnp.einsum('bgrsn,nj->bgsj', p_cmp, cmp_to_slc_overlap(n_cmp, n_slc))
    tb = (t // NSA_SLC_LEN)[:, None]
    j = jnp.arange(n_slc)[None, :]
    forced = (j == 0) | (j == tb) | (j == tb - 1)
    imp = jnp.where(j > tb, NEG, jnp.where(forced, NSA_FORCE_BONUS, imp))
    _, sel = lax.top_k(imp, k_sel)
    o_slc = selected_block_attention(q, k[:, :, 1], v[:, :, 1], sel)

    o_win = banded_attention(q, k[:, :, 2], v[:, :, 2], NSA_WINDOW)

    o = gates[:, :, 0] * o_cmp + gates[:, :, 1] * o_slc + gates[:, :, 2] * o_win
    return o.astype(h.dtype).reshape(B, S, Q_WIDTH) @ w_out


def swa_mixer(h, cos, sin, w_in, sinks, w_out):
    B, S, _ = h.shape
    proj = h @ w_in
    q = apply_rope(proj[..., :Q_WIDTH].reshape(B, S, N_HEADS, HEAD_DIM), cos, sin)
    q = q.reshape(B, S, SWA_KV_HEADS, SWA_REP, HEAD_DIM)
    kv = proj[..., Q_WIDTH:].reshape(B, S, 2, SWA_KV_HEADS, HEAD_DIM)
    k = apply_rope(kv[:, :, 0], cos, sin)
    v = kv[:, :, 1]
    o = banded_attention(q, k, v, SWA_WINDOW, sinks.reshape(SWA_KV_HEADS, SWA_REP))
    return o.reshape(B, S, Q_WIDTH) @ w_out


def fox_mixer(h, w_in, b_f, w_out):
    B, S, _ = h.shape
    dh = HEAD_DIM
    scale = dh ** -0.5
    proj = h @ w_in
    qkv = proj[..., :3 * FOX_HEADS * dh].reshape(B, S, 3, FOX_HEADS, dh)
    q, k, v = qkv[:, :, 0], qkv[:, :, 1], qkv[:, :, 2]
    log_f = jax.nn.log_sigmoid((proj[..., 3 * FOX_HEADS * dh:] + b_f).astype(jnp.float32))
    c = jnp.cumsum(log_f, axis=1).transpose(0, 2, 1)
    kpos = jnp.arange(S)

    def step(i):
        start = i * BLOCK_Q
        qi = lax.dynamic_slice_in_dim(q, start, BLOCK_Q, axis=1)
        ci = lax.dynamic_slice_in_dim(c, start, BLOCK_Q, axis=2)
        s = jnp.einsum('bqhd,bkhd->bhqk', qi, k).astype(jnp.float32) * scale
        s = s + ci[..., None] - c[:, :, None, :]
        qpos = start + jnp.arange(BLOCK_Q)
        p = jax.nn.softmax(jnp.where(kpos[None, :] <= qpos[:, None], s, NEG), axis=-1)
        return jnp.einsum('bhqk,bkhd->bqhd', p.astype(v.dtype), v)

    o = lax.map(step, jnp.arange(S // BLOCK_Q))
    return jnp.moveaxis(o, 0, 1).reshape(B, S, FOX_HEADS * dh) @ w_out


def setup_inputs(seed: int = 0) -> dict:
    key = jax.random.key(seed)
    ks = iter(jax.random.split(key, 32))
    f32 = jnp.float32

    def nrm(shape, scale):
        return scale * jax.random.normal(next(ks), shape, f32)

    def gain(shape):
        return 1.0 + 0.01 * jax.random.normal(next(ks), shape, f32)

    L, dh, Hc = NSA_CMP_LEN, HEAD_DIM, NSA_CMP_HIDDEN
    return {
        'x': nrm((BATCH, SEQ, D_MODEL), 1.0),
        'ffn1_norm': gain((DEPTH, D_MODEL)),
        'ffn1_w_gu': nrm((DEPTH, D_MODEL, 2 * D_FF), D_MODEL ** -0.5),
        'ffn1_w_down': nrm((DEPTH, D_FF, D_MODEL), D_FF ** -0.5),
        'mix_norm': gain((DEPTH, D_MODEL)),
        'ffn2_norm': gain((DEPTH, D_MODEL)),
        'ffn2_w_gu': nrm((DEPTH, D_MODEL, 2 * D_FF), D_MODEL ** -0.5),
        'ffn2_w_down': nrm((DEPTH, D_FF, D_MODEL), D_FF ** -0.5),
        'nsa_w_in': nrm((N_A, D_MODEL, NSA_IN), D_MODEL ** -0.5),
        'nsa_ck_pe': nrm((N_A, L, dh), 0.1),
        'nsa_ck_w1': nrm((N_A, L * dh, Hc), (L * dh) ** -0.5),
        'nsa_ck_w2': nrm((N_A, Hc, dh), Hc ** -0.5),
        'nsa_cv_pe': nrm((N_A, L, dh), 0.1),
        'nsa_cv_w1': nrm((N_A, L * dh, Hc), (L * dh) ** -0.5),
        'nsa_cv_w2': nrm((N_A, Hc, dh), Hc ** -0.5),
        'nsa_w_out': nrm((N_A, Q_WIDTH, D_MODEL), Q_WIDTH ** -0.5),
        'swa_w_in': nrm((N_B, D_MODEL, SWA_IN), D_MODEL ** -0.5),
        'swa_sinks': nrm((N_B, N_HEADS), 1.0),
        'swa_w_out': nrm((N_B, Q_WIDTH, D_MODEL), Q_WIDTH ** -0.5),
        'fox_w_in': nrm((N_C, D_MODEL, FOX_IN), D_MODEL ** -0.5),
        'fox_b_f': jax.random.uniform(next(ks), (N_C, FOX_HEADS), f32, 1.0, 4.0),
        'fox_w_out': nrm((N_C, FOX_HEADS * HEAD_DIM, D_MODEL), (FOX_HEADS * HEAD_DIM) ** -0.5),
        'final_norm': gain((D_MODEL,)),
    }


def reference(x, ffn1_norm, ffn1_w_gu, ffn1_w_down, mix_norm, ffn2_norm, ffn2_w_gu, ffn2_w_down,
              nsa_w_in, nsa_ck_pe, nsa_ck_w1, nsa_ck_w2, nsa_cv_pe, nsa_cv_w1, nsa_cv_w2, nsa_w_out,
              swa_w_in, swa_sinks, swa_w_out, fox_w_in, fox_b_f, fox_w_out, final_norm):
    S = x.shape[1]
    cos, sin = rope_tables(S)
    for i in range(DEPTH):
        kind, j = i % N_MIXERS, i // N_MIXERS
        x = x + 0.5 * swiglu(rmsnorm(x, ffn1_norm[i]), ffn1_w_gu[i], ffn1_w_down[i])
        h = rmsnorm(x, mix_norm[i])
        if kind == 0:
            y = nsa_mixer(h, cos, sin, nsa_w_in[j], nsa_ck_pe[j], nsa_ck_w1[j], nsa_ck_w2[j],
                          nsa_cv_pe[j], nsa_cv_w1[j], nsa_cv_w2[j], nsa_w_out[j])
        elif kind == 1:
            y = swa_mixer(h, cos, sin, swa_w_in[j], swa_sinks[j], swa_w_out[j])
        else:
            y = fox_mixer(h, fox_w_in[j], fox_b_f[j], fox_w_out[j])
        x = x + y
        x = x + 0.5 * swiglu(rmsnorm(x, ffn2_norm[i]), ffn2_w_gu[i], ffn2_w_down[i])
    return rmsnorm(x, final_norm)
```

```python
import functools

import numpy as np
import jax
import jax.numpy as jnp
from jax import lax
from jax.experimental import pallas as pl
from jax.experimental.pallas import tpu as pltpu

D_MODEL = 1024
HEAD_DIM = 64
N_HEADS = 16
D_FF = 2816
RMS_EPS = 1e-6
ROPE_THETA = 10000.0
NEG = -1e30

NSA_GROUPS = 4
NSA_CMP_LEN = 32
NSA_CMP_STRIDE = 16
NSA_CMP_HIDDEN = 128
NSA_SLC_LEN = 64
NSA_TOPK = 16
NSA_WINDOW = 512
NSA_FORCE_BONUS = 1e4
SWA_WINDOW = 128
N_MIXERS = 3

LANES = 128
HALF = 64
VMEM_LIMIT = 56 * 1024 * 1024

F32 = jnp.float32
BF16 = jnp.bfloat16


def _nt_dot(a, b):
    return lax.dot_general(a, b, (((1,), (1,)), ((), ())), preferred_element_type=F32)


def _dot(a, b):
    return jnp.dot(a, b, preferred_element_type=F32)


def _rms(x, g):
    ms = jnp.mean(x * x, axis=-1, keepdims=True)
    return x * lax.rsqrt(ms + RMS_EPS) * g


def _cparams(sem):
    return pltpu.CompilerParams(dimension_semantics=sem, vmem_limit_bytes=VMEM_LIMIT)


FFN_TM = 512
FFN_TF = 256


def _ffn_kernel(x_ref, g_ref, wgu_ref, wd_ref, *rest, final):
    if final:
        fg_ref, o_ref = rest
    else:
        (o_ref,) = rest
    x = x_ref[...]
    hb = _rms(x, g_ref[...]).astype(BF16)
    acc = jnp.zeros(x.shape, F32)
    for f in range(D_FF // FFN_TF):
        lo = f * FFN_TF
        g = _dot(hb, wgu_ref[:, lo:lo + FFN_TF])
        u = _dot(hb, wgu_ref[:, D_FF + lo:D_FF + lo + FFN_TF])
        a = (g * jax.nn.sigmoid(g)) * u
        acc = acc + _dot(a.astype(BF16), wd_ref[lo:lo + FFN_TF, :])
    y = x + 0.5 * acc
    if final:
        y = _rms(y, fg_ref[...])
    o_ref[...] = y


def _ffn(x2, gain, wgu, wd, final_gain=None):
    n = x2.shape[0]
    final = final_gain is not None
    resident = dict(pipeline_mode=pl.Buffered(1))
    in_specs = [
        pl.BlockSpec((FFN_TM, D_MODEL), lambda i: (i, 0)),
        pl.BlockSpec((1, D_MODEL), lambda i: (0, 0)),
        pl.BlockSpec((D_MODEL, 2 * D_FF), lambda i: (0, 0), **resident),
        pl.BlockSpec((D_FF, D_MODEL), lambda i: (0, 0), **resident),
    ]
    args = [x2, gain.reshape(1, D_MODEL), wgu, wd]
    if final:
        in_specs.append(pl.BlockSpec((1, D_MODEL), lambda i: (0, 0)))
        args.append(final_gain.reshape(1, D_MODEL))
    return pl.pallas_call(
        functools.partial(_ffn_kernel, final=final),
        grid=(n // FFN_TM,),
        in_specs=in_specs,
        out_specs=pl.BlockSpec((FFN_TM, D_MODEL), lambda i: (i, 0)),
        out_shape=jax.ShapeDtypeStruct((n, D_MODEL), F32),
        compiler_params=_cparams(("parallel",)),
        name="ffn",
    )(*args)


PROJ_TM = 512
PROJ_CH = 256


def _rope_tile(y, cos_t, sin_t, first_half):
    rot = jnp.where(first_half, pltpu.roll(y, 96, 1), pltpu.roll(y, 32, 1))
    return y * cos_t + rot * sin_t


def _proj_kernel(*refs, groups, use_rope):
    if use_rope:
        x_ref, g_ref, w_ref, cos_ref, sin_ref = refs[:5]
        o_refs = refs[5:]
        cos_t = cos_ref[...]
        sin_t = sin_ref[...]
        lane = lax.broadcasted_iota(jnp.int32, (1, LANES), 1)
        first_half = (lane % HALF) < (HALF // 2)
    else:
        x_ref, g_ref, w_ref = refs[:3]
        o_refs = refs[3:]
    hb = _rms(x_ref[...], g_ref[...]).astype(BF16)
    for (c0, width, rope, scale), o_ref in zip(groups, o_refs):
        for t0 in range(0, width, PROJ_CH):
            ch = min(PROJ_CH, width - t0)
            y = _dot(hb, w_ref[:, c0 + t0:c0 + t0 + ch])
            for l0 in range(0, ch, LANES):
                yt = y[:, l0:l0 + LANES]
                if rope:
                    yt = _rope_tile(yt, cos_t, sin_t, first_half)
                if scale != 1.0:
                    yt = yt * scale
                o_ref[:, t0 + l0:t0 + l0 + LANES] = yt.astype(o_ref.dtype)


def _proj(x2, gain, w, groups, dtypes, rope_tabs=None, seq=None):
    n = x2.shape[0]
    wtot = w.shape[1]
    use_rope = rope_tabs is not None
    in_specs = [
        pl.BlockSpec((PROJ_TM, D_MODEL), lambda i: (i, 0)),
        pl.BlockSpec((1, D_MODEL), lambda i: (0, 0)),
        pl.BlockSpec((D_MODEL, wtot), lambda i: (0, 0), pipeline_mode=pl.Buffered(1)),
    ]
    args = [x2, gain.reshape(1, D_MODEL), w]
    if use_rope:
        nblk = seq // PROJ_TM
        in_specs += [pl.BlockSpec((PROJ_TM, LANES), lambda i: (i % nblk, 0))] * 2
        args += list(rope_tabs)
    out_specs = [pl.BlockSpec((PROJ_TM, g[1]), lambda i: (i, 0)) for g in groups]
    out_shape = [jax.ShapeDtypeStruct((n, g[1]), dt) for g, dt in zip(groups, dtypes)]
    return pl.pallas_call(
        functools.partial(_proj_kernel, groups=groups, use_rope=use_rope),
        grid=(n // PROJ_TM,),
        in_specs=in_specs,
        out_specs=out_specs,
        out_shape=out_shape,
        compiler_params=_cparams(("parallel",)),
        name="proj",
    )(*args)


OUT_TM = 512


def _outproj_kernel(*refs, n_in):
    o_refs = refs[:n_in]
    w_ref, x_ref, out_ref = refs[n_in:]
    o = o_refs[0][...].astype(F32)
    for r in o_refs[1:]:
        o = o + r[...].astype(F32)
    out_ref[...] = x_ref[...] + _dot(o.astype(BF16), w_ref[...])


def _outproj(os_, w, x2):
    n = x2.shape[0]
    n_in = len(os_)
    row = pl.BlockSpec((OUT_TM, D_MODEL), lambda i: (i, 0))
    return pl.pallas_call(
        functools.partial(_outproj_kernel, n_in=n_in),
        grid=(n // OUT_TM,),
        in_specs=[row] * n_in + [pl.BlockSpec((D_MODEL, D_MODEL), lambda i: (0, 0)), row],
        out_specs=row,
        out_shape=jax.ShapeDtypeStruct((n, D_MODEL), F32),
        compiler_params=_cparams(("parallel",)),
        name="outproj",
    )(*os_, w, x2)


def _gate_col(gates, lane_idx):
    lane = lax.broadcasted_iota(jnp.int32, gates.shape, 1)
    col = jnp.sum(jnp.where(lane == lane_idx, gates, 0.0), axis=1, keepdims=True)
    return jax.nn.sigmoid(col)


def _flash_kernel(*refs, nt, tq, tk, window, use_sel, use_fox, use_sink, gate_branch, gqa):
    it = iter(refs)
    q_ref = next(it)
    k_ref = next(it)
    v_ref = next(it)
    sel_ref = e_ref = crow_ref = ccol_ref = sink_ref = gate_ref = None
    if use_sel:
        sel_ref = next(it)
        e_ref = next(it)
    if use_fox:
        crow_ref = next(it)
        ccol_ref = next(it)
    if use_sink:
        sink_ref = next(it)
    if gate_branch is not None:
        gate_ref = next(it)
    o_ref = next(it)
    qal_ref = next(it)
    m_ref = next(it)
    l_ref = next(it)
    acc_ref = next(it)

    b = pl.program_id(0)
    hg = pl.program_id(1)
    i = pl.program_id(2)
    nh = 2 * nt

    lane = lax.broadcasted_iota(jnp.int32, (1, LANES), 1)
    half = lane // HALF
    if gqa:
        kv_half = jnp.zeros((1, LANES), jnp.int32) + (hg % 2)

    for tt in range(nt):
        qt = q_ref[0, :, tt * LANES:(tt + 1) * LANES].astype(F32)
        if gqa:
            qr = pltpu.roll(qt, HALF, 1)
        for a in range(2):
            if gqa:
                qa = jnp.where(half == kv_half, jnp.where(kv_half == a, qt, qr), 0.0)
            else:
                qa = jnp.where(half == a, qt, 0.0)
            qal_ref[2 * tt + a] = qa.astype(BF16)

    m_ref[...] = jnp.full(m_ref.shape, NEG, F32)
    l_ref[...] = jnp.zeros(l_ref.shape, F32)
    acc_ref[...] = jnp.zeros(acc_ref.shape, F32)

    q0 = i * tq
    rowcol = (lax.broadcasted_iota(jnp.int32, (tq, tk), 0)
              - lax.broadcasted_iota(jnp.int32, (tq, tk), 1))
    if use_sel:
        sel = sel_ref[0, 0]
    if use_fox:
        lane_q = lax.broadcasted_iota(jnp.int32, (tq, LANES), 1)
        cblk = ccol_ref[...]
        ccols = []
        for a in range(2):
            li = (b * N_HEADS + 2 * hg + a) % LANES
            ccols.append(jnp.sum(jnp.where(lane_q == li, cblk, 0.0), axis=1, keepdims=True))

    if window is None:
        c_lo = 0
    else:
        c_lo = jnp.maximum(q0 - (window - 1), 0) // tk
    c_hi = (q0 + tq + tk - 1) // tk

    def chunk(c, carry):
        k0 = pl.multiple_of(c * tk, tk)
        kc = k_ref[0, pl.ds(k0, tk), :]
        vc = v_ref[0, pl.ds(k0, tk), :]
        d = rowcol + (q0 - k0)
        mask = d >= 0
        if window is not None:
            mask = mask & (d < window)
        if use_sel:
            mask = mask & (_dot(sel, e_ref[c]) > 0.5)
        if use_fox:
            crow = crow_ref[0, 0, c]
        for h in range(nh):
            s = _nt_dot(qal_ref[h], kc)
            if use_fox:
                s = s + (ccols[h] - crow[h:h + 1, :])
            s = jnp.where(mask, s, NEG)
            m_old = m_ref[h]
            m_new = jnp.maximum(m_old, jnp.max(s, axis=1, keepdims=True))
            alpha = jnp.exp(m_old - m_new)
            p = jnp.exp(s - m_new)
            l_ref[h] = alpha * l_ref[h] + jnp.sum(p, axis=1, keepdims=True)
            acc_ref[h] = alpha * acc_ref[h] + _dot(p.astype(BF16), vc)
            m_ref[h] = m_new
        return carry

    lax.fori_loop(c_lo, c_hi, chunk, 0)

    if gate_branch is not None:
        gates = gate_ref[0]
    for tt in range(nt):
        outs = []
        for a in range(2):
            h = 2 * tt + a
            m = m_ref[h]
            l = l_ref[h]
            acc = acc_ref[h]
            if use_sink:
                sk = sink_ref[hg * nh + h]
                m2 = jnp.maximum(m, sk)
                f = jnp.exp(m - m2)
                l = l * f + jnp.exp(sk - m2)
                acc = acc * f
            out = acc * (1.0 / l)
            if gqa:
                out = jnp.where(kv_half == a, out, pltpu.roll(out, HALF, 1))
            if gate_branch is not None:
                out = out * _gate_col(gates, gate_branch * N_HEADS + hg * nh + h)
            outs.append(out)
        tile = jnp.where(half == 0, outs[0], outs[1])
        o_ref[0, :, tt * LANES:(tt + 1) * LANES] = tile.astype(o_ref.dtype)


def _flash(q, k, v, *, nt, tq, tk, kv_tile, gqa, out_dtype, window=None, sel=None, emat=None,
           crow=None, ccol=None, sinks=None, gates=None, gate_branch=None):
    bsz, seq, qw = q.shape
    n_hg = qw // (nt * LANES)
    nq = seq // tq
    use_sel = sel is not None
    use_fox = crow is not None
    use_sink = sinks is not None
    in_specs = [
        pl.BlockSpec((1, tq, nt * LANES), lambda b, g, i: (b, i, g)),
        pl.BlockSpec((1, seq, LANES), lambda b, g, i: (b, 0, kv_tile(g))),
        pl.BlockSpec((1, seq, LANES), lambda b, g, i: (b, 0, kv_tile(g))),
    ]
    args = [q, k, v]
    if use_sel:
        in_specs += [
            pl.BlockSpec((1, 1, tq, LANES), lambda b, g, i: (b, g, i, 0)),
            pl.BlockSpec(emat.shape, lambda b, g, i: (0, 0, 0)),
        ]
        args += [sel, emat]
    if use_fox:
        in_specs += [
            pl.BlockSpec((1, 1) + crow.shape[2:], lambda b, g, i: (b, g, 0, 0, 0)),
            pl.BlockSpec((tq, LANES), lambda b, g, i: (i, (b * N_HEADS) // LANES)),
        ]
        args += [crow, ccol]
    if use_sink:
        in_specs.append(pl.BlockSpec(memory_space=pltpu.SMEM))
        args.append(sinks)
    if gate_branch is not None:
        in_specs.append(pl.BlockSpec((1, tq, LANES), lambda b, g, i: (b, i, 0)))
        args.append(gates)
    nh = 2 * nt
    kern = functools.partial(
        _flash_kernel, nt=nt, tq=tq, tk=tk, window=window, use_sel=use_sel, use_fox=use_fox,
        use_sink=use_sink, gate_branch=gate_branch, gqa=gqa)
    return pl.pallas_call(
        kern,
        grid=(bsz, n_hg, nq),
        in_specs=in_specs,
        out_specs=pl.BlockSpec((1, tq, nt * LANES), lambda b, g, i: (b, i, g)),
        out_shape=jax.ShapeDtypeStruct((bsz, seq, qw), out_dtype),
        scratch_shapes=[
            pltpu.VMEM((nh, tq, LANES), BF16),
            pltpu.VMEM((nh, tq, 1), F32),
            pltpu.VMEM((nh, tq, 1), F32),
            pltpu.VMEM((nh, tq, LANES), F32),
        ],
        compiler_params=_cparams(("parallel", "parallel", "arbitrary")),
        name="flash",
    )(*args)


N_CHUNK16 = 128


def _compress_kernel(xk_ref, xv_ref, pek_ref, pev_ref, wk1_ref, wv1_ref, wk2_ref, wv2_ref,
                     kc_ref, vc_ref):
    hid = NSA_GROUPS * NSA_CMP_HIDDEN
    for x_ref, pe_ref, w1_ref, w2_ref, o_ref in (
            (xk_ref, pek_ref, wk1_ref, wk2_ref, kc_ref),
            (xv_ref, pev_ref, wv1_ref, wv2_ref, vc_ref)):
        x = x_ref[0]
        top = _dot((x + pe_ref[0:1, :]).astype(BF16), w1_ref[:, :hid])
        bot = _dot((x + pe_ref[1:2, :]).astype(BF16), w1_ref[:, hid:])
        h1 = top + pltpu.roll(bot, N_CHUNK16 - 1, 0)
        a = jax.nn.gelu(h1, approximate=True)
        o_ref[0] = _dot(a.astype(BF16), w2_ref[...]).astype(o_ref.dtype)


def _compress(xk, xv, pek, pev, wk1, wv1, wk2, wv2):
    bsz = xk.shape[0]
    width = xk.shape[2]
    gd = NSA_GROUPS * HEAD_DIM
    xspec = pl.BlockSpec((1, N_CHUNK16, width), lambda b: (b, 0, 0))
    full = lambda a: pl.BlockSpec(a.shape, lambda b: (0,) * a.ndim)
    ospec = pl.BlockSpec((1, N_CHUNK16, gd), lambda b: (b, 0, 0))
    oshape = jax.ShapeDtypeStruct((bsz, N_CHUNK16, gd), BF16)
    return pl.pallas_call(
        _compress_kernel,
        grid=(bsz,),
        in_specs=[xspec, xspec, full(pek), full(pev), full(wk1), full(wv1), full(wk2), full(wv2)],
        out_specs=[ospec, ospec],
        out_shape=[oshape, oshape],
        compiler_params=_cparams(("parallel",)),
        name="compress",
    )(xk, xv, pek, pev, wk1, wv1, wk2, wv2)


CMP_TQ = 256
N_SLC = 32


def _cmp_kernel(q_ref, kc_ref, vc_ref, gate_ref, ovt_ref, o_ref, sel_ref):
    tq = CMP_TQ
    g = pl.program_id(1)
    i = pl.program_id(2)
    lane = lax.broadcasted_iota(jnp.int32, (1, LANES), 1)
    half = lane // HALF
    kv_half = jnp.zeros((1, LANES), jnp.int32) + (g % 2)
    kc = kc_ref[0]
    vc = vc_ref[0]
    gates = gate_ref[0]
    t_col = i * tq + lax.broadcasted_iota(jnp.int32, (tq, 1), 0)
    n_cmp = (N_CHUNK16 * NSA_CMP_STRIDE - NSA_CMP_LEN) // NSA_CMP_STRIDE + 1
    valid = (lane * NSA_CMP_STRIDE + (NSA_CMP_LEN - 1) <= t_col) & (lane < n_cmp)

    psum = jnp.zeros((tq, LANES), F32)
    for tt in range(2):
        qt = q_ref[0, :, tt * LANES:(tt + 1) * LANES].astype(F32)
        qr = pltpu.roll(qt, HALF, 1)
        outs = []
        for a in range(2):
            qa = jnp.where(half == kv_half, jnp.where(kv_half == a, qt, qr), 0.0).astype(BF16)
            s = jnp.where(valid, _nt_dot(qa, kc), NEG)
            m = jnp.max(s, axis=1, keepdims=True)
            e = jnp.where(valid, jnp.exp(s - m), 0.0)
            l = jnp.sum(e, axis=1, keepdims=True)
            p = e * (1.0 / jnp.where(l > 0.0, l, 1.0))
            psum = psum + p
            out = _dot(p.astype(BF16), vc)
            out = jnp.where(kv_half == a, out, pltpu.roll(out, HALF, 1))
            outs.append(out * _gate_col(gates, g * 4 + 2 * tt + a))
        o_ref[0, :, tt * LANES:(tt + 1) * LANES] = jnp.where(half == 0, outs[0], outs[1])

    p_hi = psum.astype(BF16)
    p_lo = (psum - p_hi.astype(F32)).astype(BF16)
    ovt = ovt_ref[...]
    imp = _nt_dot(ovt, p_hi) + _nt_dot(ovt, p_lo)
    j = lax.broadcasted_iota(jnp.int32, (N_SLC, 1), 0)
    tb = jnp.right_shift(i * tq + lax.broadcasted_iota(jnp.int32, (1, tq), 1), 6)
    forced = (j == 0) | (j == tb) | (j == tb - 1)
    imp = jnp.where(j > tb, NEG, jnp.where(forced, NSA_FORCE_BONUS, imp))
    cnt = jnp.zeros((N_SLC, tq), jnp.int32)
    for jp in range(N_SLC):
        row = imp[jp:jp + 1, :]
        beats = (row > imp) | ((row == imp) & (jp < j))
        cnt = cnt + beats.astype(jnp.int32)
    sel_t = (cnt < NSA_TOPK).astype(F32)
    sel_t = jnp.concatenate([sel_t, jnp.zeros((LANES - N_SLC, tq), F32)], axis=0).astype(BF16)
    eye = (lax.broadcasted_iota(jnp.int32, (tq, tq), 0)
           == lax.broadcasted_iota(jnp.int32, (tq, tq), 1)).astype(BF16)
    sel_ref[0, 0] = _nt_dot(eye, sel_t).astype(BF16)


def _cmp_attn(q, kc, vc, gates, ovt):
    bsz, seq, _ = q.shape
    tq = CMP_TQ
    gw = 2 * LANES
    return pl.pallas_call(
        _cmp_kernel,
        grid=(bsz, NSA_GROUPS, seq // tq),
        in_specs=[
            pl.BlockSpec((1, tq, gw), lambda b, g, i: (b, i, g)),
            pl.BlockSpec((1, N_CHUNK16, LANES), lambda b, g, i: (b, 0, g // 2)),
            pl.BlockSpec((1, N_CHUNK16, LANES), lambda b, g, i: (b, 0, g // 2)),
            pl.BlockSpec((1, tq, LANES), lambda b, g, i: (b, i, 0)),
            pl.BlockSpec(ovt.shape, lambda b, g, i: (0, 0)),
        ],
        out_specs=[
            pl.BlockSpec((1, tq, gw), lambda b, g, i: (b, i, g)),
            pl.BlockSpec((1, 1, tq, LANES), lambda b, g, i: (b, g, i, 0)),
        ],
        out_shape=[
            jax.ShapeDtypeStruct((bsz, seq, D_MODEL), F32),
            jax.ShapeDtypeStruct((bsz, NSA_GROUPS, seq, LANES), BF16),
        ],
        compiler_params=_cparams(("parallel", "parallel", "arbitrary")),
        name="cmp_attn",
    )(q, kc, vc, gates, ovt)


CS_BLK = 256


def _cumsum_kernel(z_ref, b_ref, c_ref):
    seq, width = z_ref.shape
    tri = (lax.broadcasted_iota(jnp.int32, (CS_BLK, CS_BLK), 0)
           >= lax.broadcasted_iota(jnp.int32, (CS_BLK, CS_BLK), 1)).astype(BF16)
    carry = jnp.zeros((1, width), F32)
    for blk in range(seq // CS_BLK):
        x = jax.nn.log_sigmoid(z_ref[blk * CS_BLK:(blk + 1) * CS_BLK, :] + b_ref[...])
        hi = x.astype(BF16)
        r1 = x - hi.astype(F32)
        mid = r1.astype(BF16)
        lo = (r1 - mid.astype(F32)).astype(BF16)
        cs = _dot(tri, hi) + _dot(tri, mid) + _dot(tri, lo) + carry
        c_ref[blk * CS_BLK:(blk + 1) * CS_BLK, :] = cs
        carry = cs[CS_BLK - 1:CS_BLK, :]


def _cumsum(z, bias):
    seq, width = z.shape
    return pl.pallas_call(
        _cumsum_kernel,
        grid=(width // LANES,),
        in_specs=[pl.BlockSpec((seq, LANES), lambda j: (0, j)),
                  pl.BlockSpec((1, LANES), lambda j: (0, j))],
        out_specs=pl.BlockSpec((seq, LANES), lambda j: (0, j)),
        out_shape=jax.ShapeDtypeStruct((seq, width), F32),
        compiler_params=_cparams(("parallel",)),
        name="cumsum",
    )(z, bias)


def _rope_tables(seq):
    inv = ROPE_THETA ** (-jnp.arange(0, HEAD_DIM, 2, dtype=F32) / HEAD_DIM)
    ang = jnp.arange(seq, dtype=F32)[:, None] * inv[None, :]
    cos, sin = jnp.cos(ang), jnp.sin(ang)
    cos_t = jnp.tile(cos, (1, LANES // (HEAD_DIM // 2)))
    sin_t = jnp.tile(jnp.concatenate([-sin, sin], axis=1), (1, LANES // HEAD_DIM))
    return cos_t, sin_t


def _overlap_t(seq):
    n_cmp = (seq - NSA_CMP_LEN) // NSA_CMP_STRIDE + 1
    n_slc = seq // NSA_SLC_LEN
    cs = np.arange(n_cmp) * NSA_CMP_STRIDE
    ce = cs + NSA_CMP_LEN
    ss = np.arange(n_slc) * NSA_SLC_LEN
    se = ss + NSA_SLC_LEN
    ov = np.clip(np.minimum(ce[:, None], se[None, :]) - np.maximum(cs[:, None], ss[None, :]), 0, None)
    ov = (ov / NSA_CMP_LEN).astype(np.float32)
    ovt = np.zeros((n_slc, LANES), np.float32)
    ovt[:, :n_cmp] = ov.T
    return jnp.asarray(ovt, dtype=BF16)


def _expand_mat(seq, tk):
    key = np.arange(seq)
    e = (np.arange(LANES)[:, None] == (key // NSA_SLC_LEN)[None, :]).astype(np.float32)
    e = e.reshape(LANES, seq // tk, tk).transpose(1, 0, 2)
    return jnp.asarray(e, dtype=BF16)


def _compress_weights(pe, w1, w2):
    g = NSA_GROUPS
    half = NSA_CMP_LEN // 2
    eye = jnp.eye(g, dtype=F32)
    w1r = w1.reshape(NSA_CMP_LEN, HEAD_DIM, NSA_CMP_HIDDEN)

    def big(part):
        return jnp.einsum('ldj,gh->lgdhj', part, eye).reshape(half * g * HEAD_DIM, g * NSA_CMP_HIDDEN)

    w1big = jnp.concatenate([big(w1r[:half]), big(w1r[half:])], axis=1).astype(BF16)
    w2big = jnp.einsum('jd,gh->gjhd', w2, eye).reshape(g * NSA_CMP_HIDDEN, g * HEAD_DIM).astype(BF16)

    def pebig(part):
        return jnp.broadcast_to(part[:, None, :], (half, g, HEAD_DIM)).reshape(-1)

    pe2 = jnp.stack([pebig(pe[:half]), pebig(pe[half:])], axis=0)
    pe2 = jnp.concatenate([pe2, jnp.zeros((6, pe2.shape[1]), F32)], axis=0)
    return pe2, w1big, w2big


def _pad_cols(w, width):
    return jnp.concatenate([w, jnp.zeros((w.shape[0], width - w.shape[1]), w.dtype)], axis=1)


ATT_T = 256


def _nsa_mixer(x2, bsz, seq, gain, w_in, ck_pe, ck_w1, ck_w2, cv_pe, cv_w1, cv_w2, w_out,
               rope_tabs, ovt, emat):
    qw = N_HEADS * HEAD_DIM
    gd = NSA_GROUPS * HEAD_DIM

    def kvcols(c, s):
        lo = qw + (c * 2 + s) * gd
        return w_in[:, lo:lo + gd]

    w = jnp.concatenate(
        [w_in[:, :qw]] + [kvcols(c, 0) for c in range(3)] + [kvcols(c, 1) for c in range(3)]
        + [_pad_cols(w_in[:, qw + 6 * gd:], LANES)], axis=1).astype(BF16)
    scale = HEAD_DIM ** -0.5
    groups = ((0, qw, True, scale),
              (qw, gd, True, 1.0), (qw + gd, gd, True, 1.0), (qw + 2 * gd, gd, True, 1.0),
              (qw + 3 * gd, gd, False, 1.0), (qw + 4 * gd, gd, False, 1.0),
              (qw + 5 * gd, gd, False, 1.0), (qw + 6 * gd, LANES, False, 1.0))
    dtypes = (BF16, F32, BF16, BF16, F32, BF16, BF16, F32)
    q, k0, k1, k2, v0, v1, v2, gates = _proj(x2, gain, w, groups, dtypes, rope_tabs, seq)
    r3 = lambda a: a.reshape(bsz, seq, a.shape[-1])
    q, k1, k2, v1, v2, gates = map(r3, (q, k1, k2, v1, v2, gates))

    chunkw = NSA_CMP_STRIDE * gd
    xk = k0.reshape(bsz, seq // NSA_CMP_STRIDE, chunkw)
    xv = v0.reshape(bsz, seq // NSA_CMP_STRIDE, chunkw)
    pek, wk1, wk2 = _compress_weights(ck_pe, ck_w1, ck_w2)
    pev, wv1, wv2 = _compress_weights(cv_pe, cv_w1, cv_w2)
    kc, vc = _compress(xk, xv, pek, pev, wk1, wv1, wk2, wv2)

    o_cmp, sel = _cmp_attn(q, kc, vc, gates, ovt)
    kv_tile = lambda g: g // 2
    o_slc = _flash(q, k1, v1, nt=2, tq=ATT_T, tk=ATT_T, kv_tile=kv_tile, gqa=True, out_dtype=F32,
                   sel=sel, emat=emat, gates=gates, gate_branch=1)
    o_win = _flash(q, k2, v2, nt=2, tq=ATT_T, tk=ATT_T, kv_tile=kv_tile, gqa=True, out_dtype=F32,
                   window=NSA_WINDOW, gates=gates, gate_branch=2)
    flat = lambda a: a.reshape(bsz * seq, D_MODEL)
    return _outproj([flat(o_cmp), flat(o_slc), flat(o_win)], w_out.astype(BF16), x2)


def _swa_mixer(x2, bsz, seq, gain, w_in, sinks, w_out, rope_tabs):
    qw = N_HEADS * HEAD_DIM
    scale = HEAD_DIM ** -0.5
    groups = ((0, qw, True, scale), (qw, LANES, True, 1.0), (qw + LANES, LANES, False, 1.0))
    q, k, v = _proj(x2, gain, w_in.astype(BF16), groups, (BF16, BF16, BF16), rope_tabs, seq)
    r3 = lambda a: a.reshape(bsz, seq, a.shape[-1])
    o = _flash(r3(q), r3(k), r3(v), nt=4, tq=SWA_WINDOW, tk=SWA_WINDOW, kv_tile=lambda g: 0,
               gqa=True, out_dtype=BF16, window=SWA_WINDOW, sinks=sinks)
    return _outproj([o.reshape(bsz * seq, D_MODEL)], w_out.astype(BF16), x2)


def _fox_mixer(x2, bsz, seq, gain, w_in, b_f, w_out):
    qw = N_HEADS * HEAD_DIM
    scale = HEAD_DIM ** -0.5
    w = _pad_cols(w_in, 3 * qw + LANES).astype(BF16)
    groups = ((0, qw, False, scale), (qw, qw, False, 1.0), (2 * qw, qw, False, 1.0),
              (3 * qw, LANES, False, 1.0))
    q, k, v, f = _proj(x2, gain, w, groups, (BF16, BF16, BF16, F32))
    r3 = lambda a: a.reshape(bsz, seq, a.shape[-1])

    bh = bsz * N_HEADS
    bhp = -(-bh // LANES) * LANES
    z = f.reshape(bsz, seq, LANES)[:, :, :N_HEADS].transpose(1, 0, 2).reshape(seq, bh)
    z = _pad_cols(z, bhp)
    bias = _pad_cols(jnp.tile(b_f, bsz).reshape(1, bh), bhp)
    c = _cumsum(z, bias)

    tk = ATT_T
    crow = c[:, :bh].T.reshape(bsz, N_HEADS // 2, 2, seq // tk, tk).transpose(0, 1, 3, 2, 4)
    crow = jnp.concatenate([crow, jnp.zeros(crow.shape[:3] + (6, tk), F32)], axis=3)
    o = _flash(r3(q), r3(k), r3(v), nt=1, tq=ATT_T, tk=tk, kv_tile=lambda g: g, gqa=False,
               out_dtype=BF16, crow=crow, ccol=c)
    return _outproj([o.reshape(bsz * seq, D_MODEL)], w_out.astype(BF16), x2)


def kernel(x, ffn1_norm, ffn1_w_gu, ffn1_w_down, mix_norm, ffn2_norm, ffn2_w_gu, ffn2_w_down,
           nsa_w_in, nsa_ck_pe, nsa_ck_w1, nsa_ck_w2, nsa_cv_pe, nsa_cv_w1, nsa_cv_w2, nsa_w_out,
           swa_w_in, swa_sinks, swa_w_out, fox_w_in, fox_b_f, fox_w_out, final_norm):
    bsz, seq, _ = x.shape
    depth = ffn1_norm.shape[0]
    rope_tabs = _rope_tables(seq)
    ovt = _overlap_t(seq)
    emat = _expand_mat(seq, ATT_T)
    x2 = x.reshape(bsz * seq, D_MODEL)
    for i in range(depth):
        kind, j = i % N_MIXERS, i // N_MIXERS
        x2 = _ffn(x2, ffn1_norm[i], ffn1_w_gu[i].astype(BF16), ffn1_w_down[i].astype(BF16))
        if kind == 0:
            x2 = _nsa_mixer(x2, bsz, seq, mix_norm[i], nsa_w_in[j], nsa_ck_pe[j], nsa_ck_w1[j],
                            nsa_ck_w2[j], nsa_cv_pe[j], nsa_cv_w1[j], nsa_cv_w2[j], nsa_w_out[j],
                            rope_tabs, ovt, emat)
        elif kind == 1:
            x2 = _swa_mixer(x2, bsz, seq, mix_norm[i], swa_w_in[j], swa_sinks[j], swa_w_out[j],
                            rope_tabs)
        else:
            x2 = _fox_mixer(x2, bsz, seq, mix_norm[i], fox_w_in[j], fox_b_f[j], fox_w_out[j])
        last = i == depth - 1
        x2 = _ffn(x2, ffn2_norm[i], ffn2_w_gu[i].astype(BF16), ffn2_w_down[i].astype(BF16),
                  final_gain=final_norm if last else None)
    return x2.reshape(bsz, seq, D_MODEL)
```

```python
import functools

import numpy as np
import jax
import jax.numpy as jnp
from jax import lax
from jax.experimental import pallas as pl
from jax.experimental.pallas import tpu as pltpu

D_MODEL = 1024
HEAD_DIM = 64
N_HEADS = 16
D_FF = 2816
RMS_EPS = 1e-6
ROPE_THETA = 10000.0
NEG = -1e30

NSA_GROUPS = 4
NSA_CMP_LEN = 32
NSA_CMP_STRIDE = 16
NSA_CMP_HIDDEN = 128
NSA_SLC_LEN = 64
NSA_TOPK = 16
NSA_WINDOW = 512
NSA_FORCE_BONUS = 1e4
SWA_WINDOW = 128
N_MIXERS = 3

LANES = 128
HALF = 64
VMEM_LIMIT = 56 * 1024 * 1024

F32 = jnp.float32
BF16 = jnp.bfloat16


def _nt_dot(a, b):
    return lax.dot_general(a, b, (((1,), (1,)), ((), ())), preferred_element_type=F32)


def _dot(a, b):
    return jnp.dot(a, b, preferred_element_type=F32)


def _rms(x, g):
    ms = jnp.mean(x * x, axis=-1, keepdims=True)
    return x * lax.rsqrt(ms + RMS_EPS) * g


def _cparams(sem):
    return pltpu.CompilerParams(dimension_semantics=sem, vmem_limit_bytes=VMEM_LIMIT)


FFN_TM = 512
FFN_TF = 256


def _ffn_kernel(x_ref, g_ref, wgu_ref, wd_ref, *rest, final):
    if final:
        fg_ref, o_ref = rest
    else:
        (o_ref,) = rest
    x = x_ref[...]
    hb = _rms(x, g_ref[...]).astype(BF16)
    acc = jnp.zeros(x.shape, F32)
    for f in range(D_FF // FFN_TF):
        lo = f * FFN_TF
        g = _dot(hb, wgu_ref[:, lo:lo + FFN_TF])
        u = _dot(hb, wgu_ref[:, D_FF + lo:D_FF + lo + FFN_TF])
        a = (g * jax.nn.sigmoid(g)) * u
        acc = acc + _dot(a.astype(BF16), wd_ref[lo:lo + FFN_TF, :])
    y = x + 0.5 * acc
    if final:
        y = _rms(y, fg_ref[...])
    o_ref[...] = y


def _ffn(x2, gain, wgu, wd, final_gain=None):
    n = x2.shape[0]
    final = final_gain is not None
    resident = dict(pipeline_mode=pl.Buffered(1))
    in_specs = [
        pl.BlockSpec((FFN_TM, D_MODEL), lambda i: (i, 0)),
        pl.BlockSpec((1, D_MODEL), lambda i: (0, 0)),
        pl.BlockSpec((D_MODEL, 2 * D_FF), lambda i: (0, 0), **resident),
        pl.BlockSpec((D_FF, D_MODEL), lambda i: (0, 0), **resident),
    ]
    args = [x2, gain.reshape(1, D_MODEL), wgu, wd]
    if final:
        in_specs.append(pl.BlockSpec((1, D_MODEL), lambda i: (0, 0)))
        args.append(final_gain.reshape(1, D_MODEL))
    return pl.pallas_call(
        functools.partial(_ffn_kernel, final=final),
        grid=(n // FFN_TM,),
        in_specs=in_specs,
        out_specs=pl.BlockSpec((FFN_TM, D_MODEL), lambda i: (i, 0)),
        out_shape=jax.ShapeDtypeStruct((n, D_MODEL), F32),
        compiler_params=_cparams(("parallel",)),
        name="ffn",
    )(*args)


PROJ_TM = 512
PROJ_CH = 256


def _rope_tile(y, cos_t, sin_t, first_half):
    rot = jnp.where(first_half, pltpu.roll(y, 96, 1), pltpu.roll(y, 32, 1))
    return y * cos_t + rot * sin_t


def _proj_kernel(*refs, row_groups, t_groups, use_rope):
    it = iter(refs)
    x_ref = next(it)
    g_ref = next(it)
    w_ref = next(it)
    wt_ref = next(it) if t_groups else None
    if use_rope:
        cos_t = next(it)[...]
        sin_t = next(it)[...]
        lane = lax.broadcasted_iota(jnp.int32, (1, LANES), 1)
        first_half = (lane % HALF) < (HALF // 2)
    o_refs = list(it)
    hb = _rms(x_ref[...], g_ref[...]).astype(BF16)
    tm = hb.shape[0]
    for (c0, width, rope, scale), o_ref in zip(row_groups, o_refs):
        for t0 in range(0, width, PROJ_CH):
            ch = min(PROJ_CH, width - t0)
            y = _dot(hb, w_ref[:, c0 + t0:c0 + t0 + ch])
            for l0 in range(0, ch, LANES):
                yt = y[:, l0:l0 + LANES]
                if rope:
                    yt = _rope_tile(yt, cos_t, sin_t, first_half)
                if scale != 1.0:
                    yt = yt * scale
                o_ref[:, t0 + l0:t0 + l0 + LANES] = yt.astype(o_ref.dtype)
    for (c0, width, ck), o_ref in zip(t_groups, o_refs[len(row_groups):]):
        for t0 in range(0, width, PROJ_CH):
            ch = min(PROJ_CH, width - t0)
            yt = _nt_dot(wt_ref[c0 + t0:c0 + t0 + ch, :], hb)
            for s0 in range(0, tm, ck):
                o_ref[0, s0 // ck, t0:t0 + ch, :] = yt[:, s0:s0 + ck].astype(o_ref.dtype)


def _proj(x2, gain, w, row_groups, row_dtypes, seq, wt=None, t_groups=(), t_dtypes=(),
          rope_tabs=None):
    n = x2.shape[0]
    bsz = n // seq
    nblk = seq // PROJ_TM
    use_rope = rope_tabs is not None
    resident = dict(pipeline_mode=pl.Buffered(1))
    in_specs = [
        pl.BlockSpec((PROJ_TM, D_MODEL), lambda i: (i, 0)),
        pl.BlockSpec((1, D_MODEL), lambda i: (0, 0)),
        pl.BlockSpec(w.shape, lambda i: (0, 0), **resident),
    ]
    args = [x2, gain.reshape(1, D_MODEL), w]
    if t_groups:
        in_specs.append(pl.BlockSpec(wt.shape, lambda i: (0, 0), **resident))
        args.append(wt)
    if use_rope:
        in_specs += [pl.BlockSpec((PROJ_TM, LANES), lambda i: (i % nblk, 0))] * 2
        args += list(rope_tabs)
    out_specs = [pl.BlockSpec((PROJ_TM, g[1]), lambda i: (i, 0)) for g in row_groups]
    out_shape = [jax.ShapeDtypeStruct((n, g[1]), dt) for g, dt in zip(row_groups, row_dtypes)]
    for (_, width, ck), dt in zip(t_groups, t_dtypes):
        out_specs.append(pl.BlockSpec((1, PROJ_TM // ck, width, ck),
                                      lambda i: (i // nblk, i % nblk, 0, 0)))
        out_shape.append(jax.ShapeDtypeStruct((bsz, seq // ck, width, ck), dt))
    return pl.pallas_call(
        functools.partial(_proj_kernel, row_groups=row_groups, t_groups=t_groups,
                          use_rope=use_rope),
        grid=(n // PROJ_TM,),
        in_specs=in_specs,
        out_specs=out_specs,
        out_shape=out_shape,
        compiler_params=_cparams(("parallel",)),
        name="proj",
    )(*args)


OUT_TM = 512


def _outproj_kernel(*refs, n_in):
    o_refs = refs[:n_in]
    w_ref, x_ref, out_ref = refs[n_in:]
    o = o_refs[0][...].astype(F32)
    for r in o_refs[1:]:
        o = o + r[...].astype(F32)
    out_ref[...] = x_ref[...] + _dot(o.astype(BF16), w_ref[...])


def _outproj(os_, w, x2):
    n = x2.shape[0]
    n_in = len(os_)
    row = pl.BlockSpec((OUT_TM, D_MODEL), lambda i: (i, 0))
    return pl.pallas_call(
        functools.partial(_outproj_kernel, n_in=n_in),
        grid=(n // OUT_TM,),
        in_specs=[row] * n_in + [pl.BlockSpec((D_MODEL, D_MODEL), lambda i: (0, 0)), row],
        out_specs=row,
        out_shape=jax.ShapeDtypeStruct((n, D_MODEL), F32),
        compiler_params=_cparams(("parallel",)),
        name="outproj",
    )(*os_, w, x2)


def _align_queries(q_ref, qal_ref, nt, kv_half):
    lane = lax.broadcasted_iota(jnp.int32, (1, LANES), 1)
    half = lane // HALF
    for tt in range(nt):
        qt = q_ref[0, :, tt * LANES:(tt + 1) * LANES].astype(F32)
        if kv_half is not None:
            qr = pltpu.roll(qt, HALF, 1)
        for a in range(2):
            if kv_half is not None:
                qa = jnp.where(half == kv_half, jnp.where(kv_half == a, qt, qr), 0.0)
            else:
                qa = jnp.where(half == a, qt, 0.0)
            qal_ref[2 * tt + a] = qa.astype(BF16)


def _pick_half(x, kv_half, a):
    if kv_half is None:
        return x[a * HALF:(a + 1) * HALF]
    return jnp.where(kv_half == 0, x[:HALF], x[HALF:])


def _flash_kernel(*refs, nt, tq, tk, window, use_sel, use_fox, use_sink, gate_branch, gqa):
    it = iter(refs)
    q_ref = next(it)
    k_ref = next(it)
    vt_ref = next(it)
    if use_sel:
        selt_ref = next(it)
        et_ref = next(it)
    if use_fox:
        ct_ref = next(it)
        cs_ref = next(it)
    if use_sink:
        sink_ref = next(it)
    if gate_branch is not None:
        gt_ref = next(it)
    o_ref = next(it)
    qal_ref = next(it)
    m_ref = next(it)
    acc_ref = next(it)

    hg = pl.program_id(1)
    i = pl.program_id(2)
    nh = 2 * nt
    kv_half = (jnp.zeros((1, 1), jnp.int32) + (hg % 2)) if gqa else None
    _align_queries(q_ref, qal_ref, nt, kv_half)

    m_ref[...] = jnp.full(m_ref.shape, NEG, F32)
    acc_ref[...] = jnp.zeros(acc_ref.shape, F32)

    row_half = lax.broadcasted_iota(jnp.int32, (LANES, 1), 0) // HALF
    q0 = i * tq
    colrow = (lax.broadcasted_iota(jnp.int32, (tk, tq), 1)
              - lax.broadcasted_iota(jnp.int32, (tk, tq), 0))
    if use_sel:
        selt = selt_ref[0, 0]
    if use_fox:
        ct = ct_ref[0, 0, i]

    if window is None:
        c_lo = 0
    else:
        c_lo = jnp.maximum(q0 - (window - 1), 0) // tk
    c_hi = (q0 + tq + tk - 1) // tk

    def chunk(c, carry):
        k0 = pl.multiple_of(c * tk, tk)
        kc = k_ref[0, pl.ds(k0, tk), :]
        vt = vt_ref[0, c]
        d = colrow + (q0 - k0)
        mask = d >= 0
        if window is not None:
            mask = mask & (d < window)
        if use_sel:
            mask = mask & (_dot(et_ref[c], selt) > 0.5)
        if gqa:
            vth_all = jnp.where(row_half == kv_half, vt, jnp.ones_like(vt))
        for h in range(nh):
            if gqa:
                vth = vth_all
            else:
                vth = jnp.where(row_half == (h % 2), vt, jnp.ones_like(vt))
            s = _nt_dot(kc, qal_ref[h])
            if use_fox:
                cs = cs_ref[0, h, pl.ds(k0, tk), :]
                s = s + (ct[h:h + 1, :] - jnp.concatenate([cs] * (tq // LANES), axis=1))
            s = jnp.where(mask, s, NEG)
            m_old = m_ref[h]
            m_new = jnp.maximum(m_old, jnp.max(s, axis=0, keepdims=True))
            alpha = jnp.exp(m_old - m_new)
            p = jnp.exp(s - m_new).astype(BF16)
            acc_ref[h] = alpha * acc_ref[h] + _dot(vth, p)
            m_ref[h] = m_new
        return carry

    lax.fori_loop(c_lo, c_hi, chunk, 0)

    for tt in range(nt):
        outs = []
        for a in range(2):
            h = 2 * tt + a
            acc = acc_ref[h]
            if gqa:
                l = jnp.where(kv_half == 0, acc[HALF:HALF + 1], acc[0:1])
            else:
                l = acc[(1 - a) * HALF:(1 - a) * HALF + 1]
            out = _pick_half(acc, kv_half, a)
            if use_sink:
                m = m_ref[h]
                sk = sink_ref[hg * nh + h]
                m2 = jnp.maximum(m, sk)
                f = jnp.exp(m - m2)
                l = l * f + jnp.exp(sk - m2)
                out = out * f
            out = out * (1.0 / l)
            if gate_branch is not None:
                gi = gate_branch * N_HEADS + hg * nh + h
                out = out * jax.nn.sigmoid(gt_ref[0, 0, pl.ds(gi, 1), :])
            outs.append(out)
        tile_t = jnp.concatenate(outs, axis=0)
        o_ref[0, :, tt * LANES:(tt + 1) * LANES] = tile_t.T.astype(o_ref.dtype)


def _flash(q, k, vt, *, nt, tq, tk, kv_tile, gqa, out_dtype, window=None, selt=None, emat_t=None,
           ct=None, cs=None, sinks=None, gates_t=None, gate_branch=None):
    bsz, seq, qw = q.shape
    n_hg = qw // (nt * LANES)
    nq = seq // tq
    nk = seq // tk
    use_sel = selt is not None
    use_fox = ct is not None
    use_sink = sinks is not None
    in_specs = [
        pl.BlockSpec((1, tq, nt * LANES), lambda b, g, i: (b, i, g)),
        pl.BlockSpec((1, seq, LANES), lambda b, g, i: (b, 0, kv_tile(g))),
        pl.BlockSpec((1, nk, LANES, tk), lambda b, g, i: (b, 0, kv_tile(g), 0)),
    ]
    args = [q, k, vt]
    if use_sel:
        in_specs += [
            pl.BlockSpec((1, 1, LANES, tq), lambda b, g, i: (b, g, 0, i)),
            pl.BlockSpec(emat_t.shape, lambda b, g, i: (0, 0, 0)),
        ]
        args += [selt, emat_t]
    if use_fox:
        in_specs += [
            pl.BlockSpec((1, 1) + ct.shape[2:], lambda b, g, i: (b, g, 0, 0, 0)),
            pl.BlockSpec((1, 2, seq, LANES), lambda b, g, i: (b, g, 0, 0)),
        ]
        args += [ct, cs]
    if use_sink:
        in_specs.append(pl.BlockSpec(memory_space=pltpu.SMEM))
        args.append(sinks)
    if gate_branch is not None:
        in_specs.append(pl.BlockSpec((1, 1, LANES, tq), lambda b, g, i: (b, i, 0, 0)))
        args.append(gates_t)
    nh = 2 * nt
    kern = functools.partial(
        _flash_kernel, nt=nt, tq=tq, tk=tk, window=window, use_sel=use_sel, use_fox=use_fox,
        use_sink=use_sink, gate_branch=gate_branch, gqa=gqa)
    return pl.pallas_call(
        kern,
        grid=(bsz, n_hg, nq),
        in_specs=in_specs,
        out_specs=pl.BlockSpec((1, tq, nt * LANES), lambda b, g, i: (b, i, g)),
        out_shape=jax.ShapeDtypeStruct((bsz, seq, qw), out_dtype),
        scratch_shapes=[
            pltpu.VMEM((nh, tq, LANES), BF16),
            pltpu.VMEM((nh, 1, tq), F32),
            pltpu.VMEM((nh, LANES, tq), F32),
        ],
        compiler_params=_cparams(("parallel", "parallel", "arbitrary")),
        name="flash",
    )(*args)


N_CHUNK16 = 128


def _compress_hidden(x_ref, pe_ref, w1_ref):
    hid = NSA_GROUPS * NSA_CMP_HIDDEN
    x = x_ref[0]
    top = _dot((x + pe_ref[0:1, :]).astype(BF16), w1_ref[:, :hid])
    bot = _dot((x + pe_ref[1:2, :]).astype(BF16), w1_ref[:, hid:])
    h1 = top + pltpu.roll(bot, N_CHUNK16 - 1, 0)
    return jax.nn.gelu(h1, approximate=True).astype(BF16)


def _compress_kernel(xk_ref, xv_ref, pek_ref, pev_ref, wk1_ref, wv1_ref, wk2_ref, wv2t_ref,
                     kc_ref, vct_ref):
    kc_ref[0] = _dot(_compress_hidden(xk_ref, pek_ref, wk1_ref), wk2_ref[...]).astype(kc_ref.dtype)
    vct_ref[0] = _nt_dot(wv2t_ref[...], _compress_hidden(xv_ref, pev_ref, wv1_ref)).astype(vct_ref.dtype)


def _compress(xk, xv, pek, pev, wk1, wv1, wk2, wv2t):
    bsz = xk.shape[0]
    width = xk.shape[2]
    gd = NSA_GROUPS * HEAD_DIM
    xspec = pl.BlockSpec((1, N_CHUNK16, width), lambda b: (b, 0, 0))
    full = lambda a: pl.BlockSpec(a.shape, lambda b: (0,) * a.ndim)
    return pl.pallas_call(
        _compress_kernel,
        grid=(bsz,),
        in_specs=[xspec, xspec, full(pek), full(pev), full(wk1), full(wv1), full(wk2), full(wv2t)],
        out_specs=[pl.BlockSpec((1, N_CHUNK16, gd), lambda b: (b, 0, 0)),
                   pl.BlockSpec((1, gd, N_CHUNK16), lambda b: (b, 0, 0))],
        out_shape=[jax.ShapeDtypeStruct((bsz, N_CHUNK16, gd), BF16),
                   jax.ShapeDtypeStruct((bsz, gd, N_CHUNK16), BF16)],
        compiler_params=_cparams(("parallel",)),
        name="compress",
    )(xk, xv, pek, pev, wk1, wv1, wk2, wv2t)


CMP_TQ = 256
N_SLC = 32


def _cmp_kernel(q_ref, kc_ref, vct_ref, gt_ref, ovt_ref, o_ref, selt_ref, qal_ref):
    tq = CMP_TQ
    g = pl.program_id(1)
    i = pl.program_id(2)
    kv_half = jnp.zeros((1, 1), jnp.int32) + (g % 2)
    _align_queries(q_ref, qal_ref, 2, kv_half)
    kc = kc_ref[0]
    vct = vct_ref[0]
    t_row = i * tq + lax.broadcasted_iota(jnp.int32, (1, tq), 1)
    n_col = lax.broadcasted_iota(jnp.int32, (N_CHUNK16, 1), 0)
    n_cmp = (N_CHUNK16 * NSA_CMP_STRIDE - NSA_CMP_LEN) // NSA_CMP_STRIDE + 1
    valid = (n_col * NSA_CMP_STRIDE + (NSA_CMP_LEN - 1) <= t_row) & (n_col < n_cmp)

    psum = jnp.zeros((N_CHUNK16, tq), F32)
    for tt in range(2):
        outs = []
        for a in range(2):
            h = 2 * tt + a
            s = jnp.where(valid, _nt_dot(kc, qal_ref[h]), NEG)
            m = jnp.max(s, axis=0, keepdims=True)
            e = jnp.where(valid, jnp.exp(s - m), 0.0)
            l = jnp.sum(e, axis=0, keepdims=True)
            p = e * (1.0 / jnp.where(l > 0.0, l, 1.0))
            psum = psum + p
            out = _pick_half(_dot(vct, p.astype(BF16)), kv_half, a)
            outs.append(out * jax.nn.sigmoid(gt_ref[0, 0, pl.ds(g * 4 + h, 1), :]))
        tile_t = jnp.concatenate(outs, axis=0)
        o_ref[0, :, tt * LANES:(tt + 1) * LANES] = tile_t.T

    p_hi = psum.astype(BF16)
    p_lo = (psum - p_hi.astype(F32)).astype(BF16)
    ovt = ovt_ref[...]
    imp = _dot(ovt, p_hi) + _dot(ovt, p_lo)
    j = lax.broadcasted_iota(jnp.int32, (N_SLC, 1), 0)
    tb = jnp.right_shift(t_row, 6)
    forced = (j == 0) | (j == tb) | (j == tb - 1)
    imp = jnp.where(j > tb, NEG, jnp.where(forced, NSA_FORCE_BONUS, imp))
    cnt = jnp.zeros((N_SLC, tq), jnp.int32)
    for jp in range(N_SLC):
        row = imp[jp:jp + 1, :]
        beats = (row > imp) | ((row == imp) & (jp < j))
        cnt = cnt + beats.astype(jnp.int32)
    sel_t = (cnt < NSA_TOPK).astype(F32)
    sel_t = jnp.concatenate([sel_t, jnp.zeros((LANES - N_SLC, tq), F32)], axis=0)
    selt_ref[0, 0] = sel_t.astype(selt_ref.dtype)


def _cmp_attn(q, kc, vct, gates_t, ovt):
    bsz, seq, _ = q.shape
    tq = CMP_TQ
    gw = 2 * LANES
    return pl.pallas_call(
        _cmp_kernel,
        grid=(bsz, NSA_GROUPS, seq // tq),
        in_specs=[
            pl.BlockSpec((1, tq, gw), lambda b, g, i: (b, i, g)),
            pl.BlockSpec((1, N_CHUNK16, LANES), lambda b, g, i: (b, 0, g // 2)),
            pl.BlockSpec((1, LANES, N_CHUNK16), lambda b, g, i: (b, g // 2, 0)),
            pl.BlockSpec((1, 1, LANES, tq), lambda b, g, i: (b, i, 0, 0)),
            pl.BlockSpec(ovt.shape, lambda b, g, i: (0, 0)),
        ],
        out_specs=[
            pl.BlockSpec((1, tq, gw), lambda b, g, i: (b, i, g)),
            pl.BlockSpec((1, 1, LANES, tq), lambda b, g, i: (b, g, 0, i)),
        ],
        out_shape=[
            jax.ShapeDtypeStruct((bsz, seq, D_MODEL), F32),
            jax.ShapeDtypeStruct((bsz, NSA_GROUPS, LANES, seq), BF16),
        ],
        scratch_shapes=[pltpu.VMEM((4, tq, LANES), BF16)],
        compiler_params=_cparams(("parallel", "parallel", "arbitrary")),
        name="cmp_attn",
    )(q, kc, vct, gates_t, ovt)


CS_BLK = 256


def _cumsum_kernel(z_ref, b_ref, c_ref):
    seq, width = z_ref.shape
    tri = (lax.broadcasted_iota(jnp.int32, (CS_BLK, CS_BLK), 0)
           >= lax.broadcasted_iota(jnp.int32, (CS_BLK, CS_BLK), 1)).astype(BF16)
    carry = jnp.zeros((1, width), F32)
    for blk in range(seq // CS_BLK):
        x = jax.nn.log_sigmoid(z_ref[blk * CS_BLK:(blk + 1) * CS_BLK, :] + b_ref[...])
        hi = x.astype(BF16)
        r1 = x - hi.astype(F32)
        mid = r1.astype(BF16)
        lo = (r1 - mid.astype(F32)).astype(BF16)
        cs = _dot(tri, hi) + _dot(tri, mid) + _dot(tri, lo) + carry
        c_ref[blk * CS_BLK:(blk + 1) * CS_BLK, :] = cs
        carry = cs[CS_BLK - 1:CS_BLK, :]


def _cumsum(z, bias):
    seq, width = z.shape
    return pl.pallas_call(
        _cumsum_kernel,
        grid=(width // LANES,),
        in_specs=[pl.BlockSpec((seq, LANES), lambda j: (0, j)),
                  pl.BlockSpec((1, LANES), lambda j: (0, j))],
        out_specs=pl.BlockSpec((seq, LANES), lambda j: (0, j)),
        out_shape=jax.ShapeDtypeStruct((seq, width), F32),
        compiler_params=_cparams(("parallel",)),
        name="cumsum",
    )(z, bias)


def _lane_bcast_kernel(c_ref, o_ref):
    li = pl.program_id(0) % LANES
    c = c_ref[...]
    lane = lax.broadcasted_iota(jnp.int32, c.shape, 1)
    col = jnp.sum(jnp.where(lane == li, c, 0.0), axis=1, keepdims=True)
    o_ref[0] = jnp.broadcast_to(col, c.shape)


def _lane_bcast(c, n):
    seq = c.shape[0]
    return pl.pallas_call(
        _lane_bcast_kernel,
        grid=(n,),
        in_specs=[pl.BlockSpec((seq, LANES), lambda j: (0, j // LANES))],
        out_specs=pl.BlockSpec((1, seq, LANES), lambda j: (j, 0, 0)),
        out_shape=jax.ShapeDtypeStruct((n, seq, LANES), F32),
        compiler_params=_cparams(("parallel",)),
        name="lane_bcast",
    )(c)


def _rope_tables(seq):
    inv = ROPE_THETA ** (-jnp.arange(0, HEAD_DIM, 2, dtype=F32) / HEAD_DIM)
    ang = jnp.arange(seq, dtype=F32)[:, None] * inv[None, :]
    cos, sin = jnp.cos(ang), jnp.sin(ang)
    cos_t = jnp.tile(cos, (1, LANES // (HEAD_DIM // 2)))
    sin_t = jnp.tile(jnp.concatenate([-sin, sin], axis=1), (1, LANES // HEAD_DIM))
    return cos_t, sin_t


def _overlap_t(seq):
    n_cmp = (seq - NSA_CMP_LEN) // NSA_CMP_STRIDE + 1
    n_slc = seq // NSA_SLC_LEN
    cs = np.arange(n_cmp) * NSA_CMP_STRIDE
    ce = cs + NSA_CMP_LEN
    ss = np.arange(n_slc) * NSA_SLC_LEN
    se = ss + NSA_SLC_LEN
    ov = np.clip(np.minimum(ce[:, None], se[None, :]) - np.maximum(cs[:, None], ss[None, :]), 0, None)
    ov = (ov / NSA_CMP_LEN).astype(np.float32)
    ovt = np.zeros((n_slc, LANES), np.float32)
    ovt[:, :n_cmp] = ov.T
    return jnp.asarray(ovt, dtype=BF16)


def _expand_mat_t(seq, tk):
    key = np.arange(seq)
    e = ((key // NSA_SLC_LEN)[:, None] == np.arange(LANES)[None, :]).astype(np.float32)
    return jnp.asarray(e.reshape(seq // tk, tk, LANES), dtype=BF16)


def _compress_weights(pe, w1, w2):
    g = NSA_GROUPS
    half = NSA_CMP_LEN // 2
    eye = jnp.eye(g, dtype=F32)
    w1r = w1.reshape(NSA_CMP_LEN, HEAD_DIM, NSA_CMP_HIDDEN)

    def big(part):
        return jnp.einsum('ldj,gh->lgdhj', part, eye).reshape(half * g * HEAD_DIM, g * NSA_CMP_HIDDEN)

    w1big = jnp.concatenate([big(w1r[:half]), big(w1r[half:])], axis=1).astype(BF16)
    w2big = jnp.einsum('jd,gh->gjhd', w2, eye).reshape(g * NSA_CMP_HIDDEN, g * HEAD_DIM).astype(BF16)

    def pebig(part):
        return jnp.broadcast_to(part[:, None, :], (half, g, HEAD_DIM)).reshape(-1)

    pe2 = jnp.stack([pebig(pe[:half]), pebig(pe[half:])], axis=0)
    pe2 = jnp.concatenate([pe2, jnp.zeros((6, pe2.shape[1]), F32)], axis=0)
    return pe2, w1big, w2big


def _pad_cols(w, width):
    return jnp.concatenate([w, jnp.zeros((w.shape[0], width - w.shape[1]), w.dtype)], axis=1)


ATT_T = 256


def _nsa_mixer(x2, bsz, seq, gain, w_in, ck_pe, ck_w1, ck_w2, cv_pe, cv_w1, cv_w2, w_out,
               rope_tabs, ovt, emat_t):
    qw = N_HEADS * HEAD_DIM
    gd = NSA_GROUPS * HEAD_DIM

    def kvcols(c, s):
        lo = qw + (c * 2 + s) * gd
        return w_in[:, lo:lo + gd]

    w = jnp.concatenate([w_in[:, :qw]] + [kvcols(c, 0) for c in range(3)] + [kvcols(0, 1)],
                        axis=1).astype(BF16)
    wt = jnp.concatenate([kvcols(1, 1), kvcols(2, 1), _pad_cols(w_in[:, qw + 6 * gd:], LANES)],
                         axis=1).T.astype(BF16)
    scale = HEAD_DIM ** -0.5
    row_groups = ((0, qw, True, scale), (qw, gd, True, 1.0), (qw + gd, gd, True, 1.0),
                  (qw + 2 * gd, gd, True, 1.0), (qw + 3 * gd, gd, False, 1.0))
    t_groups = ((0, gd, ATT_T), (gd, gd, ATT_T), (2 * gd, LANES, ATT_T))
    q, k0, k1, k2, v0, v1t, v2t, gates_t = _proj(
        x2, gain, w, row_groups, (BF16, F32, BF16, BF16, F32), seq, wt=wt, t_groups=t_groups,
        t_dtypes=(BF16, BF16, F32), rope_tabs=rope_tabs)
    r3 = lambda a: a.reshape(bsz, seq, a.shape[-1])
    q, k1, k2 = map(r3, (q, k1, k2))

    chunkw = NSA_CMP_STRIDE * gd
    xk = k0.reshape(bsz, seq // NSA_CMP_STRIDE, chunkw)
    xv = v0.reshape(bsz, seq // NSA_CMP_STRIDE, chunkw)
    pek, wk1, wk2 = _compress_weights(ck_pe, ck_w1, ck_w2)
    pev, wv1, wv2 = _compress_weights(cv_pe, cv_w1, cv_w2)
    kc, vct = _compress(xk, xv, pek, pev, wk1, wv1, wk2, wv2.T)

    o_cmp, selt = _cmp_attn(q, kc, vct, gates_t, ovt)
    kv_tile = lambda g: g // 2
    o_slc = _flash(q, k1, v1t, nt=2, tq=ATT_T, tk=ATT_T, kv_tile=kv_tile, gqa=True, out_dtype=F32,
                   selt=selt, emat_t=emat_t, gates_t=gates_t, gate_branch=1)
    o_win = _flash(q, k2, v2t, nt=2, tq=ATT_T, tk=ATT_T, kv_tile=kv_tile, gqa=True, out_dtype=F32,
                   window=NSA_WINDOW, gates_t=gates_t, gate_branch=2)
    flat = lambda a: a.reshape(bsz * seq, D_MODEL)
    return _outproj([flat(o_cmp), flat(o_slc), flat(o_win)], w_out.astype(BF16), x2)


def _swa_mixer(x2, bsz, seq, gain, w_in, sinks, w_out, rope_tabs):
    qw = N_HEADS * HEAD_DIM
    scale = HEAD_DIM ** -0.5
    row_groups = ((0, qw, True, scale), (qw, LANES, True, 1.0))
    w = w_in[:, :qw + LANES].astype(BF16)
    wt = w_in[:, qw + LANES:].T.astype(BF16)
    q, k, vt = _proj(x2, gain, w, row_groups, (BF16, BF16), seq, wt=wt,
                     t_groups=((0, LANES, SWA_WINDOW),), t_dtypes=(BF16,), rope_tabs=rope_tabs)
    r3 = lambda a: a.reshape(bsz, seq, a.shape[-1])
    o = _flash(r3(q), r3(k), vt, nt=4, tq=SWA_WINDOW, tk=SWA_WINDOW, kv_tile=lambda g: 0,
               gqa=True, out_dtype=BF16, window=SWA_WINDOW, sinks=sinks)
    return _outproj([o.reshape(bsz * seq, D_MODEL)], w_out.astype(BF16), x2)


def _fox_mixer(x2, bsz, seq, gain, w_in, b_f, w_out):
    qw = N_HEADS * HEAD_DIM
    scale = HEAD_DIM ** -0.5
    w = jnp.concatenate([w_in[:, :2 * qw], _pad_cols(w_in[:, 3 * qw:], LANES)], axis=1).astype(BF16)
    wt = w_in[:, 2 * qw:3 * qw].T.astype(BF16)
    row_groups = ((0, qw, False, scale), (qw, qw, False, 1.0), (2 * qw, LANES, False, 1.0))
    q, k, f, vt = _proj(x2, gain, w, row_groups, (BF16, BF16, F32), seq, wt=wt,
                        t_groups=((0, qw, ATT_T),), t_dtypes=(BF16,))
    r3 = lambda a: a.reshape(bsz, seq, a.shape[-1])

    bh = bsz * N_HEADS
    bhp = -(-bh // LANES) * LANES
    z = f.reshape(bsz, seq, LANES)[:, :, :N_HEADS].transpose(1, 0, 2).reshape(seq, bh)
    z = _pad_cols(z, bhp)
    bias = _pad_cols(jnp.tile(b_f, bsz).reshape(1, bh), bhp)
    c = _cumsum(z, bias)
    cs = _lane_bcast(c, bh).reshape(bsz, N_HEADS, seq, LANES)

    tq = ATT_T
    ct = c[:, :bh].T.reshape(bsz, N_HEADS // 2, 2, seq // tq, tq).transpose(0, 1, 3, 2, 4)
    ct = jnp.concatenate([ct, jnp.zeros(ct.shape[:3] + (6, tq), F32)], axis=3)
    o = _flash(r3(q), r3(k), vt, nt=1, tq=tq, tk=ATT_T, kv_tile=lambda g: g, gqa=False,
               out_dtype=BF16, ct=ct, cs=cs)
    return _outproj([o.reshape(bsz * seq, D_MODEL)], w_out.astype(BF16), x2)


def kernel(x, ffn1_norm, ffn1_w_gu, ffn1_w_down, mix_norm, ffn2_norm, ffn2_w_gu, ffn2_w_down,
           nsa_w_in, nsa_ck_pe, nsa_ck_w1, nsa_ck_w2, nsa_cv_pe, nsa_cv_w1, nsa_cv_w2, nsa_w_out,
           swa_w_in, swa_sinks, swa_w_out, fox_w_in, fox_b_f, fox_w_out, final_norm):
    bsz, seq, _ = x.shape
    depth = ffn1_norm.shape[0]
    rope_tabs = _rope_tables(seq)
    ovt = _overlap_t(seq)
    emat_t = _expand_mat_t(seq, ATT_T)
    x2 = x.reshape(bsz * seq, D_MODEL)
    for i in range(depth):
        kind, j = i % N_MIXERS, i // N_MIXERS
        x2 = _ffn(x2, ffn1_norm[i], ffn1_w_gu[i].astype(BF16), ffn1_w_down[i].astype(BF16))
        if kind == 0:
            x2 = _nsa_mixer(x2, bsz, seq, mix_norm[i], nsa_w_in[j], nsa_ck_pe[j], nsa_ck_w1[j],
                            nsa_ck_w2[j], nsa_cv_pe[j], nsa_cv_w1[j], nsa_cv_w2[j], nsa_w_out[j],
                            rope_tabs, ovt, emat_t)
        elif kind == 1:
            x2 = _swa_mixer(x2, bsz, seq, mix_norm[i], swa_w_in[j], swa_sinks[j], swa_w_out[j],
                            rope_tabs)
        else:
            x2 = _fox_mixer(x2, bsz, seq, mix_norm[i], fox_w_in[j], fox_b_f[j], fox_w_out[j])
        last = i == depth - 1
        x2 = _ffn(x2, ffn2_norm[i], ffn2_w_gu[i].astype(BF16), ffn2_w_down[i].astype(BF16),
                  final_gain=final_norm if last else None)
    return x2.reshape(bsz, seq, D_MODEL)
```

```python
import functools

import numpy as np
import jax
import jax.numpy as jnp
from jax import lax
from jax.experimental import pallas as pl
from jax.experimental.pallas import tpu as pltpu

D_MODEL = 1024
HEAD_DIM = 64
N_HEADS = 16
D_FF = 2816
RMS_EPS = 1e-6
ROPE_THETA = 10000.0
NEG = -1e30
LOG2E = 1.4426950408889634

NSA_GROUPS = 4
NSA_CMP_LEN = 32
NSA_CMP_STRIDE = 16
NSA_CMP_HIDDEN = 128
NSA_SLC_LEN = 64
NSA_TOPK = 16
NSA_WINDOW = 512
NSA_FORCE_BONUS = 1e4
SWA_WINDOW = 128
N_MIXERS = 3

LANES = 128
HALF = 64
VMEM_LIMIT = 56 * 1024 * 1024

F32 = jnp.float32
BF16 = jnp.bfloat16


def _nt_dot(a, b):
    return lax.dot_general(a, b, (((1,), (1,)), ((), ())), preferred_element_type=F32)


def _dot(a, b):
    return jnp.dot(a, b, preferred_element_type=F32)


def _rms(x, g):
    ms = jnp.mean(x * x, axis=-1, keepdims=True)
    return x * lax.rsqrt(ms + RMS_EPS) * g


def _cparams(sem):
    return pltpu.CompilerParams(dimension_semantics=sem, vmem_limit_bytes=VMEM_LIMIT)


FFN_TM = 512
FFN_TF = 256


def _ffn_kernel(x_ref, g_ref, wgu_ref, wd_ref, *rest, final):
    if final:
        fg_ref, o_ref = rest
    else:
        (o_ref,) = rest
    x = x_ref[...]
    hb = _rms(x, g_ref[...]).astype(BF16)
    acc = jnp.zeros(x.shape, F32)
    for f in range(D_FF // FFN_TF):
        lo = f * FFN_TF
        g = _dot(hb, wgu_ref[:, lo:lo + FFN_TF])
        u = _dot(hb, wgu_ref[:, D_FF + lo:D_FF + lo + FFN_TF])
        a = (g * jax.nn.sigmoid(g)) * u
        acc = acc + _dot(a.astype(BF16), wd_ref[lo:lo + FFN_TF, :])
    y = x + 0.5 * acc
    if final:
        y = _rms(y, fg_ref[...])
    o_ref[...] = y


def _ffn(x2, gain, wgu, wd, final_gain=None):
    n = x2.shape[0]
    final = final_gain is not None
    resident = dict(pipeline_mode=pl.Buffered(1))
    in_specs = [
        pl.BlockSpec((FFN_TM, D_MODEL), lambda i: (i, 0)),
        pl.BlockSpec((1, D_MODEL), lambda i: (0, 0)),
        pl.BlockSpec((D_MODEL, 2 * D_FF), lambda i: (0, 0), **resident),
        pl.BlockSpec((D_FF, D_MODEL), lambda i: (0, 0), **resident),
    ]
    args = [x2, gain.reshape(1, D_MODEL), wgu, wd]
    if final:
        in_specs.append(pl.BlockSpec((1, D_MODEL), lambda i: (0, 0)))
        args.append(final_gain.reshape(1, D_MODEL))
    return pl.pallas_call(
        functools.partial(_ffn_kernel, final=final),
        grid=(n // FFN_TM,),
        in_specs=in_specs,
        out_specs=pl.BlockSpec((FFN_TM, D_MODEL), lambda i: (i, 0)),
        out_shape=jax.ShapeDtypeStruct((n, D_MODEL), F32),
        compiler_params=_cparams(("parallel",)),
        name="ffn",
    )(*args)


PROJ_TM = 512
PROJ_CH = 256


def _rope_tile(y, cos_t, sin_t, first_half):
    rot = jnp.where(first_half, pltpu.roll(y, 96, 1), pltpu.roll(y, 32, 1))
    return y * cos_t + rot * sin_t


def _proj_kernel(*refs, row_groups, t_groups, use_rope):
    it = iter(refs)
    x_ref = next(it)
    g_ref = next(it)
    w_ref = next(it)
    wt_ref = next(it) if t_groups else None
    if use_rope:
        cos_t = next(it)[...]
        sin_t = next(it)[...]
        lane = lax.broadcasted_iota(jnp.int32, (1, LANES), 1)
        first_half = (lane % HALF) < (HALF // 2)
    o_refs = list(it)
    hb = _rms(x_ref[...], g_ref[...]).astype(BF16)
    tm = hb.shape[0]
    for (c0, width, rope, scale), o_ref in zip(row_groups, o_refs):
        for t0 in range(0, width, PROJ_CH):
            ch = min(PROJ_CH, width - t0)
            y = _dot(hb, w_ref[:, c0 + t0:c0 + t0 + ch])
            for l0 in range(0, ch, LANES):
                yt = y[:, l0:l0 + LANES]
                if rope:
                    yt = _rope_tile(yt, cos_t, sin_t, first_half)
                if scale != 1.0:
                    yt = yt * scale
                o_ref[:, t0 + l0:t0 + l0 + LANES] = yt.astype(o_ref.dtype)
    for (c0, width, ck), o_ref in zip(t_groups, o_refs[len(row_groups):]):
        for t0 in range(0, width, PROJ_CH):
            ch = min(PROJ_CH, width - t0)
            yt = _nt_dot(wt_ref[c0 + t0:c0 + t0 + ch, :], hb)
            for s0 in range(0, tm, ck):
                o_ref[0, s0 // ck, t0:t0 + ch, :] = yt[:, s0:s0 + ck].astype(o_ref.dtype)


def _proj(x2, gain, w, row_groups, row_dtypes, seq, wt=None, t_groups=(), t_dtypes=(),
          rope_tabs=None):
    n = x2.shape[0]
    bsz = n // seq
    nblk = seq // PROJ_TM
    use_rope = rope_tabs is not None
    resident = dict(pipeline_mode=pl.Buffered(1))
    in_specs = [
        pl.BlockSpec((PROJ_TM, D_MODEL), lambda i: (i, 0)),
        pl.BlockSpec((1, D_MODEL), lambda i: (0, 0)),
        pl.BlockSpec(w.shape, lambda i: (0, 0), **resident),
    ]
    args = [x2, gain.reshape(1, D_MODEL), w]
    if t_groups:
        in_specs.append(pl.BlockSpec(wt.shape, lambda i: (0, 0), **resident))
        args.append(wt)
    if use_rope:
        in_specs += [pl.BlockSpec((PROJ_TM, LANES), lambda i: (i % nblk, 0))] * 2
        args += list(rope_tabs)
    out_specs = [pl.BlockSpec((PROJ_TM, g[1]), lambda i: (i, 0)) for g in row_groups]
    out_shape = [jax.ShapeDtypeStruct((n, g[1]), dt) for g, dt in zip(row_groups, row_dtypes)]
    for (_, width, ck), dt in zip(t_groups, t_dtypes):
        out_specs.append(pl.BlockSpec((1, PROJ_TM // ck, width, ck),
                                      lambda i: (i // nblk, i % nblk, 0, 0)))
        out_shape.append(jax.ShapeDtypeStruct((bsz, seq // ck, width, ck), dt))
    return pl.pallas_call(
        functools.partial(_proj_kernel, row_groups=row_groups, t_groups=t_groups,
                          use_rope=use_rope),
        grid=(n // PROJ_TM,),
        in_specs=in_specs,
        out_specs=out_specs,
        out_shape=out_shape,
        compiler_params=_cparams(("parallel",)),
        name="proj",
    )(*args)


OUT_TM = 512


def _outproj_kernel(*refs, n_in):
    o_refs = refs[:n_in]
    w_ref, x_ref, out_ref = refs[n_in:]
    o = o_refs[0][...].astype(F32)
    for r in o_refs[1:]:
        o = o + r[...].astype(F32)
    out_ref[...] = x_ref[...] + _dot(o.astype(BF16), w_ref[...])


def _outproj(os_, w, x2):
    n = x2.shape[0]
    n_in = len(os_)
    row = pl.BlockSpec((OUT_TM, D_MODEL), lambda i: (i, 0))
    return pl.pallas_call(
        functools.partial(_outproj_kernel, n_in=n_in),
        grid=(n // OUT_TM,),
        in_specs=[row] * n_in + [pl.BlockSpec((D_MODEL, D_MODEL), lambda i: (0, 0)), row],
        out_specs=row,
        out_shape=jax.ShapeDtypeStruct((n, D_MODEL), F32),
        compiler_params=_cparams(("parallel",)),
        name="outproj",
    )(*os_, w, x2)


def _align_queries(q_ref, qal_ref, nt, kv_half):
    lane = lax.broadcasted_iota(jnp.int32, (1, LANES), 1)
    half = lane // HALF
    for tt in range(nt):
        qt = q_ref[0, :, tt * LANES:(tt + 1) * LANES].astype(F32)
        if kv_half is not None:
            qr = pltpu.roll(qt, HALF, 1)
        for a in range(2):
            if kv_half is not None:
                qa = jnp.where(half == kv_half, jnp.where(kv_half == a, qt, qr), 0.0)
            else:
                qa = jnp.where(half == a, qt, 0.0)
            qal_ref[2 * tt + a] = qa.astype(BF16)


def _pick_half(x, kv_half, a):
    if kv_half is None:
        return x[a * HALF:(a + 1) * HALF]
    return jnp.where(kv_half == 0, x[:HALF], x[HALF:])


def _flash_kernel(*refs, nt, tq, tk, window, use_sel, use_fox, use_sink, gate_branch, gqa):
    it = iter(refs)
    q_ref = next(it)
    k_ref = next(it)
    vt_ref = next(it)
    if use_sel:
        selt_ref = next(it)
        et_ref = next(it)
    if use_fox:
        ct_ref = next(it)
        cs_ref = next(it)
    if use_sink:
        sink_ref = next(it)
    if gate_branch is not None:
        gt_ref = next(it)
    o_ref = next(it)
    qal_ref = next(it)
    m_ref = next(it)
    acc_ref = next(it)
    s_ref = next(it)
    p_ref = next(it)
    a_ref = next(it)
    if use_sel:
        b_ref = next(it)

    hg = pl.program_id(1)
    i = pl.program_id(2)
    nh = 2 * nt
    kv_half = (jnp.zeros((1, 1), jnp.int32) + (hg % 2)) if gqa else None
    _align_queries(q_ref, qal_ref, nt, kv_half)

    m_ref[...] = jnp.full(m_ref.shape, NEG, F32)
    acc_ref[...] = jnp.zeros(acc_ref.shape, F32)

    row_half = lax.broadcasted_iota(jnp.int32, (LANES, 1), 0) // HALF
    q0 = i * tq
    colrow = (lax.broadcasted_iota(jnp.int32, (tk, tq), 1)
              - lax.broadcasted_iota(jnp.int32, (tk, tq), 0))
    if use_sel:
        selt = selt_ref[0, 0]
    if use_fox:
        ct = ct_ref[0, 0, i]

    if window is None:
        c_lo = 0
    else:
        c_lo = jnp.maximum(q0 - (window - 1), 0) // tk
    c_hi = (q0 + tq + tk - 1) // tk

    def scores(c, h):
        return _nt_dot(k_ref[0, pl.ds(pl.multiple_of(c * tk, tk), tk), :], qal_ref[h])

    def values_t(c):
        vt = vt_ref[0, c]
        if gqa:
            return [jnp.where(row_half == kv_half, vt, jnp.ones_like(vt))] * nh
        return [jnp.where(row_half == (h % 2), vt, jnp.ones_like(vt)) for h in range(nh)]

    def accumulate(vth, par, h):
        acc_ref[h] = a_ref[par, h] * acc_ref[h] + _dot(vth[h], p_ref[par, h])

    for h in range(nh):
        s_ref[0, h] = scores(c_lo, h)
    if use_sel:
        b_ref[0] = _dot(et_ref[c_lo], selt)
    p_ref[1] = jnp.zeros(p_ref.shape[1:], BF16)
    a_ref[...] = jnp.ones(a_ref.shape, F32)

    def trip(c, par):
        k0 = pl.multiple_of(c * tk, tk)
        d = colrow + (q0 - k0)
        mask = d >= 0
        if window is not None:
            mask = mask & (d < window)
        if use_sel:
            mask = mask & (b_ref[par] > 0.5)
        c_next = jnp.minimum(c + 1, c_hi - 1)
        vth = values_t(jnp.maximum(c - 1, c_lo))
        if use_sel:
            b_ref[1 - par] = _dot(et_ref[c_next], selt)
        for h in range(nh):
            accumulate(vth, 1 - par, h)
            s_ref[1 - par, h] = scores(c_next, h)
            s = s_ref[par, h]
            if use_fox:
                cs = cs_ref[0, h, pl.ds(k0, tk), :]
                s = s + (ct[h:h + 1, :] - jnp.concatenate([cs] * (tq // LANES), axis=1))
            s = jnp.where(mask, s, NEG)
            m_old = m_ref[h]
            m_new = jnp.maximum(m_old, jnp.max(s, axis=0, keepdims=True))
            a_ref[par, h] = jnp.exp2(m_old - m_new)
            p_ref[par, h] = jnp.exp2(s - m_new).astype(BF16)
            m_ref[h] = m_new

    def chunk(c, carry):
        odd = (c - c_lo) % 2
        pl.when(odd == 0)(lambda: trip(c, 0))
        pl.when(odd == 1)(lambda: trip(c, 1))
        return carry

    lax.fori_loop(c_lo, c_hi, chunk, 0)
    last_odd = (c_hi - 1 - c_lo) % 2

    def drain(par):
        vth = values_t(c_hi - 1)
        for h in range(nh):
            accumulate(vth, par, h)

    pl.when(last_odd == 0)(lambda: drain(0))
    pl.when(last_odd == 1)(lambda: drain(1))

    for tt in range(nt):
        outs = []
        for a in range(2):
            h = 2 * tt + a
            acc = acc_ref[h]
            if gqa:
                l = jnp.where(kv_half == 0, acc[HALF:HALF + 1], acc[0:1])
            else:
                l = acc[(1 - a) * HALF:(1 - a) * HALF + 1]
            out = _pick_half(acc, kv_half, a)
            if use_sink:
                m = m_ref[h]
                sk = sink_ref[hg * nh + h] * LOG2E
                m2 = jnp.maximum(m, sk)
                f = jnp.exp2(m - m2)
                l = l * f + jnp.exp2(sk - m2)
                out = out * f
            out = out * (1.0 / l)
            if gate_branch is not None:
                gi = gate_branch * N_HEADS + hg * nh + h
                out = out * jax.nn.sigmoid(gt_ref[0, 0, pl.ds(gi, 1), :])
            outs.append(out)
        tile_t = jnp.concatenate(outs, axis=0)
        o_ref[0, :, tt * LANES:(tt + 1) * LANES] = tile_t.T.astype(o_ref.dtype)


def _flash(q, k, vt, *, nt, tq, tk, kv_tile, gqa, out_dtype, window=None, selt=None, emat_t=None,
           ct=None, cs=None, sinks=None, gates_t=None, gate_branch=None):
    bsz, seq, qw = q.shape
    n_hg = qw // (nt * LANES)
    nq = seq // tq
    nk = seq // tk
    use_sel = selt is not None
    use_fox = ct is not None
    use_sink = sinks is not None
    in_specs = [
        pl.BlockSpec((1, tq, nt * LANES), lambda b, g, i: (b, i, g)),
        pl.BlockSpec((1, seq, LANES), lambda b, g, i: (b, 0, kv_tile(g))),
        pl.BlockSpec((1, nk, LANES, tk), lambda b, g, i: (b, 0, kv_tile(g), 0)),
    ]
    args = [q, k, vt]
    if use_sel:
        in_specs += [
            pl.BlockSpec((1, 1, LANES, tq), lambda b, g, i: (b, g, 0, i)),
            pl.BlockSpec(emat_t.shape, lambda b, g, i: (0, 0, 0)),
        ]
        args += [selt, emat_t]
    if use_fox:
        in_specs += [
            pl.BlockSpec((1, 1) + ct.shape[2:], lambda b, g, i: (b, g, 0, 0, 0)),
            pl.BlockSpec((1, 2, seq, LANES), lambda b, g, i: (b, g, 0, 0)),
        ]
        args += [ct, cs]
    if use_sink:
        in_specs.append(pl.BlockSpec(memory_space=pltpu.SMEM))
        args.append(sinks)
    if gate_branch is not None:
        in_specs.append(pl.BlockSpec((1, 1, LANES, tq), lambda b, g, i: (b, i, 0, 0)))
        args.append(gates_t)
    nh = 2 * nt
    kern = functools.partial(
        _flash_kernel, nt=nt, tq=tq, tk=tk, window=window, use_sel=use_sel, use_fox=use_fox,
        use_sink=use_sink, gate_branch=gate_branch, gqa=gqa)
    return pl.pallas_call(
        kern,
        grid=(bsz, n_hg, nq),
        in_specs=in_specs,
        out_specs=pl.BlockSpec((1, tq, nt * LANES), lambda b, g, i: (b, i, g)),
        out_shape=jax.ShapeDtypeStruct((bsz, seq, qw), out_dtype),
        scratch_shapes=[
            pltpu.VMEM((nh, tq, LANES), BF16),
            pltpu.VMEM((nh, 1, tq), F32),
            pltpu.VMEM((nh, LANES, tq), F32),
            pltpu.VMEM((2, nh, tk, tq), F32),
            pltpu.VMEM((2, nh, tk, tq), BF16),
            pltpu.VMEM((2, nh, 1, tq), F32),
        ] + ([pltpu.VMEM((2, tk, tq), F32)] if use_sel else []),
        compiler_params=_cparams(("parallel", "parallel", "arbitrary")),
        name="flash",
    )(*args)


N_CHUNK16 = 128


def _compress_hidden(x_ref, pe_ref, w1_ref):
    hid = NSA_GROUPS * NSA_CMP_HIDDEN
    x = x_ref[0]
    top = _dot((x + pe_ref[0:1, :]).astype(BF16), w1_ref[:, :hid])
    bot = _dot((x + pe_ref[1:2, :]).astype(BF16), w1_ref[:, hid:])
    h1 = top + pltpu.roll(bot, N_CHUNK16 - 1, 0)
    return jax.nn.gelu(h1, approximate=True).astype(BF16)


def _compress_kernel(xk_ref, xv_ref, pek_ref, pev_ref, wk1_ref, wv1_ref, wk2_ref, wv2t_ref,
                     kc_ref, vct_ref):
    kc_ref[0] = _dot(_compress_hidden(xk_ref, pek_ref, wk1_ref), wk2_ref[...]).astype(kc_ref.dtype)
    vct_ref[0] = _nt_dot(wv2t_ref[...], _compress_hidden(xv_ref, pev_ref, wv1_ref)).astype(vct_ref.dtype)


def _compress(xk, xv, pek, pev, wk1, wv1, wk2, wv2t):
    bsz = xk.shape[0]
    width = xk.shape[2]
    gd = NSA_GROUPS * HEAD_DIM
    xspec = pl.BlockSpec((1, N_CHUNK16, width), lambda b: (b, 0, 0))
    full = lambda a: pl.BlockSpec(a.shape, lambda b: (0,) * a.ndim)
    return pl.pallas_call(
        _compress_kernel,
        grid=(bsz,),
        in_specs=[xspec, xspec, full(pek), full(pev), full(wk1), full(wv1), full(wk2), full(wv2t)],
        out_specs=[pl.BlockSpec((1, N_CHUNK16, gd), lambda b: (b, 0, 0)),
                   pl.BlockSpec((1, gd, N_CHUNK16), lambda b: (b, 0, 0))],
        out_shape=[jax.ShapeDtypeStruct((bsz, N_CHUNK16, gd), BF16),
                   jax.ShapeDtypeStruct((bsz, gd, N_CHUNK16), BF16)],
        compiler_params=_cparams(("parallel",)),
        name="compress",
    )(xk, xv, pek, pev, wk1, wv1, wk2, wv2t)


CMP_TQ = 256
N_SLC = 32


def _cmp_kernel(q_ref, kc_ref, vct_ref, gt_ref, ovt_ref, o_ref, selt_ref, qal_ref):
    tq = CMP_TQ
    g = pl.program_id(1)
    i = pl.program_id(2)
    kv_half = jnp.zeros((1, 1), jnp.int32) + (g % 2)
    _align_queries(q_ref, qal_ref, 2, kv_half)
    kc = kc_ref[0]
    vct = vct_ref[0]
    t_row = i * tq + lax.broadcasted_iota(jnp.int32, (1, tq), 1)
    n_col = lax.broadcasted_iota(jnp.int32, (N_CHUNK16, 1), 0)
    n_cmp = (N_CHUNK16 * NSA_CMP_STRIDE - NSA_CMP_LEN) // NSA_CMP_STRIDE + 1
    valid = (n_col * NSA_CMP_STRIDE + (NSA_CMP_LEN - 1) <= t_row) & (n_col < n_cmp)

    psum = jnp.zeros((N_CHUNK16, tq), F32)
    for tt in range(2):
        outs = []
        for a in range(2):
            h = 2 * tt + a
            s = jnp.where(valid, _nt_dot(kc, qal_ref[h]), NEG)
            m = jnp.max(s, axis=0, keepdims=True)
            e = jnp.where(valid, jnp.exp2(s - m), 0.0)
            l = jnp.sum(e, axis=0, keepdims=True)
            p = e * (1.0 / jnp.where(l > 0.0, l, 1.0))
            psum = psum + p
            out = _pick_half(_dot(vct, p.astype(BF16)), kv_half, a)
            outs.append(out * jax.nn.sigmoid(gt_ref[0, 0, pl.ds(g * 4 + h, 1), :]))
        tile_t = jnp.concatenate(outs, axis=0)
        o_ref[0, :, tt * LANES:(tt + 1) * LANES] = tile_t.T

    p_hi = psum.astype(BF16)
    p_lo = (psum - p_hi.astype(F32)).astype(BF16)
    ovt = ovt_ref[...]
    imp = _dot(ovt, p_hi) + _dot(ovt, p_lo)
    j = lax.broadcasted_iota(jnp.int32, (N_SLC, 1), 0)
    tb = jnp.right_shift(t_row, 6)
    forced = (j == 0) | (j == tb) | (j == tb - 1)
    imp = jnp.where(j > tb, NEG, jnp.where(forced, NSA_FORCE_BONUS, imp))
    cnt = jnp.zeros((N_SLC, tq), jnp.int32)
    for jp in range(N_SLC):
        row = imp[jp:jp + 1, :]
        beats = (row > imp) | ((row == imp) & (jp < j))
        cnt = cnt + beats.astype(jnp.int32)
    sel_t = (cnt < NSA_TOPK).astype(F32)
    sel_t = jnp.concatenate([sel_t, jnp.zeros((LANES - N_SLC, tq), F32)], axis=0)
    selt_ref[0, 0] = sel_t.astype(selt_ref.dtype)


def _cmp_attn(q, kc, vct, gates_t, ovt):
    bsz, seq, _ = q.shape
    tq = CMP_TQ
    gw = 2 * LANES
    return pl.pallas_call(
        _cmp_kernel,
        grid=(bsz, NSA_GROUPS, seq // tq),
        in_specs=[
            pl.BlockSpec((1, tq, gw), lambda b, g, i: (b, i, g)),
            pl.BlockSpec((1, N_CHUNK16, LANES), lambda b, g, i: (b, 0, g // 2)),
            pl.BlockSpec((1, LANES, N_CHUNK16), lambda b, g, i: (b, g // 2, 0)),
            pl.BlockSpec((1, 1, LANES, tq), lambda b, g, i: (b, i, 0, 0)),
            pl.BlockSpec(ovt.shape, lambda b, g, i: (0, 0)),
        ],
        out_specs=[
            pl.BlockSpec((1, tq, gw), lambda b, g, i: (b, i, g)),
            pl.BlockSpec((1, 1, LANES, tq), lambda b, g, i: (b, g, 0, i)),
        ],
        out_shape=[
            jax.ShapeDtypeStruct((bsz, seq, D_MODEL), F32),
            jax.ShapeDtypeStruct((bsz, NSA_GROUPS, LANES, seq), BF16),
        ],
        scratch_shapes=[pltpu.VMEM((4, tq, LANES), BF16)],
        compiler_params=_cparams(("parallel", "parallel", "arbitrary")),
        name="cmp_attn",
    )(q, kc, vct, gates_t, ovt)


CS_BLK = 256


def _cumsum_kernel(z_ref, b_ref, c_ref):
    seq, width = z_ref.shape
    tri = (lax.broadcasted_iota(jnp.int32, (CS_BLK, CS_BLK), 0)
           >= lax.broadcasted_iota(jnp.int32, (CS_BLK, CS_BLK), 1)).astype(BF16)
    carry = jnp.zeros((1, width), F32)
    for blk in range(seq // CS_BLK):
        x = jax.nn.log_sigmoid(z_ref[blk * CS_BLK:(blk + 1) * CS_BLK, :] + b_ref[...])
        hi = x.astype(BF16)
        r1 = x - hi.astype(F32)
        mid = r1.astype(BF16)
        lo = (r1 - mid.astype(F32)).astype(BF16)
        cs = _dot(tri, hi) + _dot(tri, mid) + _dot(tri, lo) + carry
        c_ref[blk * CS_BLK:(blk + 1) * CS_BLK, :] = cs * LOG2E
        carry = cs[CS_BLK - 1:CS_BLK, :]


def _cumsum(z, bias):
    seq, width = z.shape
    return pl.pallas_call(
        _cumsum_kernel,
        grid=(width // LANES,),
        in_specs=[pl.BlockSpec((seq, LANES), lambda j: (0, j)),
                  pl.BlockSpec((1, LANES), lambda j: (0, j))],
        out_specs=pl.BlockSpec((seq, LANES), lambda j: (0, j)),
        out_shape=jax.ShapeDtypeStruct((seq, width), F32),
        compiler_params=_cparams(("parallel",)),
        name="cumsum",
    )(z, bias)


def _lane_bcast_kernel(c_ref, o_ref):
    li = pl.program_id(0) % LANES
    c = c_ref[...]
    lane = lax.broadcasted_iota(jnp.int32, c.shape, 1)
    col = jnp.sum(jnp.where(lane == li, c, 0.0), axis=1, keepdims=True)
    o_ref[0] = jnp.broadcast_to(col, c.shape)


def _lane_bcast(c, n):
    seq = c.shape[0]
    return pl.pallas_call(
        _lane_bcast_kernel,
        grid=(n,),
        in_specs=[pl.BlockSpec((seq, LANES), lambda j: (0, j // LANES))],
        out_specs=pl.BlockSpec((1, seq, LANES), lambda j: (j, 0, 0)),
        out_shape=jax.ShapeDtypeStruct((n, seq, LANES), F32),
        compiler_params=_cparams(("parallel",)),
        name="lane_bcast",
    )(c)


def _rope_tables(seq):
    inv = ROPE_THETA ** (-jnp.arange(0, HEAD_DIM, 2, dtype=F32) / HEAD_DIM)
    ang = jnp.arange(seq, dtype=F32)[:, None] * inv[None, :]
    cos, sin = jnp.cos(ang), jnp.sin(ang)
    cos_t = jnp.tile(cos, (1, LANES // (HEAD_DIM // 2)))
    sin_t = jnp.tile(jnp.concatenate([-sin, sin], axis=1), (1, LANES // HEAD_DIM))
    return cos_t, sin_t


def _overlap_t(seq):
    n_cmp = (seq - NSA_CMP_LEN) // NSA_CMP_STRIDE + 1
    n_slc = seq // NSA_SLC_LEN
    cs = np.arange(n_cmp) * NSA_CMP_STRIDE
    ce = cs + NSA_CMP_LEN
    ss = np.arange(n_slc) * NSA_SLC_LEN
    se = ss + NSA_SLC_LEN
    ov = np.clip(np.minimum(ce[:, None], se[None, :]) - np.maximum(cs[:, None], ss[None, :]), 0, None)
    ov = (ov / NSA_CMP_LEN).astype(np.float32)
    ovt = np.zeros((n_slc, LANES), np.float32)
    ovt[:, :n_cmp] = ov.T
    return jnp.asarray(ovt, dtype=BF16)


def _expand_mat_t(seq, tk):
    key = np.arange(seq)
    e = ((key // NSA_SLC_LEN)[:, None] == np.arange(LANES)[None, :]).astype(np.float32)
    return jnp.asarray(e.reshape(seq // tk, tk, LANES), dtype=BF16)


def _compress_weights(pe, w1, w2):
    g = NSA_GROUPS
    half = NSA_CMP_LEN // 2
    eye = jnp.eye(g, dtype=F32)
    w1r = w1.reshape(NSA_CMP_LEN, HEAD_DIM, NSA_CMP_HIDDEN)

    def big(part):
        return jnp.einsum('ldj,gh->lgdhj', part, eye).reshape(half * g * HEAD_DIM, g * NSA_CMP_HIDDEN)

    w1big = jnp.concatenate([big(w1r[:half]), big(w1r[half:])], axis=1).astype(BF16)
    w2big = jnp.einsum('jd,gh->gjhd', w2, eye).reshape(g * NSA_CMP_HIDDEN, g * HEAD_DIM).astype(BF16)

    def pebig(part):
        return jnp.broadcast_to(part[:, None, :], (half, g, HEAD_DIM)).reshape(-1)

    pe2 = jnp.stack([pebig(pe[:half]), pebig(pe[half:])], axis=0)
    pe2 = jnp.concatenate([pe2, jnp.zeros((6, pe2.shape[1]), F32)], axis=0)
    return pe2, w1big, w2big


def _pad_cols(w, width):
    return jnp.concatenate([w, jnp.zeros((w.shape[0], width - w.shape[1]), w.dtype)], axis=1)


ATT_T = 256


def _nsa_mixer(x2, bsz, seq, gain, w_in, ck_pe, ck_w1, ck_w2, cv_pe, cv_w1, cv_w2, w_out,
               rope_tabs, ovt, emat_t):
    qw = N_HEADS * HEAD_DIM
    gd = NSA_GROUPS * HEAD_DIM

    def kvcols(c, s):
        lo = qw + (c * 2 + s) * gd
        return w_in[:, lo:lo + gd]

    w = jnp.concatenate([w_in[:, :qw]] + [kvcols(c, 0) for c in range(3)] + [kvcols(0, 1)],
                        axis=1).astype(BF16)
    wt = jnp.concatenate([kvcols(1, 1), kvcols(2, 1), _pad_cols(w_in[:, qw + 6 * gd:], LANES)],
                         axis=1).T.astype(BF16)
    scale = HEAD_DIM ** -0.5 * LOG2E
    row_groups = ((0, qw, True, scale), (qw, gd, True, 1.0), (qw + gd, gd, True, 1.0),
                  (qw + 2 * gd, gd, True, 1.0), (qw + 3 * gd, gd, False, 1.0))
    t_groups = ((0, gd, ATT_T), (gd, gd, ATT_T), (2 * gd, LANES, ATT_T))
    q, k0, k1, k2, v0, v1t, v2t, gates_t = _proj(
        x2, gain, w, row_groups, (BF16, F32, BF16, BF16, F32), seq, wt=wt, t_groups=t_groups,
        t_dtypes=(BF16, BF16, F32), rope_tabs=rope_tabs)
    r3 = lambda a: a.reshape(bsz, seq, a.shape[-1])
    q, k1, k2 = map(r3, (q, k1, k2))

    chunkw = NSA_CMP_STRIDE * gd
    xk = k0.reshape(bsz, seq // NSA_CMP_STRIDE, chunkw)
    xv = v0.reshape(bsz, seq // NSA_CMP_STRIDE, chunkw)
    pek, wk1, wk2 = _compress_weights(ck_pe, ck_w1, ck_w2)
    pev, wv1, wv2 = _compress_weights(cv_pe, cv_w1, cv_w2)
    kc, vct = _compress(xk, xv, pek, pev, wk1, wv1, wk2, wv2.T)

    o_cmp, selt = _cmp_attn(q, kc, vct, gates_t, ovt)
    kv_tile = lambda g: g // 2
    o_slc = _flash(q, k1, v1t, nt=2, tq=ATT_T, tk=ATT_T, kv_tile=kv_tile, gqa=True, out_dtype=F32,
                   selt=selt, emat_t=emat_t, gates_t=gates_t, gate_branch=1)
    o_win = _flash(q, k2, v2t, nt=2, tq=ATT_T, tk=ATT_T, kv_tile=kv_tile, gqa=True, out_dtype=F32,
                   window=NSA_WINDOW, gates_t=gates_t, gate_branch=2)
    flat = lambda a: a.reshape(bsz * seq, D_MODEL)
    return _outproj([flat(o_cmp), flat(o_slc), flat(o_win)], w_out.astype(BF16), x2)


def _swa_mixer(x2, bsz, seq, gain, w_in, sinks, w_out, rope_tabs):
    qw = N_HEADS * HEAD_DIM
    scale = HEAD_DIM ** -0.5 * LOG2E
    row_groups = ((0, qw, True, scale), (qw, LANES, True, 1.0))
    w = w_in[:, :qw + LANES].astype(BF16)
    wt = w_in[:, qw + LANES:].T.astype(BF16)
    q, k, vt = _proj(x2, gain, w, row_groups, (BF16, BF16), seq, wt=wt,
                     t_groups=((0, LANES, SWA_WINDOW),), t_dtypes=(BF16,), rope_tabs=rope_tabs)
    r3 = lambda a: a.reshape(bsz, seq, a.shape[-1])
    o = _flash(r3(q), r3(k), vt, nt=4, tq=SWA_WINDOW, tk=SWA_WINDOW, kv_tile=lambda g: 0,
               gqa=True, out_dtype=BF16, window=SWA_WINDOW, sinks=sinks)
    return _outproj([o.reshape(bsz * seq, D_MODEL)], w_out.astype(BF16), x2)


def _fox_mixer(x2, bsz, seq, gain, w_in, b_f, w_out):
    qw = N_HEADS * HEAD_DIM
    scale = HEAD_DIM ** -0.5 * LOG2E
    w = jnp.concatenate([w_in[:, :2 * qw], _pad_cols(w_in[:, 3 * qw:], LANES)], axis=1).astype(BF16)
    wt = w_in[:, 2 * qw:3 * qw].T.astype(BF16)
    row_groups = ((0, qw, False, scale), (qw, qw, False, 1.0), (2 * qw, LANES, False, 1.0))
    q, k, f, vt = _proj(x2, gain, w, row_groups, (BF16, BF16, F32), seq, wt=wt,
                        t_groups=((0, qw, ATT_T),), t_dtypes=(BF16,))
    r3 = lambda a: a.reshape(bsz, seq, a.shape[-1])

    bh = bsz * N_HEADS
    bhp = -(-bh // LANES) * LANES
    z = f.reshape(bsz, seq, LANES)[:, :, :N_HEADS].transpose(1, 0, 2).reshape(seq, bh)
    z = _pad_cols(z, bhp)
    bias = _pad_cols(jnp.tile(b_f, bsz).reshape(1, bh), bhp)
    c = _cumsum(z, bias)
    cs = _lane_bcast(c, bh).reshape(bsz, N_HEADS, seq, LANES)

    tq = ATT_T
    ct = c[:, :bh].T.reshape(bsz, N_HEADS // 2, 2, seq // tq, tq).transpose(0, 1, 3, 2, 4)
    ct = jnp.concatenate([ct, jnp.zeros(ct.shape[:3] + (6, tq), F32)], axis=3)
    o = _flash(r3(q), r3(k), vt, nt=1, tq=tq, tk=ATT_T, kv_tile=lambda g: g, gqa=False,
               out_dtype=BF16, ct=ct, cs=cs)
    return _outproj([o.reshape(bsz * seq, D_MODEL)], w_out.astype(BF16), x2)


def kernel(x, ffn1_norm, ffn1_w_gu, ffn1_w_down, mix_norm, ffn2_norm, ffn2_w_gu, ffn2_w_down,
           nsa_w_in, nsa_ck_pe, nsa_ck_w1, nsa_ck_w2, nsa_cv_pe, nsa_cv_w1, nsa_cv_w2, nsa_w_out,
           swa_w_in, swa_sinks, swa_w_out, fox_w_in, fox_b_f, fox_w_out, final_norm):
    bsz, seq, _ = x.shape
    depth = ffn1_norm.shape[0]
    rope_tabs = _rope_tables(seq)
    ovt = _overlap_t(seq)
    emat_t = _expand_mat_t(seq, ATT_T)
    x2 = x.reshape(bsz * seq, D_MODEL)
    for i in range(depth):
        kind, j = i % N_MIXERS, i // N_MIXERS
        x2 = _ffn(x2, ffn1_norm[i], ffn1_w_gu[i].astype(BF16), ffn1_w_down[i].astype(BF16))
        if kind == 0:
            x2 = _nsa_mixer(x2, bsz, seq, mix_norm[i], nsa_w_in[j], nsa_ck_pe[j], nsa_ck_w1[j],
                            nsa_ck_w2[j], nsa_cv_pe[j], nsa_cv_w1[j], nsa_cv_w2[j], nsa_w_out[j],
                            rope_tabs, ovt, emat_t)
        elif kind == 1:
            x2 = _swa_mixer(x2, bsz, seq, mix_norm[i], swa_w_in[j], swa_sinks[j], swa_w_out[j],
                            rope_tabs)
        else:
            x2 = _fox_mixer(x2, bsz, seq, mix_norm[i], fox_w_in[j], fox_b_f[j], fox_w_out[j])
        last = i == depth - 1
        x2 = _ffn(x2, ffn2_norm[i], ffn2_w_gu[i].astype(BF16), ffn2_w_down[i].astype(BF16),
                  final_gain=final_norm if last else None)
    return x2.reshape(bsz, seq, D_MODEL)
```

```python
import functools

import numpy as np
import jax
import jax.numpy as jnp
from jax import lax
from jax.experimental import pallas as pl
from jax.experimental.pallas import tpu as pltpu

D_MODEL = 1024
HEAD_DIM = 64
N_HEADS = 16
D_FF = 2816
RMS_EPS = 1e-6
ROPE_THETA = 10000.0
NEG = -1e30
SEL_BIG = float(2 ** 100)
LOG2E = 1.4426950408889634

NSA_GROUPS = 4
NSA_CMP_LEN = 32
NSA_CMP_STRIDE = 16
NSA_CMP_HIDDEN = 128
NSA_SLC_LEN = 64
NSA_TOPK = 16
NSA_WINDOW = 512
NSA_FORCE_BONUS = 1e4
SWA_WINDOW = 128
N_MIXERS = 3

LANES = 128
HALF = 64
VMEM_LIMIT = 56 * 1024 * 1024

F32 = jnp.float32
BF16 = jnp.bfloat16


def _nt_dot(a, b):
    return lax.dot_general(a, b, (((1,), (1,)), ((), ())), preferred_element_type=F32)


def _dot(a, b):
    return jnp.dot(a, b, preferred_element_type=F32)


def _rms(x, g):
    ms = jnp.mean(x * x, axis=-1, keepdims=True)
    return x * lax.rsqrt(ms + RMS_EPS) * g


def _cparams(sem):
    return pltpu.CompilerParams(dimension_semantics=sem, vmem_limit_bytes=VMEM_LIMIT)


FFN_TM = 512
FFN_TF = 256


def _ffn_kernel(x_ref, g_ref, wgu_ref, wd_ref, *rest, final):
    if final:
        fg_ref, o_ref = rest
    else:
        (o_ref,) = rest
    x = x_ref[...]
    hb = _rms(x, g_ref[...]).astype(BF16)
    acc = jnp.zeros(x.shape, F32)
    for f in range(D_FF // FFN_TF):
        lo = f * FFN_TF
        g = _dot(hb, wgu_ref[:, lo:lo + FFN_TF])
        u = _dot(hb, wgu_ref[:, D_FF + lo:D_FF + lo + FFN_TF])
        a = (g * jax.nn.sigmoid(g)) * u
        acc = acc + _dot(a.astype(BF16), wd_ref[lo:lo + FFN_TF, :])
    y = x + 0.5 * acc
    if final:
        y = _rms(y, fg_ref[...])
    o_ref[...] = y


def _ffn(x2, gain, wgu, wd, final_gain=None):
    n = x2.shape[0]
    final = final_gain is not None
    resident = dict(pipeline_mode=pl.Buffered(1))
    in_specs = [
        pl.BlockSpec((FFN_TM, D_MODEL), lambda i: (i, 0)),
        pl.BlockSpec((1, D_MODEL), lambda i: (0, 0)),
        pl.BlockSpec((D_MODEL, 2 * D_FF), lambda i: (0, 0), **resident),
        pl.BlockSpec((D_FF, D_MODEL), lambda i: (0, 0), **resident),
    ]
    args = [x2, gain.reshape(1, D_MODEL), wgu, wd]
    if final:
        in_specs.append(pl.BlockSpec((1, D_MODEL), lambda i: (0, 0)))
        args.append(final_gain.reshape(1, D_MODEL))
    return pl.pallas_call(
        functools.partial(_ffn_kernel, final=final),
        grid=(n // FFN_TM,),
        in_specs=in_specs,
        out_specs=pl.BlockSpec((FFN_TM, D_MODEL), lambda i: (i, 0)),
        out_shape=jax.ShapeDtypeStruct((n, D_MODEL), F32),
        compiler_params=_cparams(("parallel",)),
        name="ffn",
    )(*args)


PROJ_TM = 512
PROJ_CH = 256


def _rope_tile(y, cos_t, sin_t, first_half):
    rot = jnp.where(first_half, pltpu.roll(y, 96, 1), pltpu.roll(y, 32, 1))
    return y * cos_t + rot * sin_t


def _proj_kernel(*refs, row_groups, t_groups, use_rope):
    it = iter(refs)
    x_ref = next(it)
    g_ref = next(it)
    w_ref = next(it)
    wt_ref = next(it) if t_groups else None
    if use_rope:
        cos_t = next(it)[...]
        sin_t = next(it)[...]
        lane = lax.broadcasted_iota(jnp.int32, (1, LANES), 1)
        first_half = (lane % HALF) < (HALF // 2)
    o_refs = list(it)
    hb = _rms(x_ref[...], g_ref[...]).astype(BF16)
    tm = hb.shape[0]
    for (c0, width, rope, scale), o_ref in zip(row_groups, o_refs):
        for t0 in range(0, width, PROJ_CH):
            ch = min(PROJ_CH, width - t0)
            y = _dot(hb, w_ref[:, c0 + t0:c0 + t0 + ch])
            for l0 in range(0, ch, LANES):
                yt = y[:, l0:l0 + LANES]
                if rope:
                    yt = _rope_tile(yt, cos_t, sin_t, first_half)
                if scale != 1.0:
                    yt = yt * scale
                o_ref[:, t0 + l0:t0 + l0 + LANES] = yt.astype(o_ref.dtype)
    for (c0, width, ck), o_ref in zip(t_groups, o_refs[len(row_groups):]):
        for t0 in range(0, width, PROJ_CH):
            ch = min(PROJ_CH, width - t0)
            yt = _nt_dot(wt_ref[c0 + t0:c0 + t0 + ch, :], hb)
            for s0 in range(0, tm, ck):
                o_ref[0, s0 // ck, t0:t0 + ch, :] = yt[:, s0:s0 + ck].astype(o_ref.dtype)


def _proj(x2, gain, w, row_groups, row_dtypes, seq, wt=None, t_groups=(), t_dtypes=(),
          rope_tabs=None):
    n = x2.shape[0]
    bsz = n // seq
    nblk = seq // PROJ_TM
    use_rope = rope_tabs is not None
    resident = dict(pipeline_mode=pl.Buffered(1))
    in_specs = [
        pl.BlockSpec((PROJ_TM, D_MODEL), lambda i: (i, 0)),
        pl.BlockSpec((1, D_MODEL), lambda i: (0, 0)),
        pl.BlockSpec(w.shape, lambda i: (0, 0), **resident),
    ]
    args = [x2, gain.reshape(1, D_MODEL), w]
    if t_groups:
        in_specs.append(pl.BlockSpec(wt.shape, lambda i: (0, 0), **resident))
        args.append(wt)
    if use_rope:
        in_specs += [pl.BlockSpec((PROJ_TM, LANES), lambda i: (i % nblk, 0))] * 2
        args += list(rope_tabs)
    out_specs = [pl.BlockSpec((PROJ_TM, g[1]), lambda i: (i, 0)) for g in row_groups]
    out_shape = [jax.ShapeDtypeStruct((n, g[1]), dt) for g, dt in zip(row_groups, row_dtypes)]
    for (_, width, ck), dt in zip(t_groups, t_dtypes):
        out_specs.append(pl.BlockSpec((1, PROJ_TM // ck, width, ck),
                                      lambda i: (i // nblk, i % nblk, 0, 0)))
        out_shape.append(jax.ShapeDtypeStruct((bsz, seq // ck, width, ck), dt))
    return pl.pallas_call(
        functools.partial(_proj_kernel, row_groups=row_groups, t_groups=t_groups,
                          use_rope=use_rope),
        grid=(n // PROJ_TM,),
        in_specs=in_specs,
        out_specs=out_specs,
        out_shape=out_shape,
        compiler_params=_cparams(("parallel",)),
        name="proj",
    )(*args)


OUT_TM = 512


def _outproj_kernel(*refs, n_in):
    o_refs = refs[:n_in]
    w_ref, x_ref, out_ref = refs[n_in:]
    o = o_refs[0][...].astype(F32)
    for r in o_refs[1:]:
        o = o + r[...].astype(F32)
    out_ref[...] = x_ref[...] + _dot(o.astype(BF16), w_ref[...])


def _outproj(os_, w, x2):
    n = x2.shape[0]
    n_in = len(os_)
    row = pl.BlockSpec((OUT_TM, D_MODEL), lambda i: (i, 0))
    return pl.pallas_call(
        functools.partial(_outproj_kernel, n_in=n_in),
        grid=(n // OUT_TM,),
        in_specs=[row] * n_in + [pl.BlockSpec((D_MODEL, D_MODEL), lambda i: (0, 0)), row],
        out_specs=row,
        out_shape=jax.ShapeDtypeStruct((n, D_MODEL), F32),
        compiler_params=_cparams(("parallel",)),
        name="outproj",
    )(*os_, w, x2)


def _align_queries(q_ref, qal_ref, nt, kv_half):
    lane = lax.broadcasted_iota(jnp.int32, (1, LANES), 1)
    half = lane // HALF
    for tt in range(nt):
        qt = q_ref[0, :, tt * LANES:(tt + 1) * LANES].astype(F32)
        qr = pltpu.roll(qt, HALF, 1)
        for a in range(2):
            qa = jnp.where(half == kv_half, jnp.where(kv_half == a, qt, qr), 0.0)
            qal_ref[2 * tt + a] = qa.astype(BF16)


def _pick_half(x, kv_half):
    return jnp.where(kv_half == 0, x[:HALF], x[HALF:])


def _flash_kernel(*refs, nt, tq, tk, window, use_sel, use_fox, use_sink, gate_branch,
                  head_slot, head_half, head_group):
    it = iter(refs)
    q_ref = next(it)
    k_ref = next(it)
    vt_ref = next(it)
    if use_sel:
        nsel_ref = next(it)
        e_ref = next(it)
    if use_fox:
        ck_ref = next(it)
        cq_ref = next(it)
    if use_sink:
        sink_ref = next(it)
    if gate_branch is not None:
        gt_ref = next(it)
    o_ref = next(it)
    qal_ref = next(it)
    m_ref = next(it)
    acc_ref = next(it)
    s_ref = next(it)
    p_ref = next(it)
    a_ref = next(it)

    hg = pl.program_id(1)
    i = pl.program_id(2)
    nh = 2 * nt

    lane = lax.broadcasted_iota(jnp.int32, (1, LANES), 1)
    half = lane // HALF
    for tt in range(nt):
        qt = q_ref[0, :, tt * LANES:(tt + 1) * LANES].astype(F32)
        qr = pltpu.roll(qt, HALF, 1) if any(head_half[2 * tt + a] != a for a in range(2)) else None
        for a in range(2):
            h = 2 * tt + a
            src = qt if head_half[h] == a else qr
            qal_ref[h, :, :LANES] = jnp.where(half == head_half[h], src, 0.0).astype(BF16)
            if use_sel:
                qal_ref[h, :, LANES:] = nsel_ref[0, head_group[h]]
            if use_fox:
                qal_ref[h, :, LANES:] = cq_ref[0, h]

    m_ref[...] = jnp.full(m_ref.shape, NEG, F32)
    acc_ref[...] = jnp.zeros(acc_ref.shape, F32)

    row_half = lax.broadcasted_iota(jnp.int32, (LANES, 1), 0) // HALF
    q0 = i * tq
    colrow = (lax.broadcasted_iota(jnp.int32, (tk, tq), 1)
              - lax.broadcasted_iota(jnp.int32, (tk, tq), 0))

    if window is None:
        c_lo = 0
    else:
        c_lo = jnp.maximum(q0 - (window - 1), 0) // tk
    c_hi = (q0 + tq + tk - 1) // tk

    def scores(c, h):
        k0 = pl.multiple_of(c * tk, tk)
        sl = head_slot[h]
        kc = k_ref[0, pl.ds(k0, tk), sl * LANES:(sl + 1) * LANES]
        if use_sel:
            kc = jnp.concatenate([kc, e_ref[c]], axis=1)
        if use_fox:
            kc = jnp.concatenate([kc, ck_ref[0, h, pl.ds(k0, tk), :]], axis=1)
        return _nt_dot(kc, qal_ref[h])

    def values_t(c):
        made = {}
        for h in range(nh):
            key = (head_slot[h], head_half[h])
            if key not in made:
                vt = vt_ref[0, c, key[0] * LANES:(key[0] + 1) * LANES, :]
                made[key] = jnp.where(row_half == key[1], vt, jnp.ones_like(vt))
        return [made[(head_slot[h], head_half[h])] for h in range(nh)]

    def accumulate(vth, par, h):
        acc_ref[h] = a_ref[par, h] * acc_ref[h] + _dot(vth[h], p_ref[par, h])

    for h in range(nh):
        s_ref[0, h] = scores(c_lo, h)
    p_ref[1] = jnp.zeros(p_ref.shape[1:], BF16)
    a_ref[...] = jnp.ones(a_ref.shape, F32)

    def trip(c, par, masked):
        k0 = pl.multiple_of(c * tk, tk)
        if masked:
            d = colrow + (q0 - k0)
            mask = d >= 0
            if window is not None:
                mask = mask & (d < window)
        c_next = jnp.minimum(c + 1, c_hi - 1)
        vth = values_t(jnp.maximum(c - 1, c_lo))
        for h in range(nh):
            accumulate(vth, 1 - par, h)
            s_ref[1 - par, h] = scores(c_next, h)
            s = s_ref[par, h]
            if masked:
                s = jnp.where(mask, s, NEG)
            m_old = m_ref[h]
            m_new = jnp.maximum(m_old, jnp.max(s, axis=0, keepdims=True))
            a_ref[par, h] = jnp.exp2(m_old - m_new)
            p_ref[par, h] = jnp.exp2(s - m_new).astype(BF16)
            m_ref[h] = m_new

    def chunk(c, carry):
        odd = (c - c_lo) % 2
        k0 = c * tk
        edge = k0 + (tk - 1) > q0
        if window is not None:
            edge = edge | (k0 + window <= q0 + (tq - 1))
        for par in range(2):
            pl.when((odd == par) & edge)(functools.partial(trip, c, par, True))
            if window is None or window > tk:
                pl.when((odd == par) & jnp.logical_not(edge))(functools.partial(trip, c, par, False))
        return carry

    lax.fori_loop(c_lo, c_hi, chunk, 0)
    last_odd = (c_hi - 1 - c_lo) % 2

    def drain(par):
        vth = values_t(c_hi - 1)
        for h in range(nh):
            accumulate(vth, par, h)

    pl.when(last_odd == 0)(functools.partial(drain, 0))
    pl.when(last_odd == 1)(functools.partial(drain, 1))

    for tt in range(nt):
        outs = []
        for a in range(2):
            h = 2 * tt + a
            acc = acc_ref[h]
            kh = head_half[h]
            l = acc[(1 - kh) * HALF:(1 - kh) * HALF + 1]
            out = acc[kh * HALF:(kh + 1) * HALF]
            if use_sink:
                m = m_ref[h]
                sk = sink_ref[hg * nh + h] * LOG2E
                m2 = jnp.maximum(m, sk)
                f = jnp.exp2(m - m2)
                l = l * f + jnp.exp2(sk - m2)
                out = out * f
            out = out * (1.0 / l)
            if gate_branch is not None:
                gi = gate_branch * N_HEADS + hg * nh + h
                out = out * jax.nn.sigmoid(gt_ref[0, 0, pl.ds(gi, 1), :])
            outs.append(out)
        tile_t = jnp.concatenate(outs, axis=0)
        o_ref[0, :, tt * LANES:(tt + 1) * LANES] = tile_t.T.astype(o_ref.dtype)


def _flash(q, k, vt, *, nt, nkv, tq, tk, head_slot, head_half, out_dtype, head_group=None,
           window=None, nsel=None, emat=None, ck=None, cq=None, sinks=None, gates_t=None,
           gate_branch=None):
    bsz, seq, qw = q.shape
    n_hg = qw // (nt * LANES)
    nq = seq // tq
    nk = seq // tk
    nh = 2 * nt
    use_sel = nsel is not None
    use_fox = ck is not None
    use_sink = sinks is not None
    kw = 2 * LANES if (use_sel or use_fox) else LANES
    in_specs = [
        pl.BlockSpec((1, tq, nt * LANES), lambda b, g, i: (b, i, g)),
        pl.BlockSpec((1, seq, nkv * LANES), lambda b, g, i: (b, 0, g)),
        pl.BlockSpec((1, nk, nkv * LANES, tk), lambda b, g, i: (b, 0, g, 0)),
    ]
    args = [q, k, vt]
    if use_sel:
        ngrp = max(head_group) + 1
        in_specs += [
            pl.BlockSpec((1, ngrp, tq, LANES), lambda b, g, i: (b, g, i, 0)),
            pl.BlockSpec(emat.shape, lambda b, g, i: (0, 0, 0)),
        ]
        args += [nsel, emat]
    if use_fox:
        in_specs += [
            pl.BlockSpec((1, nh, seq, LANES), lambda b, g, i: (b, g, 0, 0)),
            pl.BlockSpec((1, nh, tq, LANES), lambda b, g, i: (b, g, i, 0)),
        ]
        args += [ck, cq]
    if use_sink:
        in_specs.append(pl.BlockSpec(memory_space=pltpu.SMEM))
        args.append(sinks)
    if gate_branch is not None:
        in_specs.append(pl.BlockSpec((1, 1, LANES, tq), lambda b, g, i: (b, i, 0, 0)))
        args.append(gates_t)
    kern = functools.partial(
        _flash_kernel, nt=nt, tq=tq, tk=tk, window=window, use_sel=use_sel, use_fox=use_fox,
        use_sink=use_sink, gate_branch=gate_branch, head_slot=head_slot, head_half=head_half,
        head_group=head_group)
    return pl.pallas_call(
        kern,
        grid=(bsz, n_hg, nq),
        in_specs=in_specs,
        out_specs=pl.BlockSpec((1, tq, nt * LANES), lambda b, g, i: (b, i, g)),
        out_shape=jax.ShapeDtypeStruct((bsz, seq, qw), out_dtype),
        scratch_shapes=[
            pltpu.VMEM((nh, tq, kw), BF16),
            pltpu.VMEM((nh, 1, tq), F32),
            pltpu.VMEM((nh, LANES, tq), F32),
            pltpu.VMEM((2, nh, tk, tq), F32),
            pltpu.VMEM((2, nh, tk, tq), BF16),
            pltpu.VMEM((2, nh, 1, tq), F32),
        ],
        compiler_params=_cparams(("parallel", "parallel", "arbitrary")),
        name="flash",
    )(*args)


N_CHUNK16 = 128


def _compress_hidden(x_ref, pe_ref, w1_ref):
    hid = NSA_GROUPS * NSA_CMP_HIDDEN
    x = x_ref[0]
    top = _dot((x + pe_ref[0:1, :]).astype(BF16), w1_ref[:, :hid])
    bot = _dot((x + pe_ref[1:2, :]).astype(BF16), w1_ref[:, hid:])
    h1 = top + pltpu.roll(bot, N_CHUNK16 - 1, 0)
    return jax.nn.gelu(h1, approximate=True).astype(BF16)


def _compress_kernel(xk_ref, xv_ref, pek_ref, pev_ref, wk1_ref, wv1_ref, wk2_ref, wv2t_ref,
                     kc_ref, vct_ref):
    kc_ref[0] = _dot(_compress_hidden(xk_ref, pek_ref, wk1_ref), wk2_ref[...]).astype(kc_ref.dtype)
    vct_ref[0] = _nt_dot(wv2t_ref[...], _compress_hidden(xv_ref, pev_ref, wv1_ref)).astype(vct_ref.dtype)


def _compress(xk, xv, pek, pev, wk1, wv1, wk2, wv2t):
    bsz = xk.shape[0]
    width = xk.shape[2]
    gd = NSA_GROUPS * HEAD_DIM
    xspec = pl.BlockSpec((1, N_CHUNK16, width), lambda b: (b, 0, 0))
    full = lambda a: pl.BlockSpec(a.shape, lambda b: (0,) * a.ndim)
    return pl.pallas_call(
        _compress_kernel,
        grid=(bsz,),
        in_specs=[xspec, xspec, full(pek), full(pev), full(wk1), full(wv1), full(wk2), full(wv2t)],
        out_specs=[pl.BlockSpec((1, N_CHUNK16, gd), lambda b: (b, 0, 0)),
                   pl.BlockSpec((1, gd, N_CHUNK16), lambda b: (b, 0, 0))],
        out_shape=[jax.ShapeDtypeStruct((bsz, N_CHUNK16, gd), BF16),
                   jax.ShapeDtypeStruct((bsz, gd, N_CHUNK16), BF16)],
        compiler_params=_cparams(("parallel",)),
        name="compress",
    )(xk, xv, pek, pev, wk1, wv1, wk2, wv2t)


CMP_TQ = 256
N_SLC = 32


def _cmp_kernel(q_ref, kc_ref, vct_ref, gt_ref, ovt_ref, o_ref, nsel_ref, qal_ref):
    tq = CMP_TQ
    g = pl.program_id(1)
    i = pl.program_id(2)
    kv_half = jnp.zeros((1, 1), jnp.int32) + (g % 2)
    _align_queries(q_ref, qal_ref, 2, kv_half)
    kc = kc_ref[0]
    vct = vct_ref[0]
    t_row = i * tq + lax.broadcasted_iota(jnp.int32, (1, tq), 1)
    n_col = lax.broadcasted_iota(jnp.int32, (N_CHUNK16, 1), 0)
    n_cmp = (N_CHUNK16 * NSA_CMP_STRIDE - NSA_CMP_LEN) // NSA_CMP_STRIDE + 1
    valid = (n_col * NSA_CMP_STRIDE + (NSA_CMP_LEN - 1) <= t_row) & (n_col < n_cmp)

    psum = jnp.zeros((N_CHUNK16, tq), F32)
    for tt in range(2):
        outs = []
        for a in range(2):
            h = 2 * tt + a
            s = jnp.where(valid, _nt_dot(kc, qal_ref[h]), NEG)
            m = jnp.max(s, axis=0, keepdims=True)
            e = jnp.where(valid, jnp.exp2(s - m), 0.0)
            l = jnp.sum(e, axis=0, keepdims=True)
            p = e * (1.0 / jnp.where(l > 0.0, l, 1.0))
            psum = psum + p
            out = _pick_half(_dot(vct, p.astype(BF16)), kv_half)
            outs.append(out * jax.nn.sigmoid(gt_ref[0, 0, pl.ds(g * 4 + h, 1), :]))
        tile_t = jnp.concatenate(outs, axis=0)
        o_ref[0, :, tt * LANES:(tt + 1) * LANES] = tile_t.T

    p_hi = psum.astype(BF16)
    p_lo = (psum - p_hi.astype(F32)).astype(BF16)
    ovt = ovt_ref[...]
    imp = _dot(ovt, p_hi) + _dot(ovt, p_lo)
    j = lax.broadcasted_iota(jnp.int32, (N_SLC, 1), 0)
    tb = jnp.right_shift(t_row, 6)
    forced = (j == 0) | (j == tb) | (j == tb - 1)
    imp = jnp.where(j > tb, NEG, jnp.where(forced, NSA_FORCE_BONUS, imp))
    cnt = jnp.zeros((N_SLC, tq), jnp.int32)
    for jp in range(N_SLC):
        row = imp[jp:jp + 1, :]
        beats = (row > imp) | ((row == imp) & (jp < j))
        cnt = cnt + beats.astype(jnp.int32)
    sel_t = (cnt < NSA_TOPK).astype(F32)
    sel_t = jnp.concatenate([sel_t, jnp.ones((LANES - N_SLC, tq), F32)], axis=0)
    nsel_ref[0, 0] = ((sel_t.T - 1.0) * SEL_BIG).astype(nsel_ref.dtype)


def _cmp_attn(q, kc, vct, gates_t, ovt):
    bsz, seq, _ = q.shape
    tq = CMP_TQ
    gw = 2 * LANES
    return pl.pallas_call(
        _cmp_kernel,
        grid=(bsz, NSA_GROUPS, seq // tq),
        in_specs=[
            pl.BlockSpec((1, tq, gw), lambda b, g, i: (b, i, g)),
            pl.BlockSpec((1, N_CHUNK16, LANES), lambda b, g, i: (b, 0, g // 2)),
            pl.BlockSpec((1, LANES, N_CHUNK16), lambda b, g, i: (b, g // 2, 0)),
            pl.BlockSpec((1, 1, LANES, tq), lambda b, g, i: (b, i, 0, 0)),
            pl.BlockSpec(ovt.shape, lambda b, g, i: (0, 0)),
        ],
        out_specs=[
            pl.BlockSpec((1, tq, gw), lambda b, g, i: (b, i, g)),
            pl.BlockSpec((1, 1, tq, LANES), lambda b, g, i: (b, g, i, 0)),
        ],
        out_shape=[
            jax.ShapeDtypeStruct((bsz, seq, D_MODEL), F32),
            jax.ShapeDtypeStruct((bsz, NSA_GROUPS, seq, LANES), BF16),
        ],
        scratch_shapes=[pltpu.VMEM((4, tq, LANES), BF16)],
        compiler_params=_cparams(("parallel", "parallel", "arbitrary")),
        name="cmp_attn",
    )(q, kc, vct, gates_t, ovt)


CS_BLK = 256


def _cumsum_kernel(z_ref, b_ref, c_ref):
    seq, width = z_ref.shape
    tri = (lax.broadcasted_iota(jnp.int32, (CS_BLK, CS_BLK), 0)
           >= lax.broadcasted_iota(jnp.int32, (CS_BLK, CS_BLK), 1)).astype(BF16)
    carry = jnp.zeros((1, width), F32)
    for blk in range(seq // CS_BLK):
        x = jax.nn.log_sigmoid(z_ref[blk * CS_BLK:(blk + 1) * CS_BLK, :] + b_ref[...])
        hi = x.astype(BF16)
        r1 = x - hi.astype(F32)
        mid = r1.astype(BF16)
        lo = (r1 - mid.astype(F32)).astype(BF16)
        cs = _dot(tri, hi) + _dot(tri, mid) + _dot(tri, lo) + carry
        c_ref[blk * CS_BLK:(blk + 1) * CS_BLK, :] = cs * LOG2E
        carry = cs[CS_BLK - 1:CS_BLK, :]


def _cumsum(z, bias):
    seq, width = z.shape
    return pl.pallas_call(
        _cumsum_kernel,
        grid=(width // LANES,),
        in_specs=[pl.BlockSpec((seq, LANES), lambda j: (0, j)),
                  pl.BlockSpec((1, LANES), lambda j: (0, j))],
        out_specs=pl.BlockSpec((seq, LANES), lambda j: (0, j)),
        out_shape=jax.ShapeDtypeStruct((seq, width), F32),
        compiler_params=_cparams(("parallel",)),
        name="cumsum",
    )(z, bias)


def _fox_bias_kernel(c_ref, ck_ref, cq_ref):
    li = pl.program_id(0) % LANES
    c = c_ref[...]
    lane = lax.broadcasted_iota(jnp.int32, c.shape, 1)
    col = jnp.sum(jnp.where(lane == li, c, 0.0), axis=1, keepdims=True)
    hi = col.astype(BF16).astype(F32)
    mid = (col - hi).astype(BF16).astype(F32)
    lo = col - hi - mid
    k_piece = jnp.where(lane == 0, hi, jnp.where(lane == 1, mid, lo))
    q_piece = jnp.where(lane == 3, hi, jnp.where(lane == 4, mid, lo))
    ck_ref[0] = jnp.where(lane < 3, k_piece, jnp.where(lane < 6, 1.0, 0.0)).astype(BF16)
    cq_ref[0] = jnp.where(lane < 3, -1.0, jnp.where(lane < 6, q_piece, 0.0)).astype(BF16)


def _fox_bias(c, n):
    seq = c.shape[0]
    ospec = pl.BlockSpec((1, seq, LANES), lambda j: (j, 0, 0))
    oshape = jax.ShapeDtypeStruct((n, seq, LANES), BF16)
    return pl.pallas_call(
        _fox_bias_kernel,
        grid=(n,),
        in_specs=[pl.BlockSpec((seq, LANES), lambda j: (0, j // LANES))],
        out_specs=[ospec, ospec],
        out_shape=[oshape, oshape],
        compiler_params=_cparams(("parallel",)),
        name="fox_bias",
    )(c)


def _rope_tables(seq):
    inv = ROPE_THETA ** (-jnp.arange(0, HEAD_DIM, 2, dtype=F32) / HEAD_DIM)
    ang = jnp.arange(seq, dtype=F32)[:, None] * inv[None, :]
    cos, sin = jnp.cos(ang), jnp.sin(ang)
    cos_t = jnp.tile(cos, (1, LANES // (HEAD_DIM // 2)))
    sin_t = jnp.tile(jnp.concatenate([-sin, sin], axis=1), (1, LANES // HEAD_DIM))
    return cos_t, sin_t


def _overlap_t(seq):
    n_cmp = (seq - NSA_CMP_LEN) // NSA_CMP_STRIDE + 1
    n_slc = seq // NSA_SLC_LEN
    cs = np.arange(n_cmp) * NSA_CMP_STRIDE
    ce = cs + NSA_CMP_LEN
    ss = np.arange(n_slc) * NSA_SLC_LEN
    se = ss + NSA_SLC_LEN
    ov = np.clip(np.minimum(ce[:, None], se[None, :]) - np.maximum(cs[:, None], ss[None, :]), 0, None)
    ov = (ov / NSA_CMP_LEN).astype(np.float32)
    ovt = np.zeros((n_slc, LANES), np.float32)
    ovt[:, :n_cmp] = ov.T
    return jnp.asarray(ovt, dtype=BF16)


def _expand_mat_t(seq, tk):
    key = np.arange(seq)
    e = ((key // NSA_SLC_LEN)[:, None] == np.arange(LANES)[None, :]).astype(np.float32)
    return jnp.asarray(e.reshape(seq // tk, tk, LANES), dtype=BF16)


def _compress_weights(pe, w1, w2):
    g = NSA_GROUPS
    half = NSA_CMP_LEN // 2
    eye = jnp.eye(g, dtype=F32)
    w1r = w1.reshape(NSA_CMP_LEN, HEAD_DIM, NSA_CMP_HIDDEN)

    def big(part):
        return jnp.einsum('ldj,gh->lgdhj', part, eye).reshape(half * g * HEAD_DIM, g * NSA_CMP_HIDDEN)

    w1big = jnp.concatenate([big(w1r[:half]), big(w1r[half:])], axis=1).astype(BF16)
    w2big = jnp.einsum('jd,gh->gjhd', w2, eye).reshape(g * NSA_CMP_HIDDEN, g * HEAD_DIM).astype(BF16)

    def pebig(part):
        return jnp.broadcast_to(part[:, None, :], (half, g, HEAD_DIM)).reshape(-1)

    pe2 = jnp.stack([pebig(pe[:half]), pebig(pe[half:])], axis=0)
    pe2 = jnp.concatenate([pe2, jnp.zeros((6, pe2.shape[1]), F32)], axis=0)
    return pe2, w1big, w2big


def _pad_cols(w, width):
    return jnp.concatenate([w, jnp.zeros((w.shape[0], width - w.shape[1]), w.dtype)], axis=1)


ATT_T = 256


def _nsa_mixer(x2, bsz, seq, gain, w_in, ck_pe, ck_w1, ck_w2, cv_pe, cv_w1, cv_w2, w_out,
               rope_tabs, ovt, emat_t):
    qw = N_HEADS * HEAD_DIM
    gd = NSA_GROUPS * HEAD_DIM

    def kvcols(c, s):
        lo = qw + (c * 2 + s) * gd
        return w_in[:, lo:lo + gd]

    w = jnp.concatenate([w_in[:, :qw]] + [kvcols(c, 0) for c in range(3)] + [kvcols(0, 1)],
                        axis=1).astype(BF16)
    wt = jnp.concatenate([kvcols(1, 1), kvcols(2, 1), _pad_cols(w_in[:, qw + 6 * gd:], LANES)],
                         axis=1).T.astype(BF16)
    scale = HEAD_DIM ** -0.5 * LOG2E
    row_groups = ((0, qw, True, scale), (qw, gd, True, 1.0), (qw + gd, gd, True, 1.0),
                  (qw + 2 * gd, gd, True, 1.0), (qw + 3 * gd, gd, False, 1.0))
    t_groups = ((0, gd, ATT_T), (gd, gd, ATT_T), (2 * gd, LANES, ATT_T))
    q, k0, k1, k2, v0, v1t, v2t, gates_t = _proj(
        x2, gain, w, row_groups, (BF16, F32, BF16, BF16, F32), seq, wt=wt, t_groups=t_groups,
        t_dtypes=(BF16, BF16, F32), rope_tabs=rope_tabs)
    r3 = lambda a: a.reshape(bsz, seq, a.shape[-1])
    q, k1, k2 = map(r3, (q, k1, k2))

    chunkw = NSA_CMP_STRIDE * gd
    xk = k0.reshape(bsz, seq // NSA_CMP_STRIDE, chunkw)
    xv = v0.reshape(bsz, seq // NSA_CMP_STRIDE, chunkw)
    pek, wk1, wk2 = _compress_weights(ck_pe, ck_w1, ck_w2)
    pev, wv1, wv2 = _compress_weights(cv_pe, cv_w1, cv_w2)
    kc, vct = _compress(xk, xv, pek, pev, wk1, wv1, wk2, wv2.T)

    o_cmp, nsel = _cmp_attn(q, kc, vct, gates_t, ovt)
    grp = tuple(h // 4 for h in range(8))
    common = dict(nt=4, nkv=1, tq=ATT_T, tk=ATT_T, head_slot=(0,) * 8, head_half=grp,
                  out_dtype=F32, gates_t=gates_t)
    o_slc = _flash(q, k1, v1t, head_group=grp, nsel=nsel, emat=emat_t, gate_branch=1, **common)
    o_win = _flash(q, k2, v2t, window=NSA_WINDOW, gate_branch=2, **common)
    flat = lambda a: a.reshape(bsz * seq, D_MODEL)
    return _outproj([flat(o_cmp), flat(o_slc), flat(o_win)], w_out.astype(BF16), x2)


def _swa_mixer(x2, bsz, seq, gain, w_in, sinks, w_out, rope_tabs):
    qw = N_HEADS * HEAD_DIM
    scale = HEAD_DIM ** -0.5 * LOG2E
    row_groups = ((0, qw, True, scale), (qw, LANES, True, 1.0))
    w = w_in[:, :qw + LANES].astype(BF16)
    wt = w_in[:, qw + LANES:].T.astype(BF16)
    q, k, vt = _proj(x2, gain, w, row_groups, (BF16, BF16), seq, wt=wt,
                     t_groups=((0, LANES, SWA_WINDOW),), t_dtypes=(BF16,), rope_tabs=rope_tabs)
    r3 = lambda a: a.reshape(bsz, seq, a.shape[-1])
    o = _flash(r3(q), r3(k), vt, nt=8, nkv=1, tq=SWA_WINDOW, tk=SWA_WINDOW, head_slot=(0,) * 16,
               head_half=tuple(h // 8 for h in range(16)), out_dtype=BF16, window=SWA_WINDOW,
               sinks=sinks)
    return _outproj([o.reshape(bsz * seq, D_MODEL)], w_out.astype(BF16), x2)


def _fox_mixer(x2, bsz, seq, gain, w_in, b_f, w_out):
    qw = N_HEADS * HEAD_DIM
    scale = HEAD_DIM ** -0.5 * LOG2E
    w = jnp.concatenate([w_in[:, :2 * qw], _pad_cols(w_in[:, 3 * qw:], LANES)], axis=1).astype(BF16)
    wt = w_in[:, 2 * qw:3 * qw].T.astype(BF16)
    row_groups = ((0, qw, False, scale), (qw, qw, False, 1.0), (2 * qw, LANES, False, 1.0))
    q, k, f, vt = _proj(x2, gain, w, row_groups, (BF16, BF16, F32), seq, wt=wt,
                        t_groups=((0, qw, ATT_T),), t_dtypes=(BF16,))
    r3 = lambda a: a.reshape(bsz, seq, a.shape[-1])

    bh = bsz * N_HEADS
    bhp = -(-bh // LANES) * LANES
    z = f.reshape(bsz, seq, LANES)[:, :, :N_HEADS].transpose(1, 0, 2).reshape(seq, bh)
    z = _pad_cols(z, bhp)
    bias = _pad_cols(jnp.tile(b_f, bsz).reshape(1, bh), bhp)
    c = _cumsum(z, bias)
    ck, cq = _fox_bias(c, bh)
    r4 = lambda a: a.reshape(bsz, N_HEADS, seq, LANES)
    o = _flash(r3(q), r3(k), vt, nt=2, nkv=2, tq=ATT_T, tk=ATT_T, head_slot=(0, 0, 1, 1),
               head_half=(0, 1, 0, 1), out_dtype=BF16, ck=r4(ck), cq=r4(cq))
    return _outproj([o.reshape(bsz * seq, D_MODEL)], w_out.astype(BF16), x2)


def kernel(x, ffn1_norm, ffn1_w_gu, ffn1_w_down, mix_norm, ffn2_norm, ffn2_w_gu, ffn2_w_down,
           nsa_w_in, nsa_ck_pe, nsa_ck_w1, nsa_ck_w2, nsa_cv_pe, nsa_cv_w1, nsa_cv_w2, nsa_w_out,
           swa_w_in, swa_sinks, swa_w_out, fox_w_in, fox_b_f, fox_w_out, final_norm):
    bsz, seq, _ = x.shape
    depth = ffn1_norm.shape[0]
    rope_tabs = _rope_tables(seq)
    ovt = _overlap_t(seq)
    emat_t = _expand_mat_t(seq, ATT_T)
    x2 = x.reshape(bsz * seq, D_MODEL)
    for i in range(depth):
        kind, j = i % N_MIXERS, i // N_MIXERS
        x2 = _ffn(x2, ffn1_norm[i], ffn1_w_gu[i].astype(BF16), ffn1_w_down[i].astype(BF16))
        if kind == 0:
            x2 = _nsa_mixer(x2, bsz, seq, mix_norm[i], nsa_w_in[j], nsa_ck_pe[j], nsa_ck_w1[j],
                            nsa_ck_w2[j], nsa_cv_pe[j], nsa_cv_w1[j], nsa_cv_w2[j], nsa_w_out[j],
                            rope_tabs, ovt, emat_t)
        elif kind == 1:
            x2 = _swa_mixer(x2, bsz, seq, mix_norm[i], swa_w_in[j], swa_sinks[j], swa_w_out[j],
                            rope_tabs)
        else:
            x2 = _fox_mixer(x2, bsz, seq, mix_norm[i], fox_w_in[j], fox_b_f[j], fox_w_out[j])
        last = i == depth - 1
        x2 = _ffn(x2, ffn2_norm[i], ffn2_w_gu[i].astype(BF16), ffn2_w_down[i].astype(BF16),
                  final_gain=final_norm if last else None)
    return x2.reshape(bsz, seq, D_MODEL)
```

```python
import functools

import numpy as np
import jax
import jax.numpy as jnp
from jax import lax
from jax.experimental import pallas as pl
from jax.experimental.pallas import tpu as pltpu

D_MODEL = 1024
HEAD_DIM = 64
N_HEADS = 16
D_FF = 2816
RMS_EPS = 1e-6
ROPE_THETA = 10000.0
NEG = -1e30
SEL_BIG = float(2 ** 100)
LOG2E = 1.4426950408889634

NSA_GROUPS = 4
NSA_CMP_LEN = 32
NSA_CMP_STRIDE = 16
NSA_CMP_HIDDEN = 128
NSA_SLC_LEN = 64
NSA_TOPK = 16
NSA_WINDOW = 512
NSA_FORCE_BONUS = 1e4
SWA_WINDOW = 128
N_MIXERS = 3

LANES = 128
HALF = 64
VMEM_LIMIT = 56 * 1024 * 1024

F32 = jnp.float32
BF16 = jnp.bfloat16


def _nt_dot(a, b):
    return lax.dot_general(a, b, (((1,), (1,)), ((), ())), preferred_element_type=F32)


def _dot(a, b):
    return jnp.dot(a, b, preferred_element_type=F32)


def _rms(x, g):
    ms = jnp.mean(x * x, axis=-1, keepdims=True)
    return x * lax.rsqrt(ms + RMS_EPS) * g


def _cparams(sem):
    return pltpu.CompilerParams(dimension_semantics=sem, vmem_limit_bytes=VMEM_LIMIT)


FFN_TM = 512
FFN_TF = 256


def _ffn_kernel(x_ref, g_ref, wgu_ref, wd_ref, *rest, final):
    if final:
        fg_ref, o_ref = rest
    else:
        (o_ref,) = rest
    x = x_ref[...]
    hb = _rms(x, g_ref[...]).astype(BF16)
    acc = jnp.zeros(x.shape, F32)
    for f in range(D_FF // FFN_TF):
        lo = f * FFN_TF
        g = _dot(hb, wgu_ref[:, lo:lo + FFN_TF])
        u = _dot(hb, wgu_ref[:, D_FF + lo:D_FF + lo + FFN_TF])
        a = (g * jax.nn.sigmoid(g)) * u
        acc = acc + _dot(a.astype(BF16), wd_ref[lo:lo + FFN_TF, :])
    y = x + 0.5 * acc
    if final:
        y = _rms(y, fg_ref[...])
    o_ref[...] = y


def _ffn(x2, gain, wgu, wd, final_gain=None):
    n = x2.shape[0]
    final = final_gain is not None
    resident = dict(pipeline_mode=pl.Buffered(1))
    in_specs = [
        pl.BlockSpec((FFN_TM, D_MODEL), lambda i: (i, 0)),
        pl.BlockSpec((1, D_MODEL), lambda i: (0, 0)),
        pl.BlockSpec((D_MODEL, 2 * D_FF), lambda i: (0, 0), **resident),
        pl.BlockSpec((D_FF, D_MODEL), lambda i: (0, 0), **resident),
    ]
    args = [x2, gain.reshape(1, D_MODEL), wgu, wd]
    if final:
        in_specs.append(pl.BlockSpec((1, D_MODEL), lambda i: (0, 0)))
        args.append(final_gain.reshape(1, D_MODEL))
    return pl.pallas_call(
        functools.partial(_ffn_kernel, final=final),
        grid=(n // FFN_TM,),
        in_specs=in_specs,
        out_specs=pl.BlockSpec((FFN_TM, D_MODEL), lambda i: (i, 0)),
        out_shape=jax.ShapeDtypeStruct((n, D_MODEL), F32),
        compiler_params=_cparams(("parallel",)),
        name="ffn",
    )(*args)


PROJ_TM = 512
PROJ_CH = 256


def _rope_tile(y, cos_t, sin_t, first_half):
    rot = jnp.where(first_half, pltpu.roll(y, 96, 1), pltpu.roll(y, 32, 1))
    return y * cos_t + rot * sin_t


def _proj_kernel(*refs, row_groups, t_groups, use_rope):
    it = iter(refs)
    x_ref = next(it)
    g_ref = next(it)
    w_ref = next(it)
    wt_ref = next(it) if t_groups else None
    if use_rope:
        cos_t = next(it)[...]
        sin_t = next(it)[...]
        lane = lax.broadcasted_iota(jnp.int32, (1, LANES), 1)
        first_half = (lane % HALF) < (HALF // 2)
    o_refs = list(it)
    hb = _rms(x_ref[...], g_ref[...]).astype(BF16)
    tm = hb.shape[0]
    for (c0, width, rope, scale), o_ref in zip(row_groups, o_refs):
        for t0 in range(0, width, PROJ_CH):
            ch = min(PROJ_CH, width - t0)
            y = _dot(hb, w_ref[:, c0 + t0:c0 + t0 + ch])
            for l0 in range(0, ch, LANES):
                yt = y[:, l0:l0 + LANES]
                if rope:
                    yt = _rope_tile(yt, cos_t, sin_t, first_half)
                if scale != 1.0:
                    yt = yt * scale
                o_ref[:, t0 + l0:t0 + l0 + LANES] = yt.astype(o_ref.dtype)
    for (c0, width, ck), o_ref in zip(t_groups, o_refs[len(row_groups):]):
        for t0 in range(0, width, PROJ_CH):
            ch = min(PROJ_CH, width - t0)
            yt = _nt_dot(wt_ref[c0 + t0:c0 + t0 + ch, :], hb)
            for s0 in range(0, tm, ck):
                o_ref[0, s0 // ck, t0:t0 + ch, :] = yt[:, s0:s0 + ck].astype(o_ref.dtype)


def _proj(x2, gain, w, row_groups, row_dtypes, seq, wt=None, t_groups=(), t_dtypes=(),
          rope_tabs=None):
    n = x2.shape[0]
    bsz = n // seq
    nblk = seq // PROJ_TM
    use_rope = rope_tabs is not None
    resident = dict(pipeline_mode=pl.Buffered(1))
    in_specs = [
        pl.BlockSpec((PROJ_TM, D_MODEL), lambda i: (i, 0)),
        pl.BlockSpec((1, D_MODEL), lambda i: (0, 0)),
        pl.BlockSpec(w.shape, lambda i: (0, 0), **resident),
    ]
    args = [x2, gain.reshape(1, D_MODEL), w]
    if t_groups:
        in_specs.append(pl.BlockSpec(wt.shape, lambda i: (0, 0), **resident))
        args.append(wt)
    if use_rope:
        in_specs += [pl.BlockSpec((PROJ_TM, LANES), lambda i: (i % nblk, 0))] * 2
        args += list(rope_tabs)
    out_specs = [pl.BlockSpec((PROJ_TM, g[1]), lambda i: (i, 0)) for g in row_groups]
    out_shape = [jax.ShapeDtypeStruct((n, g[1]), dt) for g, dt in zip(row_groups, row_dtypes)]
    for (_, width, ck), dt in zip(t_groups, t_dtypes):
        out_specs.append(pl.BlockSpec((1, PROJ_TM // ck, width, ck),
                                      lambda i: (i // nblk, i % nblk, 0, 0)))
        out_shape.append(jax.ShapeDtypeStruct((bsz, seq // ck, width, ck), dt))
    return pl.pallas_call(
        functools.partial(_proj_kernel, row_groups=row_groups, t_groups=t_groups,
                          use_rope=use_rope),
        grid=(n // PROJ_TM,),
        in_specs=in_specs,
        out_specs=out_specs,
        out_shape=out_shape,
        compiler_params=_cparams(("parallel",)),
        name="proj",
    )(*args)


OUT_TM = 512


def _outproj_kernel(*refs, n_in):
    o_refs = refs[:n_in]
    w_ref, x_ref, out_ref = refs[n_in:]
    o = o_refs[0][...].astype(F32)
    for r in o_refs[1:]:
        o = o + r[...].astype(F32)
    out_ref[...] = x_ref[...] + _dot(o.astype(BF16), w_ref[...])


def _outproj(os_, w, x2):
    n = x2.shape[0]
    n_in = len(os_)
    row = pl.BlockSpec((OUT_TM, D_MODEL), lambda i: (i, 0))
    return pl.pallas_call(
        functools.partial(_outproj_kernel, n_in=n_in),
        grid=(n // OUT_TM,),
        in_specs=[row] * n_in + [pl.BlockSpec((D_MODEL, D_MODEL), lambda i: (0, 0)), row],
        out_specs=row,
        out_shape=jax.ShapeDtypeStruct((n, D_MODEL), F32),
        compiler_params=_cparams(("parallel",)),
        name="outproj",
    )(*os_, w, x2)


def _align_queries(q_ref, qal_ref, nt, kv_half):
    lane = lax.broadcasted_iota(jnp.int32, (1, LANES), 1)
    half = lane // HALF
    for tt in range(nt):
        qt = q_ref[0, :, tt * LANES:(tt + 1) * LANES].astype(F32)
        qr = pltpu.roll(qt, HALF, 1)
        for a in range(2):
            qa = jnp.where(half == kv_half, jnp.where(kv_half == a, qt, qr), 0.0)
            qal_ref[2 * tt + a] = qa.astype(BF16)


def _pick_half(x, kv_half):
    return jnp.where(kv_half == 0, x[:HALF], x[HALF:])


def _flash_kernel(*refs, nt, tq, tk, window, use_sel, use_fox, use_sink, gate_branch,
                  head_slot, head_half, head_group):
    it = iter(refs)
    q_ref = next(it)
    k_ref = next(it)
    vt_ref = next(it)
    if use_sel:
        nsel_ref = next(it)
        e_ref = next(it)
    if use_fox:
        ck_ref = next(it)
        cq_ref = next(it)
    if use_sink:
        sink_ref = next(it)
    if gate_branch is not None:
        gt_ref = next(it)
    o_ref = next(it)
    qal_ref = next(it)
    m_ref = next(it)
    acc_ref = next(it)
    s_ref = next(it)
    p_ref = next(it)
    a_ref = next(it)

    hg = pl.program_id(1)
    i = pl.program_id(2)
    nh = 2 * nt

    lane = lax.broadcasted_iota(jnp.int32, (1, LANES), 1)
    half = lane // HALF
    for tt in range(nt):
        qt = q_ref[0, :, tt * LANES:(tt + 1) * LANES].astype(F32)
        qr = pltpu.roll(qt, HALF, 1) if any(head_half[2 * tt + a] != a for a in range(2)) else None
        for a in range(2):
            h = 2 * tt + a
            src = qt if head_half[h] == a else qr
            qal_ref[h, :, :LANES] = jnp.where(half == head_half[h], src, 0.0).astype(BF16)
            if use_sel:
                qal_ref[h, :, LANES:] = nsel_ref[0, head_group[h]]
            if use_fox:
                qal_ref[h, :, LANES:] = cq_ref[0, h]

    m_ref[...] = jnp.full(m_ref.shape, NEG, F32)
    acc_ref[...] = jnp.zeros(acc_ref.shape, F32)

    row_half = lax.broadcasted_iota(jnp.int32, (LANES, 1), 0) // HALF
    q0 = i * tq
    colrow = (lax.broadcasted_iota(jnp.int32, (tk, tq), 1)
              - lax.broadcasted_iota(jnp.int32, (tk, tq), 0))

    if window is None:
        c_lo = 0
    else:
        c_lo = jnp.maximum(q0 - (window - 1), 0) // tk
    c_hi = (q0 + tq + tk - 1) // tk

    def scores(c, h):
        k0 = pl.multiple_of(c * tk, tk)
        sl = head_slot[h]
        kc = k_ref[0, pl.ds(k0, tk), sl * LANES:(sl + 1) * LANES]
        if use_sel:
            kc = jnp.concatenate([kc, e_ref[c]], axis=1)
        if use_fox:
            kc = jnp.concatenate([kc, ck_ref[0, h, pl.ds(k0, tk), :]], axis=1)
        return _nt_dot(kc, qal_ref[h])

    def values_t(c):
        made = {}
        for h in range(nh):
            key = (head_slot[h], head_half[h])
            if key not in made:
                vt = vt_ref[0, c, key[0] * LANES:(key[0] + 1) * LANES, :]
                made[key] = jnp.where(row_half == key[1], vt, jnp.ones_like(vt))
        return [made[(head_slot[h], head_half[h])] for h in range(nh)]

    def accumulate(vth, par, h):
        acc_ref[h] = a_ref[par, h] * acc_ref[h] + _dot(vth[h], p_ref[par, h])

    for h in range(nh):
        s_ref[0, h] = scores(c_lo, h)
    p_ref[1] = jnp.zeros(p_ref.shape[1:], BF16)
    a_ref[...] = jnp.ones(a_ref.shape, F32)

    def trip(c, par, masked):
        k0 = pl.multiple_of(c * tk, tk)
        if masked:
            d = colrow + (q0 - k0)
            mask = d >= 0
            if window is not None:
                mask = mask & (d < window)
        c_next = jnp.minimum(c + 1, c_hi - 1)
        vth = values_t(jnp.maximum(c - 1, c_lo))
        for h in range(nh):
            accumulate(vth, 1 - par, h)
            s_ref[1 - par, h] = scores(c_next, h)
            s = s_ref[par, h]
            if masked:
                s = jnp.where(mask, s, NEG)
            m_old = m_ref[h]
            m_new = jnp.maximum(m_old, jnp.max(s, axis=0, keepdims=True))
            a_ref[par, h] = jnp.exp2(m_old - m_new)
            p_ref[par, h] = jnp.exp2(s - m_new).astype(BF16)
            m_ref[h] = m_new

    def chunk(c, carry):
        odd = (c - c_lo) % 2
        k0 = c * tk
        edge = k0 + (tk - 1) > q0
        if window is not None:
            edge = edge | (k0 + window <= q0 + (tq - 1))
        for par in range(2):
            pl.when((odd == par) & edge)(functools.partial(trip, c, par, True))
            if window is None or window > tk:
                pl.when((odd == par) & jnp.logical_not(edge))(functools.partial(trip, c, par, False))
        return carry

    lax.fori_loop(c_lo, c_hi, chunk, 0)
    last_odd = (c_hi - 1 - c_lo) % 2

    def drain(par):
        vth = values_t(c_hi - 1)
        for h in range(nh):
            accumulate(vth, par, h)

    pl.when(last_odd == 0)(functools.partial(drain, 0))
    pl.when(last_odd == 1)(functools.partial(drain, 1))

    for tt in range(nt):
        outs = []
        for a in range(2):
            h = 2 * tt + a
            acc = acc_ref[h]
            kh = head_half[h]
            l = acc[(1 - kh) * HALF:(1 - kh) * HALF + 1]
            out = acc[kh * HALF:(kh + 1) * HALF]
            if use_sink:
                m = m_ref[h]
                sk = sink_ref[hg * nh + h] * LOG2E
                m2 = jnp.maximum(m, sk)
                f = jnp.exp2(m - m2)
                l = l * f + jnp.exp2(sk - m2)
                out = out * f
            out = out * (1.0 / l)
            if gate_branch is not None:
                gi = gate_branch * N_HEADS + hg * nh + h
                out = out * jax.nn.sigmoid(gt_ref[0, 0, pl.ds(gi, 1), :])
            outs.append(out)
        tile_t = jnp.concatenate(outs, axis=0)
        o_ref[0, :, tt * LANES:(tt + 1) * LANES] = tile_t.T.astype(o_ref.dtype)


def _flash(q, k, vt, *, nt, nkv, tq, tk, head_slot, head_half, out_dtype, head_group=None,
           window=None, nsel=None, emat=None, ck=None, cq=None, sinks=None, gates_t=None,
           gate_branch=None):
    bsz, seq, qw = q.shape
    n_hg = qw // (nt * LANES)
    nq = seq // tq
    nk = seq // tk
    nh = 2 * nt
    use_sel = nsel is not None
    use_fox = ck is not None
    use_sink = sinks is not None
    kw = 2 * LANES if (use_sel or use_fox) else LANES
    in_specs = [
        pl.BlockSpec((1, tq, nt * LANES), lambda b, g, i: (b, i, g)),
        pl.BlockSpec((1, seq, nkv * LANES), lambda b, g, i: (b, 0, g)),
        pl.BlockSpec((1, nk, nkv * LANES, tk), lambda b, g, i: (b, 0, g, 0)),
    ]
    args = [q, k, vt]
    if use_sel:
        ngrp = max(head_group) + 1
        in_specs += [
            pl.BlockSpec((1, ngrp, tq, LANES), lambda b, g, i: (b, g, i, 0)),
            pl.BlockSpec(emat.shape, lambda b, g, i: (0, 0, 0)),
        ]
        args += [nsel, emat]
    if use_fox:
        in_specs += [
            pl.BlockSpec((1, nh, seq, LANES), lambda b, g, i: (b, g, 0, 0)),
            pl.BlockSpec((1, nh, tq, LANES), lambda b, g, i: (b, g, i, 0)),
        ]
        args += [ck, cq]
    if use_sink:
        in_specs.append(pl.BlockSpec(memory_space=pltpu.SMEM))
        args.append(sinks)
    if gate_branch is not None:
        in_specs.append(pl.BlockSpec((1, 1, LANES, tq), lambda b, g, i: (b, i, 0, 0)))
        args.append(gates_t)
    kern = functools.partial(
        _flash_kernel, nt=nt, tq=tq, tk=tk, window=window, use_sel=use_sel, use_fox=use_fox,
        use_sink=use_sink, gate_branch=gate_branch, head_slot=head_slot, head_half=head_half,
        head_group=head_group)
    return pl.pallas_call(
        kern,
        grid=(bsz, n_hg, nq),
        in_specs=in_specs,
        out_specs=pl.BlockSpec((1, tq, nt * LANES), lambda b, g, i: (b, i, g)),
        out_shape=jax.ShapeDtypeStruct((bsz, seq, qw), out_dtype),
        scratch_shapes=[
            pltpu.VMEM((nh, tq, kw), BF16),
            pltpu.VMEM((nh, 1, tq), F32),
            pltpu.VMEM((nh, LANES, tq), F32),
            pltpu.VMEM((2, nh, tk, tq), F32),
            pltpu.VMEM((2, nh, tk, tq), BF16),
            pltpu.VMEM((2, nh, 1, tq), F32),
        ],
        compiler_params=_cparams(("parallel", "parallel", "arbitrary")),
        name="flash",
    )(*args)


def _banded_kernel(*refs, nt, tq, window, use_sink, gate_branch, head_half):
    it = iter(refs)
    q_ref = next(it)
    k_ref = next(it)
    vt_ref = next(it)
    if use_sink:
        sink_ref = next(it)
    if gate_branch is not None:
        gt_ref = next(it)
    o_ref = next(it)
    qal_ref = next(it)
    s_ref = next(it)

    hg = pl.program_id(1)
    i = pl.program_id(2)
    nh = 2 * nt
    span = window + tq
    q0 = i * tq
    k_start = pl.multiple_of(jnp.maximum(q0 - window, 0), tq)

    lane = lax.broadcasted_iota(jnp.int32, (1, LANES), 1)
    half = lane // HALF
    for tt in range(nt):
        qt = q_ref[0, :, tt * LANES:(tt + 1) * LANES].astype(F32)
        qr = pltpu.roll(qt, HALF, 1) if any(head_half[2 * tt + a] != a for a in range(2)) else None
        for a in range(2):
            h = 2 * tt + a
            src = qt if head_half[h] == a else qr
            qal_ref[h] = jnp.where(half == head_half[h], src, 0.0).astype(BF16)

    kspan = k_ref[0, pl.ds(k_start, span), :]
    for h in range(nh):
        s_ref[h] = _nt_dot(kspan, qal_ref[h])

    d = (q0 - k_start) + (lax.broadcasted_iota(jnp.int32, (span, tq), 1)
                          - lax.broadcasted_iota(jnp.int32, (span, tq), 0))
    mask = (d >= 0) & (d < window)
    c0 = k_start // tq
    vt = jnp.concatenate([vt_ref[0, c0 + j] for j in range(span // tq)], axis=1)
    row_half = lax.broadcasted_iota(jnp.int32, (LANES, 1), 0) // HALF
    vth = {kh: jnp.where(row_half == kh, vt, jnp.ones_like(vt)) for kh in set(head_half)}

    for tt in range(nt):
        outs = []
        for a in range(2):
            h = 2 * tt + a
            kh = head_half[h]
            s = jnp.where(mask, s_ref[h], NEG)
            m = jnp.max(s, axis=0, keepdims=True)
            p = jnp.exp2(s - m).astype(BF16)
            acc = _dot(vth[kh], p)
            l = acc[(1 - kh) * HALF:(1 - kh) * HALF + 1]
            out = acc[kh * HALF:(kh + 1) * HALF]
            if use_sink:
                sk = sink_ref[hg * nh + h] * LOG2E
                m2 = jnp.maximum(m, sk)
                f = jnp.exp2(m - m2)
                l = l * f + jnp.exp2(sk - m2)
                out = out * f
            out = out * (1.0 / l)
            if gate_branch is not None:
                gi = gate_branch * N_HEADS + hg * nh + h
                out = out * jax.nn.sigmoid(gt_ref[0, 0, pl.ds(gi, 1), :])
            outs.append(out)
        tile_t = jnp.concatenate(outs, axis=0)
        o_ref[0, :, tt * LANES:(tt + 1) * LANES] = tile_t.T.astype(o_ref.dtype)


def _banded(q, k, vt, *, nt, tq, window, head_half, out_dtype, sinks=None, gates_t=None,
            gate_branch=None):
    bsz, seq, qw = q.shape
    n_hg = qw // (nt * LANES)
    nh = 2 * nt
    use_sink = sinks is not None
    in_specs = [
        pl.BlockSpec((1, tq, nt * LANES), lambda b, g, i: (b, i, g)),
        pl.BlockSpec((1, seq, LANES), lambda b, g, i: (b, 0, g)),
        pl.BlockSpec((1, seq // tq, LANES, tq), lambda b, g, i: (b, 0, g, 0)),
    ]
    args = [q, k, vt]
    if use_sink:
        in_specs.append(pl.BlockSpec(memory_space=pltpu.SMEM))
        args.append(sinks)
    if gate_branch is not None:
        in_specs.append(pl.BlockSpec((1, 1, LANES, tq), lambda b, g, i: (b, i, 0, 0)))
        args.append(gates_t)
    kern = functools.partial(_banded_kernel, nt=nt, tq=tq, window=window, use_sink=use_sink,
                             gate_branch=gate_branch, head_half=head_half)
    return pl.pallas_call(
        kern,
        grid=(bsz, n_hg, seq // tq),
        in_specs=in_specs,
        out_specs=pl.BlockSpec((1, tq, nt * LANES), lambda b, g, i: (b, i, g)),
        out_shape=jax.ShapeDtypeStruct((bsz, seq, qw), out_dtype),
        scratch_shapes=[
            pltpu.VMEM((nh, tq, LANES), BF16),
            pltpu.VMEM((nh, window + tq, tq), F32),
        ],
        compiler_params=_cparams(("parallel", "parallel", "arbitrary")),
        name="banded",
    )(*args)


N_CHUNK16 = 128


def _compress_hidden(x_ref, pe_ref, w1_ref):
    hid = NSA_GROUPS * NSA_CMP_HIDDEN
    x = x_ref[0]
    top = _dot((x + pe_ref[0:1, :]).astype(BF16), w1_ref[:, :hid])
    bot = _dot((x + pe_ref[1:2, :]).astype(BF16), w1_ref[:, hid:])
    h1 = top + pltpu.roll(bot, N_CHUNK16 - 1, 0)
    return jax.nn.gelu(h1, approximate=True).astype(BF16)


def _compress_kernel(xk_ref, xv_ref, pek_ref, pev_ref, wk1_ref, wv1_ref, wk2_ref, wv2t_ref,
                     kc_ref, vct_ref):
    kc_ref[0] = _dot(_compress_hidden(xk_ref, pek_ref, wk1_ref), wk2_ref[...]).astype(kc_ref.dtype)
    vct_ref[0] = _nt_dot(wv2t_ref[...], _compress_hidden(xv_ref, pev_ref, wv1_ref)).astype(vct_ref.dtype)


def _compress(xk, xv, pek, pev, wk1, wv1, wk2, wv2t):
    bsz = xk.shape[0]
    width = xk.shape[2]
    gd = NSA_GROUPS * HEAD_DIM
    xspec = pl.BlockSpec((1, N_CHUNK16, width), lambda b: (b, 0, 0))
    full = lambda a: pl.BlockSpec(a.shape, lambda b: (0,) * a.ndim)
    return pl.pallas_call(
        _compress_kernel,
        grid=(bsz,),
        in_specs=[xspec, xspec, full(pek), full(pev), full(wk1), full(wv1), full(wk2), full(wv2t)],
        out_specs=[pl.BlockSpec((1, N_CHUNK16, gd), lambda b: (b, 0, 0)),
                   pl.BlockSpec((1, gd, N_CHUNK16), lambda b: (b, 0, 0))],
        out_shape=[jax.ShapeDtypeStruct((bsz, N_CHUNK16, gd), BF16),
                   jax.ShapeDtypeStruct((bsz, gd, N_CHUNK16), BF16)],
        compiler_params=_cparams(("parallel",)),
        name="compress",
    )(xk, xv, pek, pev, wk1, wv1, wk2, wv2t)


CMP_TQ = 256
N_SLC = 32


def _cmp_kernel(q_ref, kc_ref, vct_ref, gt_ref, ovt_ref, o_ref, nsel_ref, qal_ref):
    tq = CMP_TQ
    g = pl.program_id(1)
    i = pl.program_id(2)
    kv_half = jnp.zeros((1, 1), jnp.int32) + (g % 2)
    _align_queries(q_ref, qal_ref, 2, kv_half)
    kc = kc_ref[0]
    vct = vct_ref[0]
    t_row = i * tq + lax.broadcasted_iota(jnp.int32, (1, tq), 1)
    n_col = lax.broadcasted_iota(jnp.int32, (N_CHUNK16, 1), 0)
    n_cmp = (N_CHUNK16 * NSA_CMP_STRIDE - NSA_CMP_LEN) // NSA_CMP_STRIDE + 1
    valid = (n_col * NSA_CMP_STRIDE + (NSA_CMP_LEN - 1) <= t_row) & (n_col < n_cmp)

    scores = [_nt_dot(kc, qal_ref[h]) for h in range(4)]
    probs = []
    for h in range(4):
        s = jnp.where(valid, scores[h], NEG)
        m = jnp.max(s, axis=0, keepdims=True)
        e = jnp.where(valid, jnp.exp2(s - m), 0.0)
        l = jnp.sum(e, axis=0, keepdims=True)
        probs.append(e * (1.0 / jnp.where(l > 0.0, l, 1.0)))
    psum = (probs[0] + probs[1]) + (probs[2] + probs[3])
    outs = [_pick_half(_dot(vct, p.astype(BF16)), kv_half) for p in probs]
    for tt in range(2):
        gated = [outs[2 * tt + a] * jax.nn.sigmoid(gt_ref[0, 0, pl.ds(g * 4 + 2 * tt + a, 1), :])
                 for a in range(2)]
        o_ref[0, :, tt * LANES:(tt + 1) * LANES] = jnp.concatenate(gated, axis=0).T

    p_hi = psum.astype(BF16)
    p_lo = (psum - p_hi.astype(F32)).astype(BF16)
    ovt = ovt_ref[...]
    imp = _dot(ovt, p_hi) + _dot(ovt, p_lo)
    j = lax.broadcasted_iota(jnp.int32, (N_SLC, 1), 0)
    tb = jnp.right_shift(t_row, 6)
    forced = (j == 0) | (j == tb) | (j == tb - 1)
    imp = jnp.where(j > tb, NEG, jnp.where(forced, NSA_FORCE_BONUS, imp))
    cnt = jnp.zeros((N_SLC, tq), jnp.int32)
    for jp in range(N_SLC):
        row = imp[jp:jp + 1, :]
        beats = (row > imp) | ((row == imp) & (jp < j))
        cnt = cnt + beats.astype(jnp.int32)
    sel_t = (cnt < NSA_TOPK).astype(F32)
    sel_t = jnp.concatenate([sel_t, jnp.ones((LANES - N_SLC, tq), F32)], axis=0)
    nsel_ref[0, 0] = ((sel_t.T - 1.0) * SEL_BIG).astype(nsel_ref.dtype)


def _cmp_attn(q, kc, vct, gates_t, ovt):
    bsz, seq, _ = q.shape
    tq = CMP_TQ
    gw = 2 * LANES
    return pl.pallas_call(
        _cmp_kernel,
        grid=(bsz, NSA_GROUPS, seq // tq),
        in_specs=[
            pl.BlockSpec((1, tq, gw), lambda b, g, i: (b, i, g)),
            pl.BlockSpec((1, N_CHUNK16, LANES), lambda b, g, i: (b, 0, g // 2)),
            pl.BlockSpec((1, LANES, N_CHUNK16), lambda b, g, i: (b, g // 2, 0)),
            pl.BlockSpec((1, 1, LANES, tq), lambda b, g, i: (b, i, 0, 0)),
            pl.BlockSpec(ovt.shape, lambda b, g, i: (0, 0)),
        ],
        out_specs=[
            pl.BlockSpec((1, tq, gw), lambda b, g, i: (b, i, g)),
            pl.BlockSpec((1, 1, tq, LANES), lambda b, g, i: (b, g, i, 0)),
        ],
        out_shape=[
            jax.ShapeDtypeStruct((bsz, seq, D_MODEL), F32),
            jax.ShapeDtypeStruct((bsz, NSA_GROUPS, seq, LANES), BF16),
        ],
        scratch_shapes=[pltpu.VMEM((4, tq, LANES), BF16)],
        compiler_params=_cparams(("parallel", "parallel", "arbitrary")),
        name="cmp_attn",
    )(q, kc, vct, gates_t, ovt)


CS_BLK = 256


def _cumsum_kernel(z_ref, b_ref, c_ref):
    seq, width = z_ref.shape
    tri = (lax.broadcasted_iota(jnp.int32, (CS_BLK, CS_BLK), 0)
           >= lax.broadcasted_iota(jnp.int32, (CS_BLK, CS_BLK), 1)).astype(BF16)
    carry = jnp.zeros((1, width), F32)
    for blk in range(seq // CS_BLK):
        x = jax.nn.log_sigmoid(z_ref[blk * CS_BLK:(blk + 1) * CS_BLK, :] + b_ref[...])
        hi = x.astype(BF16)
        r1 = x - hi.astype(F32)
        mid = r1.astype(BF16)
        lo = (r1 - mid.astype(F32)).astype(BF16)
        cs = _dot(tri, hi) + _dot(tri, mid) + _dot(tri, lo) + carry
        c_ref[blk * CS_BLK:(blk + 1) * CS_BLK, :] = cs * LOG2E
        carry = cs[CS_BLK - 1:CS_BLK, :]


def _cumsum(z, bias):
    seq, width = z.shape
    return pl.pallas_call(
        _cumsum_kernel,
        grid=(width // LANES,),
        in_specs=[pl.BlockSpec((seq, LANES), lambda j: (0, j)),
                  pl.BlockSpec((1, LANES), lambda j: (0, j))],
        out_specs=pl.BlockSpec((seq, LANES), lambda j: (0, j)),
        out_shape=jax.ShapeDtypeStruct((seq, width), F32),
        compiler_params=_cparams(("parallel",)),
        name="cumsum",
    )(z, bias)


def _fox_bias_kernel(c_ref, ck_ref, cq_ref):
    li = pl.program_id(0) % LANES
    c = c_ref[...]
    lane = lax.broadcasted_iota(jnp.int32, c.shape, 1)
    col = jnp.sum(jnp.where(lane == li, c, 0.0), axis=1, keepdims=True)
    hi = col.astype(BF16).astype(F32)
    mid = (col - hi).astype(BF16).astype(F32)
    lo = col - hi - mid
    k_piece = jnp.where(lane == 0, hi, jnp.where(lane == 1, mid, lo))
    q_piece = jnp.where(lane == 3, hi, jnp.where(lane == 4, mid, lo))
    ck_ref[0] = jnp.where(lane < 3, k_piece, jnp.where(lane < 6, 1.0, 0.0)).astype(BF16)
    cq_ref[0] = jnp.where(lane < 3, -1.0, jnp.where(lane < 6, q_piece, 0.0)).astype(BF16)


def _fox_bias(c, n):
    seq = c.shape[0]
    ospec = pl.BlockSpec((1, seq, LANES), lambda j: (j, 0, 0))
    oshape = jax.ShapeDtypeStruct((n, seq, LANES), BF16)
    return pl.pallas_call(
        _fox_bias_kernel,
        grid=(n,),
        in_specs=[pl.BlockSpec((seq, LANES), lambda j: (0, j // LANES))],
        out_specs=[ospec, ospec],
        out_shape=[oshape, oshape],
        compiler_params=_cparams(("parallel",)),
        name="fox_bias",
    )(c)


def _rope_tables(seq):
    inv = ROPE_THETA ** (-jnp.arange(0, HEAD_DIM, 2, dtype=F32) / HEAD_DIM)
    ang = jnp.arange(seq, dtype=F32)[:, None] * inv[None, :]
    cos, sin = jnp.cos(ang), jnp.sin(ang)
    cos_t = jnp.tile(cos, (1, LANES // (HEAD_DIM // 2)))
    sin_t = jnp.tile(jnp.concatenate([-sin, sin], axis=1), (1, LANES // HEAD_DIM))
    return cos_t, sin_t


def _overlap_t(seq):
    n_cmp = (seq - NSA_CMP_LEN) // NSA_CMP_STRIDE + 1
    n_slc = seq // NSA_SLC_LEN
    cs = np.arange(n_cmp) * NSA_CMP_STRIDE
    ce = cs + NSA_CMP_LEN
    ss = np.arange(n_slc) * NSA_SLC_LEN
    se = ss + NSA_SLC_LEN
    ov = np.clip(np.minimum(ce[:, None], se[None, :]) - np.maximum(cs[:, None], ss[None, :]), 0, None)
    ov = (ov / NSA_CMP_LEN).astype(np.float32)
    ovt = np.zeros((n_slc, LANES), np.float32)
    ovt[:, :n_cmp] = ov.T
    return jnp.asarray(ovt, dtype=BF16)


def _expand_mat_t(seq, tk):
    key = np.arange(seq)
    e = ((key // NSA_SLC_LEN)[:, None] == np.arange(LANES)[None, :]).astype(np.float32)
    return jnp.asarray(e.reshape(seq // tk, tk, LANES), dtype=BF16)


def _compress_weights(pe, w1, w2):
    g = NSA_GROUPS
    half = NSA_CMP_LEN // 2
    eye = jnp.eye(g, dtype=F32)
    w1r = w1.reshape(NSA_CMP_LEN, HEAD_DIM, NSA_CMP_HIDDEN)

    def big(part):
        return jnp.einsum('ldj,gh->lgdhj', part, eye).reshape(half * g * HEAD_DIM, g * NSA_CMP_HIDDEN)

    w1big = jnp.concatenate([big(w1r[:half]), big(w1r[half:])], axis=1).astype(BF16)
    w2big = jnp.einsum('jd,gh->gjhd', w2, eye).reshape(g * NSA_CMP_HIDDEN, g * HEAD_DIM).astype(BF16)

    def pebig(part):
        return jnp.broadcast_to(part[:, None, :], (half, g, HEAD_DIM)).reshape(-1)

    pe2 = jnp.stack([pebig(pe[:half]), pebig(pe[half:])], axis=0)
    pe2 = jnp.concatenate([pe2, jnp.zeros((6, pe2.shape[1]), F32)], axis=0)
    return pe2, w1big, w2big


def _pad_cols(w, width):
    return jnp.concatenate([w, jnp.zeros((w.shape[0], width - w.shape[1]), w.dtype)], axis=1)


ATT_T = 256


def _nsa_mixer(x2, bsz, seq, gain, w_in, ck_pe, ck_w1, ck_w2, cv_pe, cv_w1, cv_w2, w_out,
               rope_tabs, ovt, emat_t):
    qw = N_HEADS * HEAD_DIM
    gd = NSA_GROUPS * HEAD_DIM

    def kvcols(c, s):
        lo = qw + (c * 2 + s) * gd
        return w_in[:, lo:lo + gd]

    w = jnp.concatenate([w_in[:, :qw]] + [kvcols(c, 0) for c in range(3)] + [kvcols(0, 1)],
                        axis=1).astype(BF16)
    wt = jnp.concatenate([kvcols(1, 1), kvcols(2, 1), _pad_cols(w_in[:, qw + 6 * gd:], LANES)],
                         axis=1).T.astype(BF16)
    scale = HEAD_DIM ** -0.5 * LOG2E
    row_groups = ((0, qw, True, scale), (qw, gd, True, 1.0), (qw + gd, gd, True, 1.0),
                  (qw + 2 * gd, gd, True, 1.0), (qw + 3 * gd, gd, False, 1.0))
    t_groups = ((0, gd, ATT_T), (gd, gd, ATT_T), (2 * gd, LANES, ATT_T))
    q, k0, k1, k2, v0, v1t, v2t, gates_t = _proj(
        x2, gain, w, row_groups, (BF16, F32, BF16, BF16, F32), seq, wt=wt, t_groups=t_groups,
        t_dtypes=(BF16, BF16, F32), rope_tabs=rope_tabs)
    r3 = lambda a: a.reshape(bsz, seq, a.shape[-1])
    q, k1, k2 = map(r3, (q, k1, k2))

    chunkw = NSA_CMP_STRIDE * gd
    xk = k0.reshape(bsz, seq // NSA_CMP_STRIDE, chunkw)
    xv = v0.reshape(bsz, seq // NSA_CMP_STRIDE, chunkw)
    pek, wk1, wk2 = _compress_weights(ck_pe, ck_w1, ck_w2)
    pev, wv1, wv2 = _compress_weights(cv_pe, cv_w1, cv_w2)
    kc, vct = _compress(xk, xv, pek, pev, wk1, wv1, wk2, wv2.T)

    o_cmp, nsel = _cmp_attn(q, kc, vct, gates_t, ovt)
    grp = tuple(h // 4 for h in range(8))
    common = dict(nt=4, nkv=1, tq=ATT_T, tk=ATT_T, head_slot=(0,) * 8, head_half=grp,
                  out_dtype=F32, gates_t=gates_t)
    o_slc = _flash(q, k1, v1t, head_group=grp, nsel=nsel, emat=emat_t, gate_branch=1, **common)
    o_win = _banded(q, k2, v2t, nt=4, tq=ATT_T, window=NSA_WINDOW, head_half=grp, out_dtype=F32,
                    gates_t=gates_t, gate_branch=2)
    flat = lambda a: a.reshape(bsz * seq, D_MODEL)
    return _outproj([flat(o_cmp), flat(o_slc), flat(o_win)], w_out.astype(BF16), x2)


def _swa_mixer(x2, bsz, seq, gain, w_in, sinks, w_out, rope_tabs):
    qw = N_HEADS * HEAD_DIM
    scale = HEAD_DIM ** -0.5 * LOG2E
    row_groups = ((0, qw, True, scale), (qw, LANES, True, 1.0))
    w = w_in[:, :qw + LANES].astype(BF16)
    wt = w_in[:, qw + LANES:].T.astype(BF16)
    q, k, vt = _proj(x2, gain, w, row_groups, (BF16, BF16), seq, wt=wt,
                     t_groups=((0, LANES, SWA_WINDOW),), t_dtypes=(BF16,), rope_tabs=rope_tabs)
    r3 = lambda a: a.reshape(bsz, seq, a.shape[-1])
    o = _banded(r3(q), r3(k), vt, nt=8, tq=SWA_WINDOW, window=SWA_WINDOW,
                head_half=tuple(h // 8 for h in range(16)), out_dtype=BF16, sinks=sinks)
    return _outproj([o.reshape(bsz * seq, D_MODEL)], w_out.astype(BF16), x2)


def _fox_mixer(x2, bsz, seq, gain, w_in, b_f, w_out):
    qw = N_HEADS * HEAD_DIM
    scale = HEAD_DIM ** -0.5 * LOG2E
    w = jnp.concatenate([w_in[:, :2 * qw], _pad_cols(w_in[:, 3 * qw:], LANES)], axis=1).astype(BF16)
    wt = w_in[:, 2 * qw:3 * qw].T.astype(BF16)
    row_groups = ((0, qw, False, scale), (qw, qw, False, 1.0), (2 * qw, LANES, False, 1.0))
    q, k, f, vt = _proj(x2, gain, w, row_groups, (BF16, BF16, F32), seq, wt=wt,
                        t_groups=((0, qw, ATT_T),), t_dtypes=(BF16,))
    r3 = lambda a: a.reshape(bsz, seq, a.shape[-1])

    bh = bsz * N_HEADS
    bhp = -(-bh // LANES) * LANES
    z = f.reshape(bsz, seq, LANES)[:, :, :N_HEADS].transpose(1, 0, 2).reshape(seq, bh)
    z = _pad_cols(z, bhp)
    bias = _pad_cols(jnp.tile(b_f, bsz).reshape(1, bh), bhp)
    c = _cumsum(z, bias)
    ck, cq = _fox_bias(c, bh)
    r4 = lambda a: a.reshape(bsz, N_HEADS, seq, LANES)
    o = _flash(r3(q), r3(k), vt, nt=4, nkv=4, tq=ATT_T, tk=ATT_T,
               head_slot=tuple(h // 2 for h in range(8)), head_half=(0, 1) * 4, out_dtype=BF16,
               ck=r4(ck), cq=r4(cq))
    return _outproj([o.reshape(bsz * seq, D_MODEL)], w_out.astype(BF16), x2)


def kernel(x, ffn1_norm, ffn1_w_gu, ffn1_w_down, mix_norm, ffn2_norm, ffn2_w_gu, ffn2_w_down,
           nsa_w_in, nsa_ck_pe, nsa_ck_w1, nsa_ck_w2, nsa_cv_pe, nsa_cv_w1, nsa_cv_w2, nsa_w_out,
           swa_w_in, swa_sinks, swa_w_out, fox_w_in, fox_b_f, fox_w_out, final_norm):
    bsz, seq, _ = x.shape
    depth = ffn1_norm.shape[0]
    rope_tabs = _rope_tables(seq)
    ovt = _overlap_t(seq)
    emat_t = _expand_mat_t(seq, ATT_T)
    x2 = x.reshape(bsz * seq, D_MODEL)
    for i in range(depth):
        kind, j = i % N_MIXERS, i // N_MIXERS
        x2 = _ffn(x2, ffn1_norm[i], ffn1_w_gu[i].astype(BF16), ffn1_w_down[i].astype(BF16))
        if kind == 0:
            x2 = _nsa_mixer(x2, bsz, seq, mix_norm[i], nsa_w_in[j], nsa_ck_pe[j], nsa_ck_w1[j],
                            nsa_ck_w2[j], nsa_cv_pe[j], nsa_cv_w1[j], nsa_cv_w2[j], nsa_w_out[j],
                            rope_tabs, ovt, emat_t)
        elif kind == 1:
            x2 = _swa_mixer(x2, bsz, seq, mix_norm[i], swa_w_in[j], swa_sinks[j], swa_w_out[j],
                            rope_tabs)
        else:
            x2 = _fox_mixer(x2, bsz, seq, mix_norm[i], fox_w_in[j], fox_b_f[j], fox_w_out[j])
        last = i == depth - 1
        x2 = _ffn(x2, ffn2_norm[i], ffn2_w_gu[i].astype(BF16), ffn2_w_down[i].astype(BF16),
                  final_gain=final_norm if last else None)
    return x2.reshape(bsz, seq, D_MODEL)
```

```python
import functools

import numpy as np
import jax
import jax.numpy as jnp
from jax import lax
from jax.experimental import pallas as pl
from jax.experimental.pallas import tpu as pltpu

D_MODEL = 1024
HEAD_DIM = 64
N_HEADS = 16
D_FF = 2816
RMS_EPS = 1e-6
ROPE_THETA = 10000.0
NEG = -1e30
SEL_BIG = float(2 ** 100)
LOG2E = 1.4426950408889634

NSA_GROUPS = 4
NSA_CMP_LEN = 32
NSA_CMP_STRIDE = 16
NSA_CMP_HIDDEN = 128
NSA_SLC_LEN = 64
NSA_TOPK = 16
NSA_WINDOW = 512
NSA_FORCE_BONUS = 1e4
SWA_WINDOW = 128
N_MIXERS = 3

LANES = 128
HALF = 64
VMEM_LIMIT = 56 * 1024 * 1024

F32 = jnp.float32
BF16 = jnp.bfloat16


def _nt_dot(a, b):
    return lax.dot_general(a, b, (((1,), (1,)), ((), ())), preferred_element_type=F32)


def _dot(a, b):
    return jnp.dot(a, b, preferred_element_type=F32)


def _rms(x, g):
    ms = jnp.mean(x * x, axis=-1, keepdims=True)
    return x * lax.rsqrt(ms + RMS_EPS) * g


def _cparams(sem):
    return pltpu.CompilerParams(dimension_semantics=sem, vmem_limit_bytes=VMEM_LIMIT)


FFN_TM = 512
FFN_TF = 256
FFN_NC = 16


def _ffn_kernel(*refs, n_in, final):
    it = iter(refs)
    x_ref = next(it)
    o_refs = [next(it) for _ in range(n_in)]
    g_ref = next(it)
    wgu_ref = next(it)
    wd_ref = next(it)
    wo_ref = next(it) if n_in else None
    fg_ref = next(it) if final else None
    out_ref = next(it)
    wgu_s = next(it)
    wd_s = next(it)
    wo_s = next(it) if n_in else None

    j = pl.program_id(0)

    @pl.when(j < FFN_NC)
    def _():
        for src, dst in ((wgu_ref, wgu_s), (wd_ref, wd_s)) + (((wo_ref, wo_s),) if n_in else ()):
            rows = src.shape[0]
            dst[pl.ds(pl.multiple_of(j * rows, rows), rows), :] = src[...].astype(BF16)

    @pl.when(j >= FFN_NC)
    def _():
        x = x_ref[...]
        if n_in:
            o = o_refs[0][...].astype(F32)
            for r in o_refs[1:]:
                o = o + r[...].astype(F32)
            x = x + _dot(o.astype(BF16), wo_s[...])
        hb = _rms(x, g_ref[...]).astype(BF16)
        acc = jnp.zeros(x.shape, F32)
        for f in range(D_FF // FFN_TF):
            lo = f * FFN_TF
            g = _dot(hb, wgu_s[:, lo:lo + FFN_TF])
            u = _dot(hb, wgu_s[:, D_FF + lo:D_FF + lo + FFN_TF])
            a = (g * jax.nn.sigmoid(g)) * u
            acc = acc + _dot(a.astype(BF16), wd_s[lo:lo + FFN_TF, :])
        y = x + 0.5 * acc
        if final:
            y = _rms(y, fg_ref[...])
        out_ref[...] = y


def _ffn(x2, gain, wgu, wd, branches=(), w_out=None, final_gain=None):
    n = x2.shape[0]
    n_in = len(branches)
    final = final_gain is not None
    slab = lambda j: (jnp.minimum(j, FFN_NC - 1), 0)
    row = pl.BlockSpec((FFN_TM, D_MODEL), lambda j: (jnp.maximum(j - FFN_NC, 0), 0))
    vec = pl.BlockSpec((1, D_MODEL), lambda j: (0, 0))
    in_specs = [row] + [row] * n_in + [
        vec,
        pl.BlockSpec((D_MODEL // FFN_NC, 2 * D_FF), slab),
        pl.BlockSpec((D_FF // FFN_NC, D_MODEL), slab),
    ]
    args = [x2, *branches, gain.reshape(1, D_MODEL), wgu, wd]
    scratch = [pltpu.VMEM((D_MODEL, 2 * D_FF), BF16), pltpu.VMEM((D_FF, D_MODEL), BF16)]
    if n_in:
        in_specs.append(pl.BlockSpec((D_MODEL // FFN_NC, D_MODEL), slab))
        args.append(w_out)
        scratch.append(pltpu.VMEM((D_MODEL, D_MODEL), BF16))
    if final:
        in_specs.append(vec)
        args.append(final_gain.reshape(1, D_MODEL))
    return pl.pallas_call(
        functools.partial(_ffn_kernel, n_in=n_in, final=final),
        grid=(FFN_NC + n // FFN_TM,),
        in_specs=in_specs,
        out_specs=row,
        out_shape=jax.ShapeDtypeStruct((n, D_MODEL), F32),
        scratch_shapes=scratch,
        compiler_params=_cparams(("arbitrary",)),
        name="ffn",
    )(*args)


PROJ_TM = 512
PROJ_CH = 256


def _rope_tile(y, cos_t, sin_t, first_half):
    rot = jnp.where(first_half, pltpu.roll(y, 96, 1), pltpu.roll(y, 32, 1))
    return y * cos_t + rot * sin_t


def _proj_kernel(*refs, row_groups, t_groups, use_rope):
    it = iter(refs)
    x_ref = next(it)
    g_ref = next(it)
    w_ref = next(it)
    wt_ref = next(it) if t_groups else None
    if use_rope:
        cos_t = next(it)[...]
        sin_t = next(it)[...]
        lane = lax.broadcasted_iota(jnp.int32, (1, LANES), 1)
        first_half = (lane % HALF) < (HALF // 2)
    o_refs = list(it)
    hb = _rms(x_ref[...], g_ref[...]).astype(BF16)
    tm = hb.shape[0]
    for (c0, width, rope, scale), o_ref in zip(row_groups, o_refs):
        for t0 in range(0, width, PROJ_CH):
            ch = min(PROJ_CH, width - t0)
            y = _dot(hb, w_ref[:, c0 + t0:c0 + t0 + ch])
            for l0 in range(0, ch, LANES):
                yt = y[:, l0:l0 + LANES]
                if rope:
                    yt = _rope_tile(yt, cos_t, sin_t, first_half)
                if scale != 1.0:
                    yt = yt * scale
                o_ref[:, t0 + l0:t0 + l0 + LANES] = yt.astype(o_ref.dtype)
    for (c0, width, ck), o_ref in zip(t_groups, o_refs[len(row_groups):]):
        for t0 in range(0, width, PROJ_CH):
            ch = min(PROJ_CH, width - t0)
            yt = _nt_dot(wt_ref[c0 + t0:c0 + t0 + ch, :], hb)
            for s0 in range(0, tm, ck):
                o_ref[0, s0 // ck, t0:t0 + ch, :] = yt[:, s0:s0 + ck].astype(o_ref.dtype)


def _proj(x2, gain, w, row_groups, row_dtypes, seq, wt=None, t_groups=(), t_dtypes=(),
          rope_tabs=None):
    n = x2.shape[0]
    bsz = n // seq
    nblk = seq // PROJ_TM
    use_rope = rope_tabs is not None
    resident = dict(pipeline_mode=pl.Buffered(1))
    in_specs = [
        pl.BlockSpec((PROJ_TM, D_MODEL), lambda i: (i, 0)),
        pl.BlockSpec((1, D_MODEL), lambda i: (0, 0)),
        pl.BlockSpec(w.shape, lambda i: (0, 0), **resident),
    ]
    args = [x2, gain.reshape(1, D_MODEL), w]
    if t_groups:
        in_specs.append(pl.BlockSpec(wt.shape, lambda i: (0, 0), **resident))
        args.append(wt)
    if use_rope:
        in_specs += [pl.BlockSpec((PROJ_TM, LANES), lambda i: (i % nblk, 0))] * 2
        args += list(rope_tabs)
    out_specs = [pl.BlockSpec((PROJ_TM, g[1]), lambda i: (i, 0)) for g in row_groups]
    out_shape = [jax.ShapeDtypeStruct((n, g[1]), dt) for g, dt in zip(row_groups, row_dtypes)]
    for (_, width, ck), dt in zip(t_groups, t_dtypes):
        out_specs.append(pl.BlockSpec((1, PROJ_TM // ck, width, ck),
                                      lambda i: (i // nblk, i % nblk, 0, 0)))
        out_shape.append(jax.ShapeDtypeStruct((bsz, seq // ck, width, ck), dt))
    return pl.pallas_call(
        functools.partial(_proj_kernel, row_groups=row_groups, t_groups=t_groups,
                          use_rope=use_rope),
        grid=(n // PROJ_TM,),
        in_specs=in_specs,
        out_specs=out_specs,
        out_shape=out_shape,
        compiler_params=_cparams(("parallel",)),
        name="proj",
    )(*args)


def _align_queries(q_ref, qal_ref, nt, kv_half):
    lane = lax.broadcasted_iota(jnp.int32, (1, LANES), 1)
    half = lane // HALF
    for tt in range(nt):
        qt = q_ref[0, :, tt * LANES:(tt + 1) * LANES].astype(F32)
        qr = pltpu.roll(qt, HALF, 1)
        for a in range(2):
            qa = jnp.where(half == kv_half, jnp.where(kv_half == a, qt, qr), 0.0)
            qal_ref[2 * tt + a] = qa.astype(BF16)


def _pick_half(x, kv_half):
    return jnp.where(kv_half == 0, x[:HALF], x[HALF:])


def _flash_kernel(*refs, nt, tq, tk, window, use_sel, use_fox, use_sink, gate_branch,
                  head_slot, head_half, head_group):
    it = iter(refs)
    q_ref = next(it)
    k_ref = next(it)
    vt_ref = next(it)
    if use_sel:
        nsel_ref = next(it)
        e_ref = next(it)
    if use_fox:
        ck_ref = next(it)
        cq_ref = next(it)
    if use_sink:
        sink_ref = next(it)
    if gate_branch is not None:
        gt_ref = next(it)
    o_ref = next(it)
    qal_ref = next(it)
    m_ref = next(it)
    acc_ref = next(it)
    s_ref = next(it)
    p_ref = next(it)
    a_ref = next(it)

    hg = pl.program_id(1)
    i = pl.program_id(2)
    nh = 2 * nt

    lane = lax.broadcasted_iota(jnp.int32, (1, LANES), 1)
    half = lane // HALF
    for tt in range(nt):
        qt = q_ref[0, :, tt * LANES:(tt + 1) * LANES].astype(F32)
        qr = pltpu.roll(qt, HALF, 1) if any(head_half[2 * tt + a] != a for a in range(2)) else None
        for a in range(2):
            h = 2 * tt + a
            src = qt if head_half[h] == a else qr
            qal_ref[h, :, :LANES] = jnp.where(half == head_half[h], src, 0.0).astype(BF16)
            if use_sel:
                qal_ref[h, :, LANES:] = nsel_ref[0, head_group[h]]
            if use_fox:
                qal_ref[h, :, LANES:] = cq_ref[0, h]

    m_ref[...] = jnp.full(m_ref.shape, NEG, F32)
    acc_ref[...] = jnp.zeros(acc_ref.shape, F32)

    row_half = lax.broadcasted_iota(jnp.int32, (LANES, 1), 0) // HALF
    q0 = i * tq
    colrow = (lax.broadcasted_iota(jnp.int32, (tk, tq), 1)
              - lax.broadcasted_iota(jnp.int32, (tk, tq), 0))

    if window is None:
        c_lo = 0
    else:
        c_lo = jnp.maximum(q0 - (window - 1), 0) // tk
    c_hi = (q0 + tq + tk - 1) // tk

    def scores(c, h):
        k0 = pl.multiple_of(c * tk, tk)
        sl = head_slot[h]
        kc = k_ref[0, pl.ds(k0, tk), sl * LANES:(sl + 1) * LANES]
        if use_sel:
            kc = jnp.concatenate([kc, e_ref[c]], axis=1)
        if use_fox:
            kc = jnp.concatenate([kc, ck_ref[0, h, pl.ds(k0, tk), :]], axis=1)
        return _nt_dot(kc, qal_ref[h])

    def values_t(c):
        made = {}
        for h in range(nh):
            key = (head_slot[h], head_half[h])
            if key not in made:
                vt = vt_ref[0, c, key[0] * LANES:(key[0] + 1) * LANES, :]
                made[key] = jnp.where(row_half == key[1], vt, jnp.ones_like(vt))
        return [made[(head_slot[h], head_half[h])] for h in range(nh)]

    def accumulate(vth, par, h):
        acc_ref[h] = a_ref[par, h] * acc_ref[h] + _dot(vth[h], p_ref[par, h])

    for h in range(nh):
        s_ref[0, h] = scores(c_lo, h)
    p_ref[1] = jnp.zeros(p_ref.shape[1:], BF16)
    a_ref[...] = jnp.ones(a_ref.shape, F32)

    def trip(c, par, masked):
        k0 = pl.multiple_of(c * tk, tk)
        if masked:
            d = colrow + (q0 - k0)
            mask = d >= 0
            if window is not None:
                mask = mask & (d < window)
        c_next = jnp.minimum(c + 1, c_hi - 1)
        vth = values_t(jnp.maximum(c - 1, c_lo))
        for h in range(nh):
            accumulate(vth, 1 - par, h)
            s_ref[1 - par, h] = scores(c_next, h)
            s = s_ref[par, h]
            if masked:
                s = jnp.where(mask, s, NEG)
            m_old = m_ref[h]
            m_new = jnp.maximum(m_old, jnp.max(s, axis=0, keepdims=True))
            a_ref[par, h] = jnp.exp2(m_old - m_new)
            p_ref[par, h] = jnp.exp2(s - m_new).astype(BF16)
            m_ref[h] = m_new

    def chunk(c, carry):
        odd = (c - c_lo) % 2
        k0 = c * tk
        edge = k0 + (tk - 1) > q0
        if window is not None:
            edge = edge | (k0 + window <= q0 + (tq - 1))
        for par in range(2):
            pl.when((odd == par) & edge)(functools.partial(trip, c, par, True))
            if window is None or window > tk:
                pl.when((odd == par) & jnp.logical_not(edge))(functools.partial(trip, c, par, False))
        return carry

    lax.fori_loop(c_lo, c_hi, chunk, 0)
    last_odd = (c_hi - 1 - c_lo) % 2

    def drain(par):
        vth = values_t(c_hi - 1)
        for h in range(nh):
            accumulate(vth, par, h)

    pl.when(last_odd == 0)(functools.partial(drain, 0))
    pl.when(last_odd == 1)(functools.partial(drain, 1))

    for tt in range(nt):
        outs = []
        for a in range(2):
            h = 2 * tt + a
            acc = acc_ref[h]
            kh = head_half[h]
            l = acc[(1 - kh) * HALF:(1 - kh) * HALF + 1]
            out = acc[kh * HALF:(kh + 1) * HALF]
            if use_sink:
                m = m_ref[h]
                sk = sink_ref[hg * nh + h] * LOG2E
                m2 = jnp.maximum(m, sk)
                f = jnp.exp2(m - m2)
                l = l * f + jnp.exp2(sk - m2)
                out = out * f
            out = out * (1.0 / l)
            if gate_branch is not None:
                gi = gate_branch * N_HEADS + hg * nh + h
                out = out * jax.nn.sigmoid(gt_ref[0, 0, pl.ds(gi, 1), :])
            outs.append(out)
        tile_t = jnp.concatenate(outs, axis=0)
        o_ref[0, :, tt * LANES:(tt + 1) * LANES] = tile_t.T.astype(o_ref.dtype)


def _flash(q, k, vt, *, nt, nkv, tq, tk, head_slot, head_half, out_dtype, head_group=None,
           window=None, nsel=None, emat=None, ck=None, cq=None, sinks=None, gates_t=None,
           gate_branch=None):
    bsz, seq, qw = q.shape
    n_hg = qw // (nt * LANES)
    nq = seq // tq
    nk = seq // tk
    nh = 2 * nt
    use_sel = nsel is not None
    use_fox = ck is not None
    use_sink = sinks is not None
    kw = 2 * LANES if (use_sel or use_fox) else LANES
    in_specs = [
        pl.BlockSpec((1, tq, nt * LANES), lambda b, g, i: (b, i, g)),
        pl.BlockSpec((1, seq, nkv * LANES), lambda b, g, i: (b, 0, g)),
        pl.BlockSpec((1, nk, nkv * LANES, tk), lambda b, g, i: (b, 0, g, 0)),
    ]
    args = [q, k, vt]
    if use_sel:
        ngrp = max(head_group) + 1
        in_specs += [
            pl.BlockSpec((1, ngrp, tq, LANES), lambda b, g, i: (b, g, i, 0)),
            pl.BlockSpec(emat.shape, lambda b, g, i: (0, 0, 0)),
        ]
        args += [nsel, emat]
    if use_fox:
        in_specs += [
            pl.BlockSpec((1, nh, seq, LANES), lambda b, g, i: (b, g, 0, 0)),
            pl.BlockSpec((1, nh, tq, LANES), lambda b, g, i: (b, g, i, 0)),
        ]
        args += [ck, cq]
    if use_sink:
        in_specs.append(pl.BlockSpec(memory_space=pltpu.SMEM))
        args.append(sinks)
    if gate_branch is not None:
        in_specs.append(pl.BlockSpec((1, 1, LANES, tq), lambda b, g, i: (b, i, 0, 0)))
        args.append(gates_t)
    kern = functools.partial(
        _flash_kernel, nt=nt, tq=tq, tk=tk, window=window, use_sel=use_sel, use_fox=use_fox,
        use_sink=use_sink, gate_branch=gate_branch, head_slot=head_slot, head_half=head_half,
        head_group=head_group)
    return pl.pallas_call(
        kern,
        grid=(bsz, n_hg, nq),
        in_specs=in_specs,
        out_specs=pl.BlockSpec((1, tq, nt * LANES), lambda b, g, i: (b, i, g)),
        out_shape=jax.ShapeDtypeStruct((bsz, seq, qw), out_dtype),
        scratch_shapes=[
            pltpu.VMEM((nh, tq, kw), BF16),
            pltpu.VMEM((nh, 1, tq), F32),
            pltpu.VMEM((nh, LANES, tq), F32),
            pltpu.VMEM((2, nh, tk, tq), F32),
            pltpu.VMEM((2, nh, tk, tq), BF16),
            pltpu.VMEM((2, nh, 1, tq), F32),
        ],
        compiler_params=_cparams(("parallel", "parallel", "arbitrary")),
        name="flash",
    )(*args)


def _banded_kernel(*refs, nt, tq, window, use_sink, gate_branch, head_half):
    it = iter(refs)
    q_ref = next(it)
    k_ref = next(it)
    vt_ref = next(it)
    if use_sink:
        sink_ref = next(it)
    if gate_branch is not None:
        gt_ref = next(it)
    o_ref = next(it)
    qal_ref = next(it)
    s_ref = next(it)

    hg = pl.program_id(1)
    i = pl.program_id(2)
    nh = 2 * nt
    span = window + tq
    q0 = i * tq
    k_start = pl.multiple_of(jnp.maximum(q0 - window, 0), tq)

    lane = lax.broadcasted_iota(jnp.int32, (1, LANES), 1)
    half = lane // HALF
    for tt in range(nt):
        qt = q_ref[0, :, tt * LANES:(tt + 1) * LANES].astype(F32)
        qr = pltpu.roll(qt, HALF, 1) if any(head_half[2 * tt + a] != a for a in range(2)) else None
        for a in range(2):
            h = 2 * tt + a
            src = qt if head_half[h] == a else qr
            qal_ref[h] = jnp.where(half == head_half[h], src, 0.0).astype(BF16)

    kspan = k_ref[0, pl.ds(k_start, span), :]
    for h in range(nh):
        s_ref[h] = _nt_dot(kspan, qal_ref[h])

    d = (q0 - k_start) + (lax.broadcasted_iota(jnp.int32, (span, tq), 1)
                          - lax.broadcasted_iota(jnp.int32, (span, tq), 0))
    mask = (d >= 0) & (d < window)
    c0 = k_start // tq
    vt = jnp.concatenate([vt_ref[0, c0 + j] for j in range(span // tq)], axis=1)
    row_half = lax.broadcasted_iota(jnp.int32, (LANES, 1), 0) // HALF
    vth = {kh: jnp.where(row_half == kh, vt, jnp.ones_like(vt)) for kh in set(head_half)}

    for tt in range(nt):
        outs = []
        for a in range(2):
            h = 2 * tt + a
            kh = head_half[h]
            s = jnp.where(mask, s_ref[h], NEG)
            m = jnp.max(s, axis=0, keepdims=True)
            p = jnp.exp2(s - m).astype(BF16)
            acc = _dot(vth[kh], p)
            l = acc[(1 - kh) * HALF:(1 - kh) * HALF + 1]
            out = acc[kh * HALF:(kh + 1) * HALF]
            if use_sink:
                sk = sink_ref[hg * nh + h] * LOG2E
                m2 = jnp.maximum(m, sk)
                f = jnp.exp2(m - m2)
                l = l * f + jnp.exp2(sk - m2)
                out = out * f
            out = out * (1.0 / l)
            if gate_branch is not None:
                gi = gate_branch * N_HEADS + hg * nh + h
                out = out * jax.nn.sigmoid(gt_ref[0, 0, pl.ds(gi, 1), :])
            outs.append(out)
        tile_t = jnp.concatenate(outs, axis=0)
        o_ref[0, :, tt * LANES:(tt + 1) * LANES] = tile_t.T.astype(o_ref.dtype)


def _banded(q, k, vt, *, nt, tq, window, head_half, out_dtype, sinks=None, gates_t=None,
            gate_branch=None):
    bsz, seq, qw = q.shape
    n_hg = qw // (nt * LANES)
    nh = 2 * nt
    use_sink = sinks is not None
    in_specs = [
        pl.BlockSpec((1, tq, nt * LANES), lambda b, g, i: (b, i, g)),
        pl.BlockSpec((1, seq, LANES), lambda b, g, i: (b, 0, g)),
        pl.BlockSpec((1, seq // tq, LANES, tq), lambda b, g, i: (b, 0, g, 0)),
    ]
    args = [q, k, vt]
    if use_sink:
        in_specs.append(pl.BlockSpec(memory_space=pltpu.SMEM))
        args.append(sinks)
    if gate_branch is not None:
        in_specs.append(pl.BlockSpec((1, 1, LANES, tq), lambda b, g, i: (b, i, 0, 0)))
        args.append(gates_t)
    kern = functools.partial(_banded_kernel, nt=nt, tq=tq, window=window, use_sink=use_sink,
                             gate_branch=gate_branch, head_half=head_half)
    return pl.pallas_call(
        kern,
        grid=(bsz, n_hg, seq // tq),
        in_specs=in_specs,
        out_specs=pl.BlockSpec((1, tq, nt * LANES), lambda b, g, i: (b, i, g)),
        out_shape=jax.ShapeDtypeStruct((bsz, seq, qw), out_dtype),
        scratch_shapes=[
            pltpu.VMEM((nh, tq, LANES), BF16),
            pltpu.VMEM((nh, window + tq, tq), F32),
        ],
        compiler_params=_cparams(("parallel", "parallel", "arbitrary")),
        name="banded",
    )(*args)


N_CHUNK16 = 128


def _compress_hidden(x_ref, pe_ref, w1_ref):
    hid = NSA_GROUPS * NSA_CMP_HIDDEN
    x = x_ref[0]
    top = _dot((x + pe_ref[0:1, :]).astype(BF16), w1_ref[:, :hid])
    bot = _dot((x + pe_ref[1:2, :]).astype(BF16), w1_ref[:, hid:])
    h1 = top + pltpu.roll(bot, N_CHUNK16 - 1, 0)
    return jax.nn.gelu(h1, approximate=True).astype(BF16)


def _compress_kernel(xk_ref, xv_ref, pek_ref, pev_ref, wk1_ref, wv1_ref, wk2_ref, wv2t_ref,
                     kc_ref, vct_ref):
    kc_ref[0] = _dot(_compress_hidden(xk_ref, pek_ref, wk1_ref), wk2_ref[...]).astype(kc_ref.dtype)
    vct_ref[0] = _nt_dot(wv2t_ref[...], _compress_hidden(xv_ref, pev_ref, wv1_ref)).astype(vct_ref.dtype)


def _compress(xk, xv, pek, pev, wk1, wv1, wk2, wv2t):
    bsz = xk.shape[0]
    width = xk.shape[2]
    gd = NSA_GROUPS * HEAD_DIM
    xspec = pl.BlockSpec((1, N_CHUNK16, width), lambda b: (b, 0, 0))
    full = lambda a: pl.BlockSpec(a.shape, lambda b: (0,) * a.ndim)
    return pl.pallas_call(
        _compress_kernel,
        grid=(bsz,),
        in_specs=[xspec, xspec, full(pek), full(pev), full(wk1), full(wv1), full(wk2), full(wv2t)],
        out_specs=[pl.BlockSpec((1, N_CHUNK16, gd), lambda b: (b, 0, 0)),
                   pl.BlockSpec((1, gd, N_CHUNK16), lambda b: (b, 0, 0))],
        out_shape=[jax.ShapeDtypeStruct((bsz, N_CHUNK16, gd), BF16),
                   jax.ShapeDtypeStruct((bsz, gd, N_CHUNK16), BF16)],
        compiler_params=_cparams(("parallel",)),
        name="compress",
    )(xk, xv, pek, pev, wk1, wv1, wk2, wv2t)


CMP_TQ = 256
N_SLC = 32


def _cmp_kernel(q_ref, kc_ref, vct_ref, gt_ref, ovt_ref, o_ref, nsel_ref, qal_ref):
    tq = CMP_TQ
    g = pl.program_id(1)
    i = pl.program_id(2)
    kv_half = jnp.zeros((1, 1), jnp.int32) + (g % 2)
    _align_queries(q_ref, qal_ref, 2, kv_half)
    kc = kc_ref[0]
    vct = vct_ref[0]
    t_row = i * tq + lax.broadcasted_iota(jnp.int32, (1, tq), 1)
    n_col = lax.broadcasted_iota(jnp.int32, (N_CHUNK16, 1), 0)
    n_cmp = (N_CHUNK16 * NSA_CMP_STRIDE - NSA_CMP_LEN) // NSA_CMP_STRIDE + 1
    valid = (n_col * NSA_CMP_STRIDE + (NSA_CMP_LEN - 1) <= t_row) & (n_col < n_cmp)

    scores = [_nt_dot(kc, qal_ref[h]) for h in range(4)]
    probs = []
    for h in range(4):
        s = jnp.where(valid, scores[h], NEG)
        m = jnp.max(s, axis=0, keepdims=True)
        e = jnp.where(valid, jnp.exp2(s - m), 0.0)
        l = jnp.sum(e, axis=0, keepdims=True)
        probs.append(e * (1.0 / jnp.where(l > 0.0, l, 1.0)))
    psum = (probs[0] + probs[1]) + (probs[2] + probs[3])
    outs = [_pick_half(_dot(vct, p.astype(BF16)), kv_half) for p in probs]
    for tt in range(2):
        gated = [outs[2 * tt + a] * jax.nn.sigmoid(gt_ref[0, 0, pl.ds(g * 4 + 2 * tt + a, 1), :])
                 for a in range(2)]
        o_ref[0, :, tt * LANES:(tt + 1) * LANES] = jnp.concatenate(gated, axis=0).T.astype(o_ref.dtype)

    p_hi = psum.astype(BF16)
    p_lo = (psum - p_hi.astype(F32)).astype(BF16)
    ovt = ovt_ref[...]
    imp = _dot(ovt, p_hi) + _dot(ovt, p_lo)
    j = lax.broadcasted_iota(jnp.int32, (N_SLC, 1), 0)
    tb = jnp.right_shift(t_row, 6)
    forced = (j == 0) | (j == tb) | (j == tb - 1)
    imp = jnp.where(j > tb, NEG, jnp.where(forced, NSA_FORCE_BONUS, imp))
    cnt = jnp.zeros((N_SLC, tq), jnp.int32)
    for jp in range(N_SLC):
        row = imp[jp:jp + 1, :]
        beats = (row > imp) | ((row == imp) & (jp < j))
        cnt = cnt + beats.astype(jnp.int32)
    sel_t = (cnt < NSA_TOPK).astype(F32)
    sel_t = jnp.concatenate([sel_t, jnp.ones((LANES - N_SLC, tq), F32)], axis=0)
    nsel_ref[0, 0] = ((sel_t.T - 1.0) * SEL_BIG).astype(nsel_ref.dtype)


def _cmp_attn(q, kc, vct, gates_t, ovt):
    bsz, seq, _ = q.shape
    tq = CMP_TQ
    gw = 2 * LANES
    return pl.pallas_call(
        _cmp_kernel,
        grid=(bsz, NSA_GROUPS, seq // tq),
        in_specs=[
            pl.BlockSpec((1, tq, gw), lambda b, g, i: (b, i, g)),
            pl.BlockSpec((1, N_CHUNK16, LANES), lambda b, g, i: (b, 0, g // 2)),
            pl.BlockSpec((1, LANES, N_CHUNK16), lambda b, g, i: (b, g // 2, 0)),
            pl.BlockSpec((1, 1, LANES, tq), lambda b, g, i: (b, i, 0, 0)),
            pl.BlockSpec(ovt.shape, lambda b, g, i: (0, 0)),
        ],
        out_specs=[
            pl.BlockSpec((1, tq, gw), lambda b, g, i: (b, i, g)),
            pl.BlockSpec((1, 1, tq, LANES), lambda b, g, i: (b, g, i, 0)),
        ],
        out_shape=[
            jax.ShapeDtypeStruct((bsz, seq, D_MODEL), BF16),
            jax.ShapeDtypeStruct((bsz, NSA_GROUPS, seq, LANES), BF16),
        ],
        scratch_shapes=[pltpu.VMEM((4, tq, LANES), BF16)],
        compiler_params=_cparams(("parallel", "parallel", "arbitrary")),
        name="cmp_attn",
    )(q, kc, vct, gates_t, ovt)


CS_BLK = 256


def _cumsum_kernel(z_ref, b_ref, c_ref):
    seq, width = z_ref.shape
    tri = (lax.broadcasted_iota(jnp.int32, (CS_BLK, CS_BLK), 0)
           >= lax.broadcasted_iota(jnp.int32, (CS_BLK, CS_BLK), 1)).astype(BF16)
    carry = jnp.zeros((1, width), F32)
    for blk in range(seq // CS_BLK):
        x = jax.nn.log_sigmoid(z_ref[blk * CS_BLK:(blk + 1) * CS_BLK, :] + b_ref[...])
        hi = x.astype(BF16)
        r1 = x - hi.astype(F32)
        mid = r1.astype(BF16)
        lo = (r1 - mid.astype(F32)).astype(BF16)
        cs = _dot(tri, hi) + _dot(tri, mid) + _dot(tri, lo) + carry
        c_ref[blk * CS_BLK:(blk + 1) * CS_BLK, :] = cs * LOG2E
        carry = cs[CS_BLK - 1:CS_BLK, :]


def _cumsum(z, bias):
    seq, width = z.shape
    return pl.pallas_call(
        _cumsum_kernel,
        grid=(width // LANES,),
        in_specs=[pl.BlockSpec((seq, LANES), lambda j: (0, j)),
                  pl.BlockSpec((1, LANES), lambda j: (0, j))],
        out_specs=pl.BlockSpec((seq, LANES), lambda j: (0, j)),
        out_shape=jax.ShapeDtypeStruct((seq, width), F32),
        compiler_params=_cparams(("parallel",)),
        name="cumsum",
    )(z, bias)


def _fox_bias_kernel(c_ref, ck_ref, cq_ref):
    li = pl.program_id(0) % LANES
    c = c_ref[...]
    lane = lax.broadcasted_iota(jnp.int32, c.shape, 1)
    col = jnp.sum(jnp.where(lane == li, c, 0.0), axis=1, keepdims=True)
    hi = col.astype(BF16).astype(F32)
    mid = (col - hi).astype(BF16).astype(F32)
    lo = col - hi - mid
    k_piece = jnp.where(lane == 0, hi, jnp.where(lane == 1, mid, lo))
    q_piece = jnp.where(lane == 3, hi, jnp.where(lane == 4, mid, lo))
    ck_ref[0] = jnp.where(lane < 3, k_piece, jnp.where(lane < 6, 1.0, 0.0)).astype(BF16)
    cq_ref[0] = jnp.where(lane < 3, -1.0, jnp.where(lane < 6, q_piece, 0.0)).astype(BF16)


def _fox_bias(c, n):
    seq = c.shape[0]
    ospec = pl.BlockSpec((1, seq, LANES), lambda j: (j, 0, 0))
    oshape = jax.ShapeDtypeStruct((n, seq, LANES), BF16)
    return pl.pallas_call(
        _fox_bias_kernel,
        grid=(n,),
        in_specs=[pl.BlockSpec((seq, LANES), lambda j: (0, j // LANES))],
        out_specs=[ospec, ospec],
        out_shape=[oshape, oshape],
        compiler_params=_cparams(("parallel",)),
        name="fox_bias",
    )(c)


def _rope_tables(seq):
    inv = ROPE_THETA ** (-jnp.arange(0, HEAD_DIM, 2, dtype=F32) / HEAD_DIM)
    ang = jnp.arange(seq, dtype=F32)[:, None] * inv[None, :]
    cos, sin = jnp.cos(ang), jnp.sin(ang)
    cos_t = jnp.tile(cos, (1, LANES // (HEAD_DIM // 2)))
    sin_t = jnp.tile(jnp.concatenate([-sin, sin], axis=1), (1, LANES // HEAD_DIM))
    return cos_t, sin_t


def _overlap_t(seq):
    n_cmp = (seq - NSA_CMP_LEN) // NSA_CMP_STRIDE + 1
    n_slc = seq // NSA_SLC_LEN
    cs = np.arange(n_cmp) * NSA_CMP_STRIDE
    ce = cs + NSA_CMP_LEN
    ss = np.arange(n_slc) * NSA_SLC_LEN
    se = ss + NSA_SLC_LEN
    ov = np.clip(np.minimum(ce[:, None], se[None, :]) - np.maximum(cs[:, None], ss[None, :]), 0, None)
    ov = (ov / NSA_CMP_LEN).astype(np.float32)
    ovt = np.zeros((n_slc, LANES), np.float32)
    ovt[:, :n_cmp] = ov.T
    return jnp.asarray(ovt, dtype=BF16)


def _expand_mat_t(seq, tk):
    key = np.arange(seq)
    e = ((key // NSA_SLC_LEN)[:, None] == np.arange(LANES)[None, :]).astype(np.float32)
    return jnp.asarray(e.reshape(seq // tk, tk, LANES), dtype=BF16)


def _compress_weights(pe, w1, w2):
    g = NSA_GROUPS
    half = NSA_CMP_LEN // 2
    eye = jnp.eye(g, dtype=F32)
    w1r = w1.reshape(NSA_CMP_LEN, HEAD_DIM, NSA_CMP_HIDDEN)

    def big(part):
        return jnp.einsum('ldj,gh->lgdhj', part, eye).reshape(half * g * HEAD_DIM, g * NSA_CMP_HIDDEN)

    w1big = jnp.concatenate([big(w1r[:half]), big(w1r[half:])], axis=1).astype(BF16)
    w2big = jnp.einsum('jd,gh->gjhd', w2, eye).reshape(g * NSA_CMP_HIDDEN, g * HEAD_DIM).astype(BF16)

    def pebig(part):
        return jnp.broadcast_to(part[:, None, :], (half, g, HEAD_DIM)).reshape(-1)

    pe2 = jnp.stack([pebig(pe[:half]), pebig(pe[half:])], axis=0)
    pe2 = jnp.concatenate([pe2, jnp.zeros((6, pe2.shape[1]), F32)], axis=0)
    return pe2, w1big, w2big


def _pad_cols(w, width):
    return jnp.concatenate([w, jnp.zeros((w.shape[0], width - w.shape[1]), w.dtype)], axis=1)


ATT_T = 256


def _nsa_mixer(x2, bsz, seq, gain, w_in, ck_pe, ck_w1, ck_w2, cv_pe, cv_w1, cv_w2,
               rope_tabs, ovt, emat_t):
    qw = N_HEADS * HEAD_DIM
    gd = NSA_GROUPS * HEAD_DIM

    def kvcols(c, s):
        lo = qw + (c * 2 + s) * gd
        return w_in[:, lo:lo + gd]

    w = jnp.concatenate([w_in[:, :qw]] + [kvcols(c, 0) for c in range(3)] + [kvcols(0, 1)],
                        axis=1).astype(BF16)
    wt = jnp.concatenate([kvcols(1, 1), kvcols(2, 1), _pad_cols(w_in[:, qw + 6 * gd:], LANES)],
                         axis=1).T.astype(BF16)
    scale = HEAD_DIM ** -0.5 * LOG2E
    row_groups = ((0, qw, True, scale), (qw, gd, True, 1.0), (qw + gd, gd, True, 1.0),
                  (qw + 2 * gd, gd, True, 1.0), (qw + 3 * gd, gd, False, 1.0))
    t_groups = ((0, gd, ATT_T), (gd, gd, ATT_T), (2 * gd, LANES, ATT_T))
    q, k0, k1, k2, v0, v1t, v2t, gates_t = _proj(
        x2, gain, w, row_groups, (BF16, F32, BF16, BF16, F32), seq, wt=wt, t_groups=t_groups,
        t_dtypes=(BF16, BF16, F32), rope_tabs=rope_tabs)
    r3 = lambda a: a.reshape(bsz, seq, a.shape[-1])
    q, k1, k2 = map(r3, (q, k1, k2))

    chunkw = NSA_CMP_STRIDE * gd
    xk = k0.reshape(bsz, seq // NSA_CMP_STRIDE, chunkw)
    xv = v0.reshape(bsz, seq // NSA_CMP_STRIDE, chunkw)
    pek, wk1, wk2 = _compress_weights(ck_pe, ck_w1, ck_w2)
    pev, wv1, wv2 = _compress_weights(cv_pe, cv_w1, cv_w2)
    kc, vct = _compress(xk, xv, pek, pev, wk1, wv1, wk2, wv2.T)

    o_cmp, nsel = _cmp_attn(q, kc, vct, gates_t, ovt)
    grp = tuple(h // 4 for h in range(8))
    common = dict(nt=4, nkv=1, tq=ATT_T, tk=ATT_T, head_slot=(0,) * 8, head_half=grp,
                  out_dtype=BF16, gates_t=gates_t)
    o_slc = _flash(q, k1, v1t, head_group=grp, nsel=nsel, emat=emat_t, gate_branch=1, **common)
    o_win = _banded(q, k2, v2t, nt=4, tq=ATT_T, window=NSA_WINDOW, head_half=grp, out_dtype=BF16,
                    gates_t=gates_t, gate_branch=2)
    flat = lambda a: a.reshape(bsz * seq, D_MODEL)
    return [flat(o_cmp), flat(o_slc), flat(o_win)]


def _swa_mixer(x2, bsz, seq, gain, w_in, sinks, rope_tabs):
    qw = N_HEADS * HEAD_DIM
    scale = HEAD_DIM ** -0.5 * LOG2E
    row_groups = ((0, qw, True, scale), (qw, LANES, True, 1.0))
    w = w_in[:, :qw + LANES].astype(BF16)
    wt = w_in[:, qw + LANES:].T.astype(BF16)
    q, k, vt = _proj(x2, gain, w, row_groups, (BF16, BF16), seq, wt=wt,
                     t_groups=((0, LANES, SWA_WINDOW),), t_dtypes=(BF16,), rope_tabs=rope_tabs)
    r3 = lambda a: a.reshape(bsz, seq, a.shape[-1])
    o = _banded(r3(q), r3(k), vt, nt=8, tq=SWA_WINDOW, window=SWA_WINDOW,
                head_half=tuple(h // 8 for h in range(16)), out_dtype=BF16, sinks=sinks)
    return [o.reshape(bsz * seq, D_MODEL)]


def _fox_mixer(x2, bsz, seq, gain, w_in, b_f):
    qw = N_HEADS * HEAD_DIM
    scale = HEAD_DIM ** -0.5 * LOG2E
    w = jnp.concatenate([w_in[:, :2 * qw], _pad_cols(w_in[:, 3 * qw:], LANES)], axis=1).astype(BF16)
    wt = w_in[:, 2 * qw:3 * qw].T.astype(BF16)
    row_groups = ((0, qw, False, scale), (qw, qw, False, 1.0), (2 * qw, LANES, False, 1.0))
    q, k, f, vt = _proj(x2, gain, w, row_groups, (BF16, BF16, F32), seq, wt=wt,
                        t_groups=((0, qw, ATT_T),), t_dtypes=(BF16,))
    r3 = lambda a: a.reshape(bsz, seq, a.shape[-1])

    bh = bsz * N_HEADS
    bhp = -(-bh // LANES) * LANES
    z = f.reshape(bsz, seq, LANES)[:, :, :N_HEADS].transpose(1, 0, 2).reshape(seq, bh)
    z = _pad_cols(z, bhp)
    bias = _pad_cols(jnp.tile(b_f, bsz).reshape(1, bh), bhp)
    c = _cumsum(z, bias)
    ck, cq = _fox_bias(c, bh)
    r4 = lambda a: a.reshape(bsz, N_HEADS, seq, LANES)
    o = _flash(r3(q), r3(k), vt, nt=4, nkv=4, tq=ATT_T, tk=ATT_T,
               head_slot=tuple(h // 2 for h in range(8)), head_half=(0, 1) * 4, out_dtype=BF16,
               ck=r4(ck), cq=r4(cq))
    return [o.reshape(bsz * seq, D_MODEL)]


def kernel(x, ffn1_norm, ffn1_w_gu, ffn1_w_down, mix_norm, ffn2_norm, ffn2_w_gu, ffn2_w_down,
           nsa_w_in, nsa_ck_pe, nsa_ck_w1, nsa_ck_w2, nsa_cv_pe, nsa_cv_w1, nsa_cv_w2, nsa_w_out,
           swa_w_in, swa_sinks, swa_w_out, fox_w_in, fox_b_f, fox_w_out, final_norm):
    bsz, seq, _ = x.shape
    depth = ffn1_norm.shape[0]
    rope_tabs = _rope_tables(seq)
    ovt = _overlap_t(seq)
    emat_t = _expand_mat_t(seq, ATT_T)
    x2 = x.reshape(bsz * seq, D_MODEL)
    for i in range(depth):
        kind, j = i % N_MIXERS, i // N_MIXERS
        x2 = _ffn(x2, ffn1_norm[i], ffn1_w_gu[i], ffn1_w_down[i])
        if kind == 0:
            branches = _nsa_mixer(x2, bsz, seq, mix_norm[i], nsa_w_in[j], nsa_ck_pe[j],
                                  nsa_ck_w1[j], nsa_ck_w2[j], nsa_cv_pe[j], nsa_cv_w1[j],
                                  nsa_cv_w2[j], rope_tabs, ovt, emat_t)
            w_out = nsa_w_out[j]
        elif kind == 1:
            branches = _swa_mixer(x2, bsz, seq, mix_norm[i], swa_w_in[j], swa_sinks[j], rope_tabs)
            w_out = swa_w_out[j]
        else:
            branches = _fox_mixer(x2, bsz, seq, mix_norm[i], fox_w_in[j], fox_b_f[j])
            w_out = fox_w_out[j]
        x2 = _ffn(x2, ffn2_norm[i], ffn2_w_gu[i], ffn2_w_down[i], branches=branches, w_out=w_out,
                  final_gain=final_norm if i == depth - 1 else None)
    return x2.reshape(bsz, seq, D_MODEL)
```

```python
import functools

import numpy as np
import jax
import jax.numpy as jnp
from jax import lax
from jax.experimental import pallas as pl
from jax.experimental.pallas import tpu as pltpu

D_MODEL = 1024
HEAD_DIM = 64
N_HEADS = 16
D_FF = 2816
RMS_EPS = 1e-6
ROPE_THETA = 10000.0
NEG = -1e30
SEL_BIG = float(2 ** 100)
LOG2E = 1.4426950408889634

NSA_GROUPS = 4
NSA_CMP_LEN = 32
NSA_CMP_STRIDE = 16
NSA_CMP_HIDDEN = 128
NSA_SLC_LEN = 64
NSA_TOPK = 16
NSA_WINDOW = 512
NSA_FORCE_BONUS = 1e4
SWA_WINDOW = 128
N_MIXERS = 3

LANES = 128
HALF = 64
VMEM_LIMIT = 56 * 1024 * 1024

F32 = jnp.float32
BF16 = jnp.bfloat16


def _nt_dot(a, b):
    return lax.dot_general(a, b, (((1,), (1,)), ((), ())), preferred_element_type=F32)


def _dot(a, b):
    return jnp.dot(a, b, preferred_element_type=F32)


def _rms(x, g):
    ms = jnp.mean(x * x, axis=-1, keepdims=True)
    return x * lax.rsqrt(ms + RMS_EPS) * g


def _cparams(sem):
    return pltpu.CompilerParams(dimension_semantics=sem, vmem_limit_bytes=VMEM_LIMIT)


FFN_TM = 512
FFN_TF = 256
FFN_NC = 16


def _ffn_kernel(*refs, n_in, final):
    it = iter(refs)
    x_ref = next(it)
    o_refs = [next(it) for _ in range(n_in)]
    g_ref = next(it)
    wgu_ref = next(it)
    wd_ref = next(it)
    wo_ref = next(it) if n_in else None
    fg_ref = next(it) if final else None
    out_ref = next(it)
    wgu_s = next(it)
    wd_s = next(it)
    wo_s = next(it) if n_in else None

    j = pl.program_id(0)

    @pl.when(j < FFN_NC)
    def _():
        for src, dst in ((wgu_ref, wgu_s), (wd_ref, wd_s)) + (((wo_ref, wo_s),) if n_in else ()):
            rows = src.shape[0]
            dst[pl.ds(pl.multiple_of(j * rows, rows), rows), :] = src[...].astype(BF16)

    @pl.when(j >= FFN_NC)
    def _():
        x = x_ref[...]
        if n_in:
            o = o_refs[0][...].astype(F32)
            for r in o_refs[1:]:
                o = o + r[...].astype(F32)
            x = x + _dot(o.astype(BF16), wo_s[...])
        hb = _rms(x, g_ref[...]).astype(BF16)
        acc = jnp.zeros(x.shape, F32)
        for f in range(D_FF // FFN_TF):
            lo = f * FFN_TF
            g = _dot(hb, wgu_s[:, lo:lo + FFN_TF])
            u = _dot(hb, wgu_s[:, D_FF + lo:D_FF + lo + FFN_TF])
            a = (g * jax.nn.sigmoid(g)) * u
            acc = acc + _dot(a.astype(BF16), wd_s[lo:lo + FFN_TF, :])
        y = x + 0.5 * acc
        if final:
            y = _rms(y, fg_ref[...])
        out_ref[...] = y


def _ffn(x2, gain, wgu, wd, layer, branches=(), w_out=None, w_out_layer=0, final_gain=None):
    n = x2.shape[0]
    n_in = len(branches)
    final = final_gain is not None
    slab = lambda lyr: (lambda j: (lyr, jnp.minimum(j, FFN_NC - 1), 0))
    row = pl.BlockSpec((FFN_TM, D_MODEL), lambda j: (jnp.maximum(j - FFN_NC, 0), 0))
    vec = pl.BlockSpec((1, D_MODEL), lambda j: (0, 0))
    in_specs = [row] + [row] * n_in + [
        vec,
        pl.BlockSpec((None, D_MODEL // FFN_NC, 2 * D_FF), slab(layer)),
        pl.BlockSpec((None, D_FF // FFN_NC, D_MODEL), slab(layer)),
    ]
    args = [x2, *branches, gain.reshape(1, D_MODEL), wgu, wd]
    scratch = [pltpu.VMEM((D_MODEL, 2 * D_FF), BF16), pltpu.VMEM((D_FF, D_MODEL), BF16)]
    if n_in:
        in_specs.append(pl.BlockSpec((None, D_MODEL // FFN_NC, D_MODEL), slab(w_out_layer)))
        args.append(w_out)
        scratch.append(pltpu.VMEM((D_MODEL, D_MODEL), BF16))
    if final:
        in_specs.append(vec)
        args.append(final_gain.reshape(1, D_MODEL))
    return pl.pallas_call(
        functools.partial(_ffn_kernel, n_in=n_in, final=final),
        grid=(FFN_NC + n // FFN_TM,),
        in_specs=in_specs,
        out_specs=row,
        out_shape=jax.ShapeDtypeStruct((n, D_MODEL), F32),
        scratch_shapes=scratch,
        compiler_params=_cparams(("arbitrary",)),
        name="ffn",
    )(*args)


PROJ_TM = 512
PROJ_CH = 256


def _rope_tile(y, cos_t, sin_t, first_half):
    rot = jnp.where(first_half, pltpu.roll(y, 96, 1), pltpu.roll(y, 32, 1))
    return y * cos_t + rot * sin_t


def _proj_kernel(*refs, row_groups, t_groups, use_rope):
    it = iter(refs)
    x_ref = next(it)
    g_ref = next(it)
    w_ref = next(it)
    wt_ref = next(it) if t_groups else None
    if use_rope:
        cos_t = next(it)[...]
        sin_t = next(it)[...]
        lane = lax.broadcasted_iota(jnp.int32, (1, LANES), 1)
        first_half = (lane % HALF) < (HALF // 2)
    o_refs = list(it)
    hb = _rms(x_ref[...], g_ref[...]).astype(BF16)
    tm = hb.shape[0]
    for (c0, width, rope, scale, planar), o_ref in zip(row_groups, o_refs):
        for t0 in range(0, width, PROJ_CH):
            ch = min(PROJ_CH, width - t0)
            y = _dot(hb, w_ref[:, c0 + t0:c0 + t0 + ch])
            for l0 in range(0, ch, LANES):
                yt = y[:, l0:l0 + LANES]
                if rope:
                    yt = _rope_tile(yt, cos_t, sin_t, first_half)
                if scale != 1.0:
                    yt = yt * scale
                if planar:
                    o_ref[(t0 + l0) // LANES] = yt.astype(o_ref.dtype)
                else:
                    o_ref[:, t0 + l0:t0 + l0 + LANES] = yt.astype(o_ref.dtype)
    for (c0, width, ck), o_ref in zip(t_groups, o_refs[len(row_groups):]):
        for t0 in range(0, width, PROJ_CH):
            ch = min(PROJ_CH, width - t0)
            yt = _nt_dot(wt_ref[c0 + t0:c0 + t0 + ch, :], hb)
            for s0 in range(0, tm, ck):
                o_ref[0, s0 // ck, t0:t0 + ch, :] = yt[:, s0:s0 + ck].astype(o_ref.dtype)


def _proj(x2, gain, w, row_groups, row_dtypes, seq, wt=None, t_groups=(), t_dtypes=(),
          rope_tabs=None):
    n = x2.shape[0]
    bsz = n // seq
    nblk = seq // PROJ_TM
    use_rope = rope_tabs is not None
    resident = dict(pipeline_mode=pl.Buffered(1))
    in_specs = [
        pl.BlockSpec((PROJ_TM, D_MODEL), lambda i: (i, 0)),
        pl.BlockSpec((1, D_MODEL), lambda i: (0, 0)),
        pl.BlockSpec(w.shape, lambda i: (0, 0), **resident),
    ]
    args = [x2, gain.reshape(1, D_MODEL), w]
    if t_groups:
        in_specs.append(pl.BlockSpec(wt.shape, lambda i: (0, 0), **resident))
        args.append(wt)
    if use_rope:
        in_specs += [pl.BlockSpec((PROJ_TM, LANES), lambda i: (i % nblk, 0))] * 2
        args += list(rope_tabs)
    out_specs, out_shape = [], []
    for (_, width, _, _, planar), dt in zip(row_groups, row_dtypes):
        if planar:
            out_specs.append(pl.BlockSpec((width // LANES, PROJ_TM, LANES), lambda i: (0, i, 0)))
            out_shape.append(jax.ShapeDtypeStruct((width // LANES, n, LANES), dt))
        else:
            out_specs.append(pl.BlockSpec((PROJ_TM, width), lambda i: (i, 0)))
            out_shape.append(jax.ShapeDtypeStruct((n, width), dt))
    for (_, width, ck), dt in zip(t_groups, t_dtypes):
        out_specs.append(pl.BlockSpec((1, PROJ_TM // ck, width, ck),
                                      lambda i: (i // nblk, i % nblk, 0, 0)))
        out_shape.append(jax.ShapeDtypeStruct((bsz, seq // ck, width, ck), dt))
    return pl.pallas_call(
        functools.partial(_proj_kernel, row_groups=row_groups, t_groups=t_groups,
                          use_rope=use_rope),
        grid=(n // PROJ_TM,),
        in_specs=in_specs,
        out_specs=out_specs,
        out_shape=out_shape,
        compiler_params=_cparams(("parallel",)),
        name="proj",
    )(*args)


def _flash_kernel(*refs, nt, tq, tk, window, use_sel, use_fox, use_sink, gate_branch,
                  head_slot, head_half, head_group):
    it = iter(refs)
    q_ref = next(it)
    k_ref = next(it)
    vt_ref = next(it)
    if use_sel:
        nsel_ref = next(it)
        e_ref = next(it)
    if use_fox:
        ck_ref = next(it)
        cq_ref = next(it)
    if use_sink:
        sink_ref = next(it)
    if gate_branch is not None:
        gt_ref = next(it)
    o_ref = next(it)
    qal_ref = next(it)
    m_ref = next(it)
    acc_ref = next(it)
    s_ref = next(it)
    p_ref = next(it)
    a_ref = next(it)

    hg = pl.program_id(1)
    i = pl.program_id(2)
    nh = 2 * nt

    lane = lax.broadcasted_iota(jnp.int32, (1, LANES), 1)
    half = lane // HALF
    for tt in range(nt):
        qt = q_ref[0, :, tt * LANES:(tt + 1) * LANES].astype(F32)
        qr = pltpu.roll(qt, HALF, 1) if any(head_half[2 * tt + a] != a for a in range(2)) else None
        for a in range(2):
            h = 2 * tt + a
            src = qt if head_half[h] == a else qr
            qal_ref[h, :, :LANES] = jnp.where(half == head_half[h], src, 0.0).astype(BF16)
            if use_sel:
                qal_ref[h, :, LANES:] = nsel_ref[0, head_group[h]]
            if use_fox:
                qal_ref[h, :, LANES:] = cq_ref[0, h]

    m_ref[...] = jnp.full(m_ref.shape, NEG, F32)
    acc_ref[...] = jnp.zeros(acc_ref.shape, F32)

    row_half = lax.broadcasted_iota(jnp.int32, (LANES, 1), 0) // HALF
    q0 = i * tq
    colrow = (lax.broadcasted_iota(jnp.int32, (tk, tq), 1)
              - lax.broadcasted_iota(jnp.int32, (tk, tq), 0))

    if window is None:
        c_lo = 0
    else:
        c_lo = jnp.maximum(q0 - (window - 1), 0) // tk
    c_hi = (q0 + tq + tk - 1) // tk

    def scores(c, h):
        k0 = pl.multiple_of(c * tk, tk)
        sl = head_slot[h]
        kc = k_ref[0, pl.ds(k0, tk), sl * LANES:(sl + 1) * LANES]
        if use_sel:
            kc = jnp.concatenate([kc, e_ref[c]], axis=1)
        if use_fox:
            kc = jnp.concatenate([kc, ck_ref[0, h, pl.ds(k0, tk), :]], axis=1)
        return _nt_dot(kc, qal_ref[h])

    def values_t(c):
        made = {}
        for h in range(nh):
            key = (head_slot[h], head_half[h])
            if key not in made:
                vt = vt_ref[0, c, key[0] * LANES:(key[0] + 1) * LANES, :]
                made[key] = jnp.where(row_half == key[1], vt, jnp.ones_like(vt))
        return [made[(head_slot[h], head_half[h])] for h in range(nh)]

    def accumulate(vth, par, h):
        acc_ref[h] = a_ref[par, h] * acc_ref[h] + _dot(vth[h], p_ref[par, h])

    for h in range(nh):
        s_ref[0, h] = scores(c_lo, h)
    p_ref[1] = jnp.zeros(p_ref.shape[1:], BF16)
    a_ref[...] = jnp.ones(a_ref.shape, F32)

    def trip(c, par, masked):
        k0 = pl.multiple_of(c * tk, tk)
        if masked:
            d = colrow + (q0 - k0)
            mask = d >= 0
            if window is not None:
                mask = mask & (d < window)
        c_next = jnp.minimum(c + 1, c_hi - 1)
        vth = values_t(jnp.maximum(c - 1, c_lo))
        for h in range(nh):
            accumulate(vth, 1 - par, h)
            s_ref[1 - par, h] = scores(c_next, h)
            s = s_ref[par, h]
            if masked:
                s = jnp.where(mask, s, NEG)
            m_old = m_ref[h]
            m_new = jnp.maximum(m_old, jnp.max(s, axis=0, keepdims=True))
            a_ref[par, h] = jnp.exp2(m_old - m_new)
            p_ref[par, h] = jnp.exp2(s - m_new).astype(BF16)
            m_ref[h] = m_new

    def chunk(c, carry):
        odd = (c - c_lo) % 2
        k0 = c * tk
        edge = k0 + (tk - 1) > q0
        if window is not None:
            edge = edge | (k0 + window <= q0 + (tq - 1))
        for par in range(2):
            pl.when((odd == par) & edge)(functools.partial(trip, c, par, True))
            if window is None or window > tk:
                pl.when((odd == par) & jnp.logical_not(edge))(functools.partial(trip, c, par, False))
        return carry

    lax.fori_loop(c_lo, c_hi, chunk, 0)
    last_odd = (c_hi - 1 - c_lo) % 2

    def drain(par):
        vth = values_t(c_hi - 1)
        for h in range(nh):
            accumulate(vth, par, h)

    pl.when(last_odd == 0)(functools.partial(drain, 0))
    pl.when(last_odd == 1)(functools.partial(drain, 1))

    for tt in range(nt):
        outs = []
        for a in range(2):
            h = 2 * tt + a
            acc = acc_ref[h]
            kh = head_half[h]
            l = acc[(1 - kh) * HALF:(1 - kh) * HALF + 1]
            out = acc[kh * HALF:(kh + 1) * HALF]
            if use_sink:
                m = m_ref[h]
                sk = sink_ref[hg * nh + h] * LOG2E
                m2 = jnp.maximum(m, sk)
                f = jnp.exp2(m - m2)
                l = l * f + jnp.exp2(sk - m2)
                out = out * f
            out = out * (1.0 / l)
            if gate_branch is not None:
                gi = gate_branch * N_HEADS + hg * nh + h
                out = out * jax.nn.sigmoid(gt_ref[0, 0, pl.ds(gi, 1), :])
            outs.append(out)
        tile_t = jnp.concatenate(outs, axis=0)
        o_ref[0, :, tt * LANES:(tt + 1) * LANES] = tile_t.T.astype(o_ref.dtype)


def _flash(q, k, vt, *, nt, nkv, tq, tk, head_slot, head_half, out_dtype, head_group=None,
           window=None, nsel=None, emat=None, ck=None, cq=None, sinks=None, gates_t=None,
           gate_branch=None):
    bsz, seq, qw = q.shape
    n_hg = qw // (nt * LANES)
    nq = seq // tq
    nk = seq // tk
    nh = 2 * nt
    use_sel = nsel is not None
    use_fox = ck is not None
    use_sink = sinks is not None
    kw = 2 * LANES if (use_sel or use_fox) else LANES
    in_specs = [
        pl.BlockSpec((1, tq, nt * LANES), lambda b, g, i: (b, i, g)),
        pl.BlockSpec((1, seq, nkv * LANES), lambda b, g, i: (b, 0, g)),
        pl.BlockSpec((1, nk, nkv * LANES, tk), lambda b, g, i: (b, 0, g, 0)),
    ]
    args = [q, k, vt]
    if use_sel:
        ngrp = max(head_group) + 1
        in_specs += [
            pl.BlockSpec((1, ngrp, tq, LANES), lambda b, g, i: (b, g, i, 0)),
            pl.BlockSpec(emat.shape, lambda b, g, i: (0, 0, 0)),
        ]
        args += [nsel, emat]
    if use_fox:
        in_specs += [
            pl.BlockSpec((1, nh, seq, LANES), lambda b, g, i: (b, g, 0, 0)),
            pl.BlockSpec((1, nh, tq, LANES), lambda b, g, i: (b, g, i, 0)),
        ]
        args += [ck, cq]
    if use_sink:
        in_specs.append(pl.BlockSpec(memory_space=pltpu.SMEM))
        args.append(sinks)
    if gate_branch is not None:
        in_specs.append(pl.BlockSpec((1, 1, LANES, tq), lambda b, g, i: (b, i, 0, 0)))
        args.append(gates_t)
    kern = functools.partial(
        _flash_kernel, nt=nt, tq=tq, tk=tk, window=window, use_sel=use_sel, use_fox=use_fox,
        use_sink=use_sink, gate_branch=gate_branch, head_slot=head_slot, head_half=head_half,
        head_group=head_group)
    return pl.pallas_call(
        kern,
        grid=(bsz, n_hg, nq),
        in_specs=in_specs,
        out_specs=pl.BlockSpec((1, tq, nt * LANES), lambda b, g, i: (b, i, g)),
        out_shape=jax.ShapeDtypeStruct((bsz, seq, qw), out_dtype),
        scratch_shapes=[
            pltpu.VMEM((nh, tq, kw), BF16),
            pltpu.VMEM((nh, 1, tq), F32),
            pltpu.VMEM((nh, LANES, tq), F32),
            pltpu.VMEM((2, nh, tk, tq), F32),
            pltpu.VMEM((2, nh, tk, tq), BF16),
            pltpu.VMEM((2, nh, 1, tq), F32),
        ],
        compiler_params=_cparams(("parallel", "parallel", "arbitrary")),
        name="flash",
    )(*args)


def _banded_kernel(*refs, nt, tq, window, use_sink, gate_branch, head_half):
    it = iter(refs)
    q_ref = next(it)
    k_ref = next(it)
    vt_ref = next(it)
    if use_sink:
        sink_ref = next(it)
    if gate_branch is not None:
        gt_ref = next(it)
    o_ref = next(it)
    qal_ref = next(it)
    s_ref = next(it)

    hg = pl.program_id(1)
    i = pl.program_id(2)
    nh = 2 * nt
    span = window + tq
    q0 = i * tq
    k_start = pl.multiple_of(jnp.maximum(q0 - window, 0), tq)

    lane = lax.broadcasted_iota(jnp.int32, (1, LANES), 1)
    half = lane // HALF
    for tt in range(nt):
        qt = q_ref[0, :, tt * LANES:(tt + 1) * LANES].astype(F32)
        qr = pltpu.roll(qt, HALF, 1) if any(head_half[2 * tt + a] != a for a in range(2)) else None
        for a in range(2):
            h = 2 * tt + a
            src = qt if head_half[h] == a else qr
            qal_ref[h] = jnp.where(half == head_half[h], src, 0.0).astype(BF16)

    kspan = k_ref[0, pl.ds(k_start, span), :]
    for h in range(nh):
        s_ref[h] = _nt_dot(kspan, qal_ref[h])

    d = (q0 - k_start) + (lax.broadcasted_iota(jnp.int32, (span, tq), 1)
                          - lax.broadcasted_iota(jnp.int32, (span, tq), 0))
    mask = (d >= 0) & (d < window)
    c0 = k_start // tq
    vt = jnp.concatenate([vt_ref[0, c0 + j] for j in range(span // tq)], axis=1)
    row_half = lax.broadcasted_iota(jnp.int32, (LANES, 1), 0) // HALF
    vth = {kh: jnp.where(row_half == kh, vt, jnp.ones_like(vt)) for kh in set(head_half)}

    for tt in range(nt):
        outs = []
        for a in range(2):
            h = 2 * tt + a
            kh = head_half[h]
            s = jnp.where(mask, s_ref[h], NEG)
            m = jnp.max(s, axis=0, keepdims=True)
            p = jnp.exp2(s - m).astype(BF16)
            acc = _dot(vth[kh], p)
            l = acc[(1 - kh) * HALF:(1 - kh) * HALF + 1]
            out = acc[kh * HALF:(kh + 1) * HALF]
            if use_sink:
                sk = sink_ref[hg * nh + h] * LOG2E
                m2 = jnp.maximum(m, sk)
                f = jnp.exp2(m - m2)
                l = l * f + jnp.exp2(sk - m2)
                out = out * f
            out = out * (1.0 / l)
            if gate_branch is not None:
                gi = gate_branch * N_HEADS + hg * nh + h
                out = out * jax.nn.sigmoid(gt_ref[0, 0, pl.ds(gi, 1), :])
            outs.append(out)
        tile_t = jnp.concatenate(outs, axis=0)
        o_ref[0, :, tt * LANES:(tt + 1) * LANES] = tile_t.T.astype(o_ref.dtype)


def _banded(q, k, vt, *, nt, tq, window, head_half, out_dtype, sinks=None, gates_t=None,
            gate_branch=None):
    bsz, seq, qw = q.shape
    n_hg = qw // (nt * LANES)
    nh = 2 * nt
    use_sink = sinks is not None
    in_specs = [
        pl.BlockSpec((1, tq, nt * LANES), lambda b, g, i: (b, i, g)),
        pl.BlockSpec((1, seq, LANES), lambda b, g, i: (b, 0, g)),
        pl.BlockSpec((1, seq // tq, LANES, tq), lambda b, g, i: (b, 0, g, 0)),
    ]
    args = [q, k, vt]
    if use_sink:
        in_specs.append(pl.BlockSpec(memory_space=pltpu.SMEM))
        args.append(sinks)
    if gate_branch is not None:
        in_specs.append(pl.BlockSpec((1, 1, LANES, tq), lambda b, g, i: (b, i, 0, 0)))
        args.append(gates_t)
    kern = functools.partial(_banded_kernel, nt=nt, tq=tq, window=window, use_sink=use_sink,
                             gate_branch=gate_branch, head_half=head_half)
    return pl.pallas_call(
        kern,
        grid=(bsz, n_hg, seq // tq),
        in_specs=in_specs,
        out_specs=pl.BlockSpec((1, tq, nt * LANES), lambda b, g, i: (b, i, g)),
        out_shape=jax.ShapeDtypeStruct((bsz, seq, qw), out_dtype),
        scratch_shapes=[
            pltpu.VMEM((nh, tq, LANES), BF16),
            pltpu.VMEM((nh, window + tq, tq), F32),
        ],
        compiler_params=_cparams(("parallel", "parallel", "arbitrary")),
        name="banded",
    )(*args)


N_CHUNK16 = 128


def _compress_hidden(x_ref, pe_ref, w1_ref):
    hid = NSA_GROUPS * NSA_CMP_HIDDEN
    x = jnp.concatenate([x_ref[t, 0, pl.ds(l, N_CHUNK16, stride=NSA_CMP_STRIDE), :]
                         for l in range(NSA_CMP_STRIDE) for t in range(x_ref.shape[0])], axis=1)
    top = _dot((x + pe_ref[0:1, :]).astype(BF16), w1_ref[:, :hid])
    bot = _dot((x + pe_ref[1:2, :]).astype(BF16), w1_ref[:, hid:])
    h1 = top + pltpu.roll(bot, N_CHUNK16 - 1, 0)
    return jax.nn.gelu(h1, approximate=True).astype(BF16)


def _compress_kernel(xk_ref, xv_ref, pek_ref, pev_ref, wk1_ref, wv1_ref, wk2_ref, wv2t_ref,
                     kc_ref, vct_ref):
    kc_ref[0] = _dot(_compress_hidden(xk_ref, pek_ref, wk1_ref), wk2_ref[...]).astype(kc_ref.dtype)
    vct_ref[0] = _nt_dot(wv2t_ref[...], _compress_hidden(xv_ref, pev_ref, wv1_ref)).astype(vct_ref.dtype)


def _compress(xk, xv, pek, pev, wk1, wv1, wk2, wv2t):
    nplane, bsz, seq, _ = xk.shape
    gd = nplane * LANES
    xspec = pl.BlockSpec((nplane, 1, seq, LANES), lambda b: (0, b, 0, 0))
    full = lambda a: pl.BlockSpec(a.shape, lambda b: (0,) * a.ndim)
    return pl.pallas_call(
        _compress_kernel,
        grid=(bsz,),
        in_specs=[xspec, xspec, full(pek), full(pev), full(wk1), full(wv1), full(wk2), full(wv2t)],
        out_specs=[pl.BlockSpec((1, N_CHUNK16, gd), lambda b: (b, 0, 0)),
                   pl.BlockSpec((1, gd, N_CHUNK16), lambda b: (b, 0, 0))],
        out_shape=[jax.ShapeDtypeStruct((bsz, N_CHUNK16, gd), BF16),
                   jax.ShapeDtypeStruct((bsz, gd, N_CHUNK16), BF16)],
        compiler_params=_cparams(("parallel",)),
        name="compress",
    )(xk, xv, pek, pev, wk1, wv1, wk2, wv2t)


CMP_TQ = 256
N_SLC = 32


def _cmp_kernel(q_ref, kc_ref, vct_ref, gt_ref, ovt_ref, o_ref, nsel_ref, qal_ref):
    tq = CMP_TQ
    hg = pl.program_id(1)
    i = pl.program_id(2)
    lane = lax.broadcasted_iota(jnp.int32, (1, LANES), 1)
    half = lane // HALF
    for tt in range(4):
        qt = q_ref[0, :, tt * LANES:(tt + 1) * LANES].astype(F32)
        qr = pltpu.roll(qt, HALF, 1)
        for a in range(2):
            kh = tt // 2
            qal_ref[2 * tt + a] = jnp.where(half == kh, qt if a == kh else qr, 0.0).astype(BF16)
    kc = kc_ref[0]
    vct = vct_ref[0]
    t_row = i * tq + lax.broadcasted_iota(jnp.int32, (1, tq), 1)
    n_col = lax.broadcasted_iota(jnp.int32, (N_CHUNK16, 1), 0)
    n_cmp = (N_CHUNK16 * NSA_CMP_STRIDE - NSA_CMP_LEN) // NSA_CMP_STRIDE + 1
    valid = (n_col * NSA_CMP_STRIDE + (NSA_CMP_LEN - 1) <= t_row) & (n_col < n_cmp)

    scores = [_nt_dot(kc, qal_ref[h]) for h in range(8)]
    probs = []
    for h in range(8):
        s = jnp.where(valid, scores[h], NEG)
        m = jnp.max(s, axis=0, keepdims=True)
        e = jnp.where(valid, jnp.exp2(s - m), 0.0)
        l = jnp.sum(e, axis=0, keepdims=True)
        probs.append(e * (1.0 / jnp.where(l > 0.0, l, 1.0)))
    outs = []
    for h in range(8):
        kh = h // 4
        out = _dot(vct, probs[h].astype(BF16))[kh * HALF:(kh + 1) * HALF]
        outs.append(out * jax.nn.sigmoid(gt_ref[0, 0, pl.ds(hg * 8 + h, 1), :]))
    for tt in range(4):
        tile_t = jnp.concatenate(outs[2 * tt:2 * tt + 2], axis=0)
        o_ref[0, :, tt * LANES:(tt + 1) * LANES] = tile_t.T.astype(o_ref.dtype)

    ovt = ovt_ref[...]
    j = lax.broadcasted_iota(jnp.int32, (N_SLC, 1), 0)
    tb = jnp.right_shift(t_row, 6)
    forced = (j == 0) | (j == tb) | (j == tb - 1)
    for grp in range(2):
        pg = probs[4 * grp:4 * grp + 4]
        psum = (pg[0] + pg[1]) + (pg[2] + pg[3])
        p_hi = psum.astype(BF16)
        p_lo = (psum - p_hi.astype(F32)).astype(BF16)
        imp = _dot(ovt, p_hi) + _dot(ovt, p_lo)
        imp = jnp.where(j > tb, NEG, jnp.where(forced, NSA_FORCE_BONUS, imp))
        cnt = jnp.zeros((N_SLC, tq), jnp.int32)
        for jp in range(N_SLC):
            row = imp[jp:jp + 1, :]
            beats = (row > imp) | ((row == imp) & (jp < j))
            cnt = cnt + beats.astype(jnp.int32)
        sel_t = (cnt < NSA_TOPK).astype(F32)
        sel_t = jnp.concatenate([sel_t, jnp.ones((LANES - N_SLC, tq), F32)], axis=0)
        nsel_ref[0, grp] = ((sel_t.T - 1.0) * SEL_BIG).astype(nsel_ref.dtype)


def _cmp_attn(q, kc, vct, gates_t, ovt):
    bsz, seq, _ = q.shape
    tq = CMP_TQ
    gw = 4 * LANES
    return pl.pallas_call(
        _cmp_kernel,
        grid=(bsz, NSA_GROUPS // 2, seq // tq),
        in_specs=[
            pl.BlockSpec((1, tq, gw), lambda b, g, i: (b, i, g)),
            pl.BlockSpec((1, N_CHUNK16, LANES), lambda b, g, i: (b, 0, g)),
            pl.BlockSpec((1, LANES, N_CHUNK16), lambda b, g, i: (b, g, 0)),
            pl.BlockSpec((1, 1, LANES, tq), lambda b, g, i: (b, i, 0, 0)),
            pl.BlockSpec(ovt.shape, lambda b, g, i: (0, 0)),
        ],
        out_specs=[
            pl.BlockSpec((1, tq, gw), lambda b, g, i: (b, i, g)),
            pl.BlockSpec((1, 2, tq, LANES), lambda b, g, i: (b, g, i, 0)),
        ],
        out_shape=[
            jax.ShapeDtypeStruct((bsz, seq, D_MODEL), BF16),
            jax.ShapeDtypeStruct((bsz, NSA_GROUPS, seq, LANES), BF16),
        ],
        scratch_shapes=[pltpu.VMEM((8, tq, LANES), BF16)],
        compiler_params=_cparams(("parallel", "parallel", "arbitrary")),
        name="cmp_attn",
    )(q, kc, vct, gates_t, ovt)


CS_BLK = 256


def _cumsum_kernel(z_ref, b_ref, c_ref):
    seq, width = z_ref.shape
    tri = (lax.broadcasted_iota(jnp.int32, (CS_BLK, CS_BLK), 0)
           >= lax.broadcasted_iota(jnp.int32, (CS_BLK, CS_BLK), 1)).astype(BF16)
    carry = jnp.zeros((1, width), F32)
    for blk in range(seq // CS_BLK):
        x = jax.nn.log_sigmoid(z_ref[blk * CS_BLK:(blk + 1) * CS_BLK, :] + b_ref[...])
        hi = x.astype(BF16)
        r1 = x - hi.astype(F32)
        mid = r1.astype(BF16)
        lo = (r1 - mid.astype(F32)).astype(BF16)
        cs = _dot(tri, hi) + _dot(tri, mid) + _dot(tri, lo) + carry
        c_ref[blk * CS_BLK:(blk + 1) * CS_BLK, :] = cs * LOG2E
        carry = cs[CS_BLK - 1:CS_BLK, :]


def _cumsum(z, bias):
    seq, width = z.shape
    return pl.pallas_call(
        _cumsum_kernel,
        grid=(width // LANES,),
        in_specs=[pl.BlockSpec((seq, LANES), lambda j: (0, j)),
                  pl.BlockSpec((1, LANES), lambda j: (0, j))],
        out_specs=pl.BlockSpec((seq, LANES), lambda j: (0, j)),
        out_shape=jax.ShapeDtypeStruct((seq, width), F32),
        compiler_params=_cparams(("parallel",)),
        name="cumsum",
    )(z, bias)


def _fox_bias_kernel(c_ref, ck_ref, cq_ref):
    li = pl.program_id(0) % LANES
    c = c_ref[...]
    lane = lax.broadcasted_iota(jnp.int32, c.shape, 1)
    col = jnp.sum(jnp.where(lane == li, c, 0.0), axis=1, keepdims=True)
    hi = col.astype(BF16).astype(F32)
    mid = (col - hi).astype(BF16).astype(F32)
    lo = col - hi - mid
    k_piece = jnp.where(lane == 0, hi, jnp.where(lane == 1, mid, lo))
    q_piece = jnp.where(lane == 3, hi, jnp.where(lane == 4, mid, lo))
    ck_ref[0] = jnp.where(lane < 3, k_piece, jnp.where(lane < 6, 1.0, 0.0)).astype(BF16)
    cq_ref[0] = jnp.where(lane < 3, -1.0, jnp.where(lane < 6, q_piece, 0.0)).astype(BF16)


def _fox_bias(c, n):
    seq = c.shape[0]
    ospec = pl.BlockSpec((1, seq, LANES), lambda j: (j, 0, 0))
    oshape = jax.ShapeDtypeStruct((n, seq, LANES), BF16)
    return pl.pallas_call(
        _fox_bias_kernel,
        grid=(n,),
        in_specs=[pl.BlockSpec((seq, LANES), lambda j: (0, j // LANES))],
        out_specs=[ospec, ospec],
        out_shape=[oshape, oshape],
        compiler_params=_cparams(("parallel",)),
        name="fox_bias",
    )(c)


def _rope_tables(seq):
    inv = ROPE_THETA ** (-jnp.arange(0, HEAD_DIM, 2, dtype=F32) / HEAD_DIM)
    ang = jnp.arange(seq, dtype=F32)[:, None] * inv[None, :]
    cos, sin = jnp.cos(ang), jnp.sin(ang)
    cos_t = jnp.tile(cos, (1, LANES // (HEAD_DIM // 2)))
    sin_t = jnp.tile(jnp.concatenate([-sin, sin], axis=1), (1, LANES // HEAD_DIM))
    return cos_t, sin_t


def _overlap_t(seq):
    n_cmp = (seq - NSA_CMP_LEN) // NSA_CMP_STRIDE + 1
    n_slc = seq // NSA_SLC_LEN
    cs = np.arange(n_cmp) * NSA_CMP_STRIDE
    ce = cs + NSA_CMP_LEN
    ss = np.arange(n_slc) * NSA_SLC_LEN
    se = ss + NSA_SLC_LEN
    ov = np.clip(np.minimum(ce[:, None], se[None, :]) - np.maximum(cs[:, None], ss[None, :]), 0, None)
    ov = (ov / NSA_CMP_LEN).astype(np.float32)
    ovt = np.zeros((n_slc, LANES), np.float32)
    ovt[:, :n_cmp] = ov.T
    return jnp.asarray(ovt, dtype=BF16)


def _expand_mat_t(seq, tk):
    key = np.arange(seq)
    e = ((key // NSA_SLC_LEN)[:, None] == np.arange(LANES)[None, :]).astype(np.float32)
    return jnp.asarray(e.reshape(seq // tk, tk, LANES), dtype=BF16)


def _compress_weights(pe, w1, w2):
    g = NSA_GROUPS
    half = NSA_CMP_LEN // 2
    eye = jnp.eye(g, dtype=F32)
    w1r = w1.reshape(NSA_CMP_LEN, HEAD_DIM, NSA_CMP_HIDDEN)

    def big(part):
        return jnp.einsum('ldj,gh->lgdhj', part, eye).reshape(half * g * HEAD_DIM, g * NSA_CMP_HIDDEN)

    w1big = jnp.concatenate([big(w1r[:half]), big(w1r[half:])], axis=1).astype(BF16)
    w2big = jnp.einsum('jd,gh->gjhd', w2, eye).reshape(g * NSA_CMP_HIDDEN, g * HEAD_DIM).astype(BF16)

    def pebig(part):
        return jnp.broadcast_to(part[:, None, :], (half, g, HEAD_DIM)).reshape(-1)

    pe2 = jnp.stack([pebig(pe[:half]), pebig(pe[half:])], axis=0)
    pe2 = jnp.concatenate([pe2, jnp.zeros((6, pe2.shape[1]), F32)], axis=0)
    return pe2, w1big, w2big


def _pad_cols(w, width):
    return jnp.concatenate([w, jnp.zeros((w.shape[0], width - w.shape[1]), w.dtype)], axis=1)


ATT_T = 256


def _nsa_mixer(x2, bsz, seq, gain, w_in, ck_pe, ck_w1, ck_w2, cv_pe, cv_w1, cv_w2,
               rope_tabs, ovt, emat_t):
    qw = N_HEADS * HEAD_DIM
    gd = NSA_GROUPS * HEAD_DIM

    def kvcols(c, s):
        lo = qw + (c * 2 + s) * gd
        return w_in[:, lo:lo + gd]

    w = jnp.concatenate([w_in[:, :qw]] + [kvcols(c, 0) for c in range(3)] + [kvcols(0, 1)],
                        axis=1).astype(BF16)
    wt = jnp.concatenate([kvcols(1, 1), kvcols(2, 1), _pad_cols(w_in[:, qw + 6 * gd:], LANES)],
                         axis=1).T.astype(BF16)
    scale = HEAD_DIM ** -0.5 * LOG2E
    row_groups = ((0, qw, True, scale, False), (qw, gd, True, 1.0, True),
                  (qw + gd, gd, True, 1.0, False), (qw + 2 * gd, gd, True, 1.0, False),
                  (qw + 3 * gd, gd, False, 1.0, True))
    t_groups = ((0, gd, ATT_T), (gd, gd, ATT_T), (2 * gd, LANES, ATT_T))
    q, k0, k1, k2, v0, v1t, v2t, gates_t = _proj(
        x2, gain, w, row_groups, (BF16, F32, BF16, BF16, F32), seq, wt=wt, t_groups=t_groups,
        t_dtypes=(BF16, BF16, F32), rope_tabs=rope_tabs)
    r3 = lambda a: a.reshape(bsz, seq, a.shape[-1])
    q, k1, k2 = map(r3, (q, k1, k2))

    planes = lambda a: a.reshape(a.shape[0], bsz, seq, LANES)
    xk, xv = planes(k0), planes(v0)
    pek, wk1, wk2 = _compress_weights(ck_pe, ck_w1, ck_w2)
    pev, wv1, wv2 = _compress_weights(cv_pe, cv_w1, cv_w2)
    kc, vct = _compress(xk, xv, pek, pev, wk1, wv1, wk2, wv2.T)

    o_cmp, nsel = _cmp_attn(q, kc, vct, gates_t, ovt)
    grp = tuple(h // 4 for h in range(8))
    common = dict(nt=4, nkv=1, tq=ATT_T, tk=ATT_T, head_slot=(0,) * 8, head_half=grp,
                  out_dtype=BF16, gates_t=gates_t)
    o_slc = _flash(q, k1, v1t, head_group=grp, nsel=nsel, emat=emat_t, gate_branch=1, **common)
    o_win = _banded(q, k2, v2t, nt=4, tq=ATT_T, window=NSA_WINDOW, head_half=grp, out_dtype=BF16,
                    gates_t=gates_t, gate_branch=2)
    flat = lambda a: a.reshape(bsz * seq, D_MODEL)
    return [flat(o_cmp), flat(o_slc), flat(o_win)]


def _swa_mixer(x2, bsz, seq, gain, w_in, sinks, rope_tabs):
    qw = N_HEADS * HEAD_DIM
    scale = HEAD_DIM ** -0.5 * LOG2E
    row_groups = ((0, qw, True, scale, False), (qw, LANES, True, 1.0, False))
    w = w_in[:, :qw + LANES].astype(BF16)
    wt = w_in[:, qw + LANES:].T.astype(BF16)
    q, k, vt = _proj(x2, gain, w, row_groups, (BF16, BF16), seq, wt=wt,
                     t_groups=((0, LANES, SWA_WINDOW),), t_dtypes=(BF16,), rope_tabs=rope_tabs)
    r3 = lambda a: a.reshape(bsz, seq, a.shape[-1])
    o = _banded(r3(q), r3(k), vt, nt=8, tq=SWA_WINDOW, window=SWA_WINDOW,
                head_half=tuple(h // 8 for h in range(16)), out_dtype=BF16, sinks=sinks)
    return [o.reshape(bsz * seq, D_MODEL)]


def _fox_mixer(x2, bsz, seq, gain, w_in, b_f):
    qw = N_HEADS * HEAD_DIM
    scale = HEAD_DIM ** -0.5 * LOG2E
    w = jnp.concatenate([w_in[:, :2 * qw], _pad_cols(w_in[:, 3 * qw:], LANES)], axis=1).astype(BF16)
    wt = w_in[:, 2 * qw:3 * qw].T.astype(BF16)
    row_groups = ((0, qw, False, scale, False), (qw, qw, False, 1.0, False),
                  (2 * qw, LANES, False, 1.0, False))
    q, k, f, vt = _proj(x2, gain, w, row_groups, (BF16, BF16, F32), seq, wt=wt,
                        t_groups=((0, qw, ATT_T),), t_dtypes=(BF16,))
    r3 = lambda a: a.reshape(bsz, seq, a.shape[-1])

    bh = bsz * N_HEADS
    bhp = -(-bh // LANES) * LANES
    z = f.reshape(bsz, seq, LANES)[:, :, :N_HEADS].transpose(1, 0, 2).reshape(seq, bh)
    z = _pad_cols(z, bhp)
    bias = _pad_cols(jnp.tile(b_f, bsz).reshape(1, bh), bhp)
    c = _cumsum(z, bias)
    ck, cq = _fox_bias(c, bh)
    r4 = lambda a: a.reshape(bsz, N_HEADS, seq, LANES)
    o = _flash(r3(q), r3(k), vt, nt=4, nkv=4, tq=ATT_T, tk=ATT_T,
               head_slot=tuple(h // 2 for h in range(8)), head_half=(0, 1) * 4, out_dtype=BF16,
               ck=r4(ck), cq=r4(cq))
    return [o.reshape(bsz * seq, D_MODEL)]


def kernel(x, ffn1_norm, ffn1_w_gu, ffn1_w_down, mix_norm, ffn2_norm, ffn2_w_gu, ffn2_w_down,
           nsa_w_in, nsa_ck_pe, nsa_ck_w1, nsa_ck_w2, nsa_cv_pe, nsa_cv_w1, nsa_cv_w2, nsa_w_out,
           swa_w_in, swa_sinks, swa_w_out, fox_w_in, fox_b_f, fox_w_out, final_norm):
    bsz, seq, _ = x.shape
    depth = ffn1_norm.shape[0]
    rope_tabs = _rope_tables(seq)
    ovt = _overlap_t(seq)
    emat_t = _expand_mat_t(seq, ATT_T)
    x2 = x.reshape(bsz * seq, D_MODEL)
    for i in range(depth):
        kind, j = i % N_MIXERS, i // N_MIXERS
        x2 = _ffn(x2, ffn1_norm[i], ffn1_w_gu, ffn1_w_down, i)
        if kind == 0:
            branches = _nsa_mixer(x2, bsz, seq, mix_norm[i], nsa_w_in[j], nsa_ck_pe[j],
                                  nsa_ck_w1[j], nsa_ck_w2[j], nsa_cv_pe[j], nsa_cv_w1[j],
                                  nsa_cv_w2[j], rope_tabs, ovt, emat_t)
            w_out = nsa_w_out
        elif kind == 1:
            branches = _swa_mixer(x2, bsz, seq, mix_norm[i], swa_w_in[j], swa_sinks[j], rope_tabs)
            w_out = swa_w_out
        else:
            branches = _fox_mixer(x2, bsz, seq, mix_norm[i], fox_w_in[j], fox_b_f[j])
            w_out = fox_w_out
        x2 = _ffn(x2, ffn2_norm[i], ffn2_w_gu, ffn2_w_down, i, branches=branches, w_out=w_out,
                  w_out_layer=j, final_gain=final_norm if i == depth - 1 else None)
    return x2.reshape(bsz, seq, D_MODEL)
```

```python
import functools

import numpy as np
import jax
import jax.numpy as jnp
from jax import lax
from jax.experimental import pallas as pl
from jax.experimental.pallas import tpu as pltpu

D_MODEL = 1024
HEAD_DIM = 64
N_HEADS = 16
D_FF = 2816
RMS_EPS = 1e-6
ROPE_THETA = 10000.0
NEG = -1e30
SEL_BIG = float(2 ** 100)
LOG2E = 1.4426950408889634

NSA_GROUPS = 4
NSA_CMP_LEN = 32
NSA_CMP_STRIDE = 16
NSA_CMP_HIDDEN = 128
NSA_SLC_LEN = 64
NSA_TOPK = 16
NSA_WINDOW = 512
NSA_FORCE_BONUS = 1e4
SWA_WINDOW = 128
N_MIXERS = 3

LANES = 128
HALF = 64
VMEM_LIMIT = 56 * 1024 * 1024

F32 = jnp.float32
BF16 = jnp.bfloat16


def _nt_dot(a, b):
    return lax.dot_general(a, b, (((1,), (1,)), ((), ())), preferred_element_type=F32)


def _dot(a, b):
    return jnp.dot(a, b, preferred_element_type=F32)


def _rms(x, g):
    ms = jnp.mean(x * x, axis=-1, keepdims=True)
    return x * lax.rsqrt(ms + RMS_EPS) * g


def _cparams(sem):
    return pltpu.CompilerParams(dimension_semantics=sem, vmem_limit_bytes=VMEM_LIMIT)


FFN_TM = 512
FFN_TF = 256
FFN_NC = 16


def _ffn_kernel(*refs, n_in, final):
    it = iter(refs)
    x_ref = next(it)
    o_refs = [next(it) for _ in range(n_in)]
    g_ref = next(it)
    wgu_ref = next(it)
    wd_ref = next(it)
    wo_ref = next(it) if n_in else None
    fg_ref = next(it) if final else None
    out_ref = next(it)
    wgu_s = next(it)
    wd_s = next(it)
    wo_s = next(it) if n_in else None

    j = pl.program_id(0)

    @pl.when(j < FFN_NC)
    def _():
        for src, dst in ((wgu_ref, wgu_s), (wd_ref, wd_s)) + (((wo_ref, wo_s),) if n_in else ()):
            rows = src.shape[0]
            dst[pl.ds(pl.multiple_of(j * rows, rows), rows), :] = src[...].astype(BF16)

    @pl.when(j >= FFN_NC)
    def _():
        x = x_ref[...]
        if n_in:
            o = o_refs[0][...].astype(F32)
            for r in o_refs[1:]:
                o = o + r[...].astype(F32)
            x = x + lax.dot_general(o.astype(BF16), wo_s[...], (((0,), (0,)), ((), ())),
                                    preferred_element_type=F32)
        hb = _rms(x, g_ref[...]).astype(BF16)
        acc = jnp.zeros(x.shape, F32)
        for f in range(D_FF // FFN_TF):
            lo = f * FFN_TF
            g = _dot(hb, wgu_s[:, lo:lo + FFN_TF])
            u = _dot(hb, wgu_s[:, D_FF + lo:D_FF + lo + FFN_TF])
            a = (g * jax.nn.sigmoid(g)) * u
            acc = acc + _dot(a.astype(BF16), wd_s[lo:lo + FFN_TF, :])
        y = x + 0.5 * acc
        if final:
            y = _rms(y, fg_ref[...])
        out_ref[...] = y


def _ffn(x2, gain, wgu, wd, layer, branches=(), w_out=None, w_out_layer=0, final_gain=None):
    n = x2.shape[0]
    n_in = len(branches)
    final = final_gain is not None
    slab = lambda lyr: (lambda j: (lyr, jnp.minimum(j, FFN_NC - 1), 0))
    row = pl.BlockSpec((FFN_TM, D_MODEL), lambda j: (jnp.maximum(j - FFN_NC, 0), 0))
    vec = pl.BlockSpec((1, D_MODEL), lambda j: (0, 0))
    nblk = branches[0].shape[2] // FFN_TM if n_in else 1
    brow = pl.BlockSpec((None, D_MODEL, FFN_TM),
                        lambda j: (jnp.maximum(j - FFN_NC, 0) // nblk, 0,
                                   jnp.maximum(j - FFN_NC, 0) % nblk))
    in_specs = [row] + [brow] * n_in + [
        vec,
        pl.BlockSpec((None, D_MODEL // FFN_NC, 2 * D_FF), slab(layer)),
        pl.BlockSpec((None, D_FF // FFN_NC, D_MODEL), slab(layer)),
    ]
    args = [x2, *branches, gain.reshape(1, D_MODEL), wgu, wd]
    scratch = [pltpu.VMEM((D_MODEL, 2 * D_FF), BF16), pltpu.VMEM((D_FF, D_MODEL), BF16)]
    if n_in:
        in_specs.append(pl.BlockSpec((None, D_MODEL // FFN_NC, D_MODEL), slab(w_out_layer)))
        args.append(w_out)
        scratch.append(pltpu.VMEM((D_MODEL, D_MODEL), BF16))
    if final:
        in_specs.append(vec)
        args.append(final_gain.reshape(1, D_MODEL))
    return pl.pallas_call(
        functools.partial(_ffn_kernel, n_in=n_in, final=final),
        grid=(FFN_NC + n // FFN_TM,),
        in_specs=in_specs,
        out_specs=row,
        out_shape=jax.ShapeDtypeStruct((n, D_MODEL), F32),
        scratch_shapes=scratch,
        compiler_params=_cparams(("arbitrary",)),
        name="ffn",
    )(*args)


PROJ_TM = 512
PROJ_CH = 256


def _rope_tile(y, cos_t, sin_t, first_half):
    rot = jnp.where(first_half, pltpu.roll(y, 96, 1), pltpu.roll(y, 32, 1))
    return y * cos_t + rot * sin_t


def _proj_kernel(*refs, row_groups, t_groups, use_rope):
    it = iter(refs)
    x_ref = next(it)
    g_ref = next(it)
    w_ref = next(it)
    wt_ref = next(it) if t_groups else None
    if use_rope:
        cos_t = next(it)[...]
        sin_t = next(it)[...]
        lane = lax.broadcasted_iota(jnp.int32, (1, LANES), 1)
        first_half = (lane % HALF) < (HALF // 2)
    o_refs = list(it)
    hb = _rms(x_ref[...], g_ref[...]).astype(BF16)
    tm = hb.shape[0]
    for (c0, width, rope, scale, planar), o_ref in zip(row_groups, o_refs):
        for t0 in range(0, width, PROJ_CH):
            ch = min(PROJ_CH, width - t0)
            y = _dot(hb, w_ref[:, c0 + t0:c0 + t0 + ch])
            for l0 in range(0, ch, LANES):
                yt = y[:, l0:l0 + LANES]
                if rope:
                    yt = _rope_tile(yt, cos_t, sin_t, first_half)
                if scale != 1.0:
                    yt = yt * scale
                if planar:
                    o_ref[(t0 + l0) // LANES] = yt.astype(o_ref.dtype)
                else:
                    o_ref[:, t0 + l0:t0 + l0 + LANES] = yt.astype(o_ref.dtype)
    for (c0, width, ck), o_ref in zip(t_groups, o_refs[len(row_groups):]):
        for t0 in range(0, width, PROJ_CH):
            ch = min(PROJ_CH, width - t0)
            yt = _nt_dot(wt_ref[c0 + t0:c0 + t0 + ch, :], hb)
            for s0 in range(0, tm, ck):
                o_ref[0, s0 // ck, t0:t0 + ch, :] = yt[:, s0:s0 + ck].astype(o_ref.dtype)


def _proj(x2, gain, w, row_groups, row_dtypes, seq, wt=None, t_groups=(), t_dtypes=(),
          rope_tabs=None):
    n = x2.shape[0]
    bsz = n // seq
    nblk = seq // PROJ_TM
    use_rope = rope_tabs is not None
    resident = dict(pipeline_mode=pl.Buffered(1))
    in_specs = [
        pl.BlockSpec((PROJ_TM, D_MODEL), lambda i: (i, 0)),
        pl.BlockSpec((1, D_MODEL), lambda i: (0, 0)),
        pl.BlockSpec(w.shape, lambda i: (0, 0), **resident),
    ]
    args = [x2, gain.reshape(1, D_MODEL), w]
    if t_groups:
        in_specs.append(pl.BlockSpec(wt.shape, lambda i: (0, 0), **resident))
        args.append(wt)
    if use_rope:
        in_specs += [pl.BlockSpec((PROJ_TM, LANES), lambda i: (i % nblk, 0))] * 2
        args += list(rope_tabs)
    out_specs, out_shape = [], []
    for (_, width, _, _, planar), dt in zip(row_groups, row_dtypes):
        if planar:
            out_specs.append(pl.BlockSpec((width // LANES, PROJ_TM, LANES), lambda i: (0, i, 0)))
            out_shape.append(jax.ShapeDtypeStruct((width // LANES, n, LANES), dt))
        else:
            out_specs.append(pl.BlockSpec((PROJ_TM, width), lambda i: (i, 0)))
            out_shape.append(jax.ShapeDtypeStruct((n, width), dt))
    for (_, width, ck), dt in zip(t_groups, t_dtypes):
        out_specs.append(pl.BlockSpec((1, PROJ_TM // ck, width, ck),
                                      lambda i: (i // nblk, i % nblk, 0, 0)))
        out_shape.append(jax.ShapeDtypeStruct((bsz, seq // ck, width, ck), dt))
    return pl.pallas_call(
        functools.partial(_proj_kernel, row_groups=row_groups, t_groups=t_groups,
                          use_rope=use_rope),
        grid=(n // PROJ_TM,),
        in_specs=in_specs,
        out_specs=out_specs,
        out_shape=out_shape,
        compiler_params=_cparams(("parallel",)),
        name="proj",
    )(*args)


def _flash_kernel(*refs, nt, tq, tk, window, use_sel, use_fox, use_sink, gate_branch,
                  head_slot, head_half, head_group):
    it = iter(refs)
    q_ref = next(it)
    k_ref = next(it)
    vt_ref = next(it)
    if use_sel:
        nsel_ref = next(it)
        e_ref = next(it)
    if use_fox:
        ck_ref = next(it)
        cq_ref = next(it)
    if use_sink:
        sink_ref = next(it)
    if gate_branch is not None:
        gt_ref = next(it)
    o_ref = next(it)
    qal_ref = next(it)
    m_ref = next(it)
    acc_ref = next(it)
    s_ref = next(it)
    p_ref = next(it)
    a_ref = next(it)

    hg = pl.program_id(1)
    i = pl.program_id(2)
    nh = 2 * nt

    lane = lax.broadcasted_iota(jnp.int32, (1, LANES), 1)
    half = lane // HALF
    for tt in range(nt):
        qt = q_ref[0, :, tt * LANES:(tt + 1) * LANES].astype(F32)
        qr = pltpu.roll(qt, HALF, 1) if any(head_half[2 * tt + a] != a for a in range(2)) else None
        for a in range(2):
            h = 2 * tt + a
            src = qt if head_half[h] == a else qr
            qal_ref[h, :, :LANES] = jnp.where(half == head_half[h], src, 0.0).astype(BF16)
            if use_sel:
                qal_ref[h, :, LANES:] = nsel_ref[0, head_group[h]]
            if use_fox:
                qal_ref[h, :, LANES:] = cq_ref[0, h]

    m_ref[...] = jnp.full(m_ref.shape, NEG, F32)
    acc_ref[...] = jnp.zeros(acc_ref.shape, F32)

    row_half = lax.broadcasted_iota(jnp.int32, (LANES, 1), 0) // HALF
    q0 = i * tq
    colrow = (lax.broadcasted_iota(jnp.int32, (tk, tq), 1)
              - lax.broadcasted_iota(jnp.int32, (tk, tq), 0))

    if window is None:
        c_lo = 0
    else:
        c_lo = jnp.maximum(q0 - (window - 1), 0) // tk
    c_hi = (q0 + tq + tk - 1) // tk

    def scores(c, h):
        k0 = pl.multiple_of(c * tk, tk)
        sl = head_slot[h]
        kc = k_ref[0, pl.ds(k0, tk), sl * LANES:(sl + 1) * LANES]
        if use_sel:
            kc = jnp.concatenate([kc, e_ref[c]], axis=1)
        if use_fox:
            kc = jnp.concatenate([kc, ck_ref[0, h, pl.ds(k0, tk), :]], axis=1)
        return _nt_dot(kc, qal_ref[h])

    def values_t(c):
        made = {}
        for h in range(nh):
            key = (head_slot[h], head_half[h])
            if key not in made:
                vt = vt_ref[0, c, key[0] * LANES:(key[0] + 1) * LANES, :]
                made[key] = jnp.where(row_half == key[1], vt, jnp.ones_like(vt))
        return [made[(head_slot[h], head_half[h])] for h in range(nh)]

    def accumulate(vth, par, h):
        acc_ref[h] = a_ref[par, h] * acc_ref[h] + _dot(vth[h], p_ref[par, h])

    for h in range(nh):
        s_ref[0, h] = scores(c_lo, h)
    p_ref[1] = jnp.zeros(p_ref.shape[1:], BF16)
    a_ref[...] = jnp.ones(a_ref.shape, F32)

    def trip(c, par, masked):
        k0 = pl.multiple_of(c * tk, tk)
        if masked:
            d = colrow + (q0 - k0)
            mask = d >= 0
            if window is not None:
                mask = mask & (d < window)
        c_next = jnp.minimum(c + 1, c_hi - 1)
        vth = values_t(jnp.maximum(c - 1, c_lo))
        for h in range(nh):
            accumulate(vth, 1 - par, h)
            s_ref[1 - par, h] = scores(c_next, h)
            s = s_ref[par, h]
            if masked:
                s = jnp.where(mask, s, NEG)
            m_old = m_ref[h]
            m_new = jnp.maximum(m_old, jnp.max(s, axis=0, keepdims=True))
            a_ref[par, h] = jnp.exp2(m_old - m_new)
            p_ref[par, h] = jnp.exp2(s - m_new).astype(BF16)
            m_ref[h] = m_new

    def chunk(c, carry):
        odd = (c - c_lo) % 2
        k0 = c * tk
        edge = k0 + (tk - 1) > q0
        if window is not None:
            edge = edge | (k0 + window <= q0 + (tq - 1))
        for par in range(2):
            pl.when((odd == par) & edge)(functools.partial(trip, c, par, True))
            if window is None or window > tk:
                pl.when((odd == par) & jnp.logical_not(edge))(functools.partial(trip, c, par, False))
        return carry

    lax.fori_loop(c_lo, c_hi, chunk, 0)
    last_odd = (c_hi - 1 - c_lo) % 2

    def drain(par):
        vth = values_t(c_hi - 1)
        for h in range(nh):
            accumulate(vth, par, h)

    pl.when(last_odd == 0)(functools.partial(drain, 0))
    pl.when(last_odd == 1)(functools.partial(drain, 1))

    for tt in range(nt):
        outs = []
        for a in range(2):
            h = 2 * tt + a
            acc = acc_ref[h]
            kh = head_half[h]
            l = acc[(1 - kh) * HALF:(1 - kh) * HALF + 1]
            out = acc[kh * HALF:(kh + 1) * HALF]
            if use_sink:
                m = m_ref[h]
                sk = sink_ref[hg * nh + h] * LOG2E
                m2 = jnp.maximum(m, sk)
                f = jnp.exp2(m - m2)
                l = l * f + jnp.exp2(sk - m2)
                out = out * f
            out = out * (1.0 / l)
            if gate_branch is not None:
                gi = gate_branch * N_HEADS + hg * nh + h
                out = out * jax.nn.sigmoid(gt_ref[0, 0, pl.ds(gi, 1), :])
            outs.append(out)
        tile_t = jnp.concatenate(outs, axis=0)
        o_ref[0, tt * LANES:(tt + 1) * LANES, :] = tile_t.astype(o_ref.dtype)


def _flash(q, k, vt, *, nt, nkv, tq, tk, head_slot, head_half, out_dtype, head_group=None,
           window=None, nsel=None, emat=None, ck=None, cq=None, sinks=None, gates_t=None,
           gate_branch=None):
    bsz, seq, qw = q.shape
    n_hg = qw // (nt * LANES)
    nq = seq // tq
    nk = seq // tk
    nh = 2 * nt
    use_sel = nsel is not None
    use_fox = ck is not None
    use_sink = sinks is not None
    kw = 2 * LANES if (use_sel or use_fox) else LANES
    in_specs = [
        pl.BlockSpec((1, tq, nt * LANES), lambda b, g, i: (b, i, g)),
        pl.BlockSpec((1, seq, nkv * LANES), lambda b, g, i: (b, 0, g)),
        pl.BlockSpec((1, nk, nkv * LANES, tk), lambda b, g, i: (b, 0, g, 0)),
    ]
    args = [q, k, vt]
    if use_sel:
        ngrp = max(head_group) + 1
        in_specs += [
            pl.BlockSpec((1, ngrp, tq, LANES), lambda b, g, i: (b, g, i, 0)),
            pl.BlockSpec(emat.shape, lambda b, g, i: (0, 0, 0)),
        ]
        args += [nsel, emat]
    if use_fox:
        in_specs += [
            pl.BlockSpec((1, nh, seq, LANES), lambda b, g, i: (b, g, 0, 0)),
            pl.BlockSpec((1, nh, tq, LANES), lambda b, g, i: (b, g, i, 0)),
        ]
        args += [ck, cq]
    if use_sink:
        in_specs.append(pl.BlockSpec(memory_space=pltpu.SMEM))
        args.append(sinks)
    if gate_branch is not None:
        in_specs.append(pl.BlockSpec((1, 1, LANES, tq), lambda b, g, i: (b, i, 0, 0)))
        args.append(gates_t)
    kern = functools.partial(
        _flash_kernel, nt=nt, tq=tq, tk=tk, window=window, use_sel=use_sel, use_fox=use_fox,
        use_sink=use_sink, gate_branch=gate_branch, head_slot=head_slot, head_half=head_half,
        head_group=head_group)
    return pl.pallas_call(
        kern,
        grid=(bsz, n_hg, nq),
        in_specs=in_specs,
        out_specs=pl.BlockSpec((1, nt * LANES, tq), lambda b, g, i: (b, g, i)),
        out_shape=jax.ShapeDtypeStruct((bsz, qw, seq), out_dtype),
        scratch_shapes=[
            pltpu.VMEM((nh, tq, kw), BF16),
            pltpu.VMEM((nh, 1, tq), F32),
            pltpu.VMEM((nh, LANES, tq), F32),
            pltpu.VMEM((2, nh, tk, tq), F32),
            pltpu.VMEM((2, nh, tk, tq), BF16),
            pltpu.VMEM((2, nh, 1, tq), F32),
        ],
        compiler_params=_cparams(("parallel", "parallel", "arbitrary")),
        name="flash",
    )(*args)


def _banded_kernel(*refs, nt, tq, window, use_sink, gate_branch, head_half):
    it = iter(refs)
    q_ref = next(it)
    k_ref = next(it)
    vt_ref = next(it)
    if use_sink:
        sink_ref = next(it)
    if gate_branch is not None:
        gt_ref = next(it)
    o_ref = next(it)
    qal_ref = next(it)
    s_ref = next(it)

    hg = pl.program_id(1)
    i = pl.program_id(2)
    nh = 2 * nt
    span = window + tq
    q0 = i * tq
    k_start = pl.multiple_of(jnp.maximum(q0 - window, 0), tq)

    lane = lax.broadcasted_iota(jnp.int32, (1, LANES), 1)
    half = lane // HALF
    for tt in range(nt):
        qt = q_ref[0, :, tt * LANES:(tt + 1) * LANES].astype(F32)
        qr = pltpu.roll(qt, HALF, 1) if any(head_half[2 * tt + a] != a for a in range(2)) else None
        for a in range(2):
            h = 2 * tt + a
            src = qt if head_half[h] == a else qr
            qal_ref[h] = jnp.where(half == head_half[h], src, 0.0).astype(BF16)

    kspan = k_ref[0, pl.ds(k_start, span), :]
    for h in range(nh):
        s_ref[h] = _nt_dot(kspan, qal_ref[h])

    d = (q0 - k_start) + (lax.broadcasted_iota(jnp.int32, (span, tq), 1)
                          - lax.broadcasted_iota(jnp.int32, (span, tq), 0))
    mask = (d >= 0) & (d < window)
    c0 = k_start // tq
    vt = jnp.concatenate([vt_ref[0, c0 + j] for j in range(span // tq)], axis=1)
    row_half = lax.broadcasted_iota(jnp.int32, (LANES, 1), 0) // HALF
    vth = {kh: jnp.where(row_half == kh, vt, jnp.ones_like(vt)) for kh in set(head_half)}

    for tt in range(nt):
        outs = []
        for a in range(2):
            h = 2 * tt + a
            kh = head_half[h]
            s = jnp.where(mask, s_ref[h], NEG)
            m = jnp.max(s, axis=0, keepdims=True)
            p = jnp.exp2(s - m).astype(BF16)
            acc = _dot(vth[kh], p)
            l = acc[(1 - kh) * HALF:(1 - kh) * HALF + 1]
            out = acc[kh * HALF:(kh + 1) * HALF]
            if use_sink:
                sk = sink_ref[hg * nh + h] * LOG2E
                m2 = jnp.maximum(m, sk)
                f = jnp.exp2(m - m2)
                l = l * f + jnp.exp2(sk - m2)
                out = out * f
            out = out * (1.0 / l)
            if gate_branch is not None:
                gi = gate_branch * N_HEADS + hg * nh + h
                out = out * jax.nn.sigmoid(gt_ref[0, 0, pl.ds(gi, 1), :])
            outs.append(out)
        tile_t = jnp.concatenate(outs, axis=0)
        o_ref[0, tt * LANES:(tt + 1) * LANES, :] = tile_t.astype(o_ref.dtype)


def _banded(q, k, vt, *, nt, tq, window, head_half, out_dtype, sinks=None, gates_t=None,
            gate_branch=None):
    bsz, seq, qw = q.shape
    n_hg = qw // (nt * LANES)
    nh = 2 * nt
    use_sink = sinks is not None
    in_specs = [
        pl.BlockSpec((1, tq, nt * LANES), lambda b, g, i: (b, i, g)),
        pl.BlockSpec((1, seq, LANES), lambda b, g, i: (b, 0, g)),
        pl.BlockSpec((1, seq // tq, LANES, tq), lambda b, g, i: (b, 0, g, 0)),
    ]
    args = [q, k, vt]
    if use_sink:
        in_specs.append(pl.BlockSpec(memory_space=pltpu.SMEM))
        args.append(sinks)
    if gate_branch is not None:
        in_specs.append(pl.BlockSpec((1, 1, LANES, tq), lambda b, g, i: (b, i, 0, 0)))
        args.append(gates_t)
    kern = functools.partial(_banded_kernel, nt=nt, tq=tq, window=window, use_sink=use_sink,
                             gate_branch=gate_branch, head_half=head_half)
    return pl.pallas_call(
        kern,
        grid=(bsz, n_hg, seq // tq),
        in_specs=in_specs,
        out_specs=pl.BlockSpec((1, nt * LANES, tq), lambda b, g, i: (b, g, i)),
        out_shape=jax.ShapeDtypeStruct((bsz, qw, seq), out_dtype),
        scratch_shapes=[
            pltpu.VMEM((nh, tq, LANES), BF16),
            pltpu.VMEM((nh, window + tq, tq), F32),
        ],
        compiler_params=_cparams(("parallel", "parallel", "arbitrary")),
        name="banded",
    )(*args)


N_CHUNK16 = 128


def _compress_hidden(x_ref, pe_ref, w1_ref):
    hid = NSA_GROUPS * NSA_CMP_HIDDEN
    x = jnp.concatenate([x_ref[t, 0, pl.ds(l, N_CHUNK16, stride=NSA_CMP_STRIDE), :]
                         for l in range(NSA_CMP_STRIDE) for t in range(x_ref.shape[0])], axis=1)
    top = _dot((x + pe_ref[0:1, :]).astype(BF16), w1_ref[:, :hid])
    bot = _dot((x + pe_ref[1:2, :]).astype(BF16), w1_ref[:, hid:])
    h1 = top + pltpu.roll(bot, N_CHUNK16 - 1, 0)
    return jax.nn.gelu(h1, approximate=True).astype(BF16)


def _compress_kernel(xk_ref, xv_ref, pek_ref, pev_ref, wk1_ref, wv1_ref, wk2_ref, wv2t_ref,
                     kc_ref, vct_ref):
    kc_ref[0] = _dot(_compress_hidden(xk_ref, pek_ref, wk1_ref), wk2_ref[...]).astype(kc_ref.dtype)
    vct_ref[0] = _nt_dot(wv2t_ref[...], _compress_hidden(xv_ref, pev_ref, wv1_ref)).astype(vct_ref.dtype)


def _compress(xk, xv, pek, pev, wk1, wv1, wk2, wv2t):
    nplane, bsz, seq, _ = xk.shape
    gd = nplane * LANES
    xspec = pl.BlockSpec((nplane, 1, seq, LANES), lambda b: (0, b, 0, 0))
    full = lambda a: pl.BlockSpec(a.shape, lambda b: (0,) * a.ndim)
    return pl.pallas_call(
        _compress_kernel,
        grid=(bsz,),
        in_specs=[xspec, xspec, full(pek), full(pev), full(wk1), full(wv1), full(wk2), full(wv2t)],
        out_specs=[pl.BlockSpec((1, N_CHUNK16, gd), lambda b: (b, 0, 0)),
                   pl.BlockSpec((1, gd, N_CHUNK16), lambda b: (b, 0, 0))],
        out_shape=[jax.ShapeDtypeStruct((bsz, N_CHUNK16, gd), BF16),
                   jax.ShapeDtypeStruct((bsz, gd, N_CHUNK16), BF16)],
        compiler_params=_cparams(("parallel",)),
        name="compress",
    )(xk, xv, pek, pev, wk1, wv1, wk2, wv2t)


CMP_TQ = 256
N_SLC = 32


def _cmp_kernel(q_ref, kc_ref, vct_ref, gt_ref, ovt_ref, o_ref, nsel_ref, qal_ref):
    tq = CMP_TQ
    hg = pl.program_id(1)
    i = pl.program_id(2)
    lane = lax.broadcasted_iota(jnp.int32, (1, LANES), 1)
    half = lane // HALF
    for tt in range(4):
        qt = q_ref[0, :, tt * LANES:(tt + 1) * LANES].astype(F32)
        qr = pltpu.roll(qt, HALF, 1)
        for a in range(2):
            kh = tt // 2
            qal_ref[2 * tt + a] = jnp.where(half == kh, qt if a == kh else qr, 0.0).astype(BF16)
    kc = kc_ref[0]
    vct = vct_ref[0]
    t_row = i * tq + lax.broadcasted_iota(jnp.int32, (1, tq), 1)
    n_col = lax.broadcasted_iota(jnp.int32, (N_CHUNK16, 1), 0)
    n_cmp = (N_CHUNK16 * NSA_CMP_STRIDE - NSA_CMP_LEN) // NSA_CMP_STRIDE + 1
    valid = (n_col * NSA_CMP_STRIDE + (NSA_CMP_LEN - 1) <= t_row) & (n_col < n_cmp)

    scores = [_nt_dot(kc, qal_ref[h]) for h in range(8)]
    probs = []
    for h in range(8):
        s = jnp.where(valid, scores[h], NEG)
        m = jnp.max(s, axis=0, keepdims=True)
        e = jnp.where(valid, jnp.exp2(s - m), 0.0)
        l = jnp.sum(e, axis=0, keepdims=True)
        probs.append(e * (1.0 / jnp.where(l > 0.0, l, 1.0)))
    outs = []
    for h in range(8):
        kh = h // 4
        out = _dot(vct, probs[h].astype(BF16))[kh * HALF:(kh + 1) * HALF]
        outs.append(out * jax.nn.sigmoid(gt_ref[0, 0, pl.ds(hg * 8 + h, 1), :]))
    for tt in range(4):
        tile_t = jnp.concatenate(outs[2 * tt:2 * tt + 2], axis=0)
        o_ref[0, tt * LANES:(tt + 1) * LANES, :] = tile_t.astype(o_ref.dtype)

    ovt = ovt_ref[...]
    j = lax.broadcasted_iota(jnp.int32, (N_SLC, 1), 0)
    tb = jnp.right_shift(t_row, 6)
    forced = (j == 0) | (j == tb) | (j == tb - 1)
    for grp in range(2):
        pg = probs[4 * grp:4 * grp + 4]
        psum = (pg[0] + pg[1]) + (pg[2] + pg[3])
        p_hi = psum.astype(BF16)
        p_lo = (psum - p_hi.astype(F32)).astype(BF16)
        imp = _dot(ovt, p_hi) + _dot(ovt, p_lo)
        imp = jnp.where(j > tb, NEG, jnp.where(forced, NSA_FORCE_BONUS, imp))
        cnt = jnp.zeros((N_SLC, tq), jnp.int32)
        for jp in range(N_SLC):
            row = imp[jp:jp + 1, :]
            beats = (row > imp) | ((row == imp) & (jp < j))
            cnt = cnt + beats.astype(jnp.int32)
        sel_t = (cnt < NSA_TOPK).astype(F32)
        sel_t = jnp.concatenate([sel_t, jnp.ones((LANES - N_SLC, tq), F32)], axis=0)
        nsel_ref[0, grp] = ((sel_t.T - 1.0) * SEL_BIG).astype(nsel_ref.dtype)


def _cmp_attn(q, kc, vct, gates_t, ovt):
    bsz, seq, _ = q.shape
    tq = CMP_TQ
    gw = 4 * LANES
    return pl.pallas_call(
        _cmp_kernel,
        grid=(bsz, NSA_GROUPS // 2, seq // tq),
        in_specs=[
            pl.BlockSpec((1, tq, gw), lambda b, g, i: (b, i, g)),
            pl.BlockSpec((1, N_CHUNK16, LANES), lambda b, g, i: (b, 0, g)),
            pl.BlockSpec((1, LANES, N_CHUNK16), lambda b, g, i: (b, g, 0)),
            pl.BlockSpec((1, 1, LANES, tq), lambda b, g, i: (b, i, 0, 0)),
            pl.BlockSpec(ovt.shape, lambda b, g, i: (0, 0)),
        ],
        out_specs=[
            pl.BlockSpec((1, gw, tq), lambda b, g, i: (b, g, i)),
            pl.BlockSpec((1, 2, tq, LANES), lambda b, g, i: (b, g, i, 0)),
        ],
        out_shape=[
            jax.ShapeDtypeStruct((bsz, D_MODEL, seq), BF16),
            jax.ShapeDtypeStruct((bsz, NSA_GROUPS, seq, LANES), BF16),
        ],
        scratch_shapes=[pltpu.VMEM((8, tq, LANES), BF16)],
        compiler_params=_cparams(("parallel", "parallel", "arbitrary")),
        name="cmp_attn",
    )(q, kc, vct, gates_t, ovt)


CS_BLK = 256


def _cumsum_kernel(z_ref, b_ref, c_ref):
    seq, width = z_ref.shape
    tri = (lax.broadcasted_iota(jnp.int32, (CS_BLK, CS_BLK), 0)
           >= lax.broadcasted_iota(jnp.int32, (CS_BLK, CS_BLK), 1)).astype(BF16)
    carry = jnp.zeros((1, width), F32)
    for blk in range(seq // CS_BLK):
        x = jax.nn.log_sigmoid(z_ref[blk * CS_BLK:(blk + 1) * CS_BLK, :] + b_ref[...])
        hi = x.astype(BF16)
        r1 = x - hi.astype(F32)
        mid = r1.astype(BF16)
        lo = (r1 - mid.astype(F32)).astype(BF16)
        cs = _dot(tri, hi) + _dot(tri, mid) + _dot(tri, lo) + carry
        c_ref[blk * CS_BLK:(blk + 1) * CS_BLK, :] = cs * LOG2E
        carry = cs[CS_BLK - 1:CS_BLK, :]


def _cumsum(z, bias):
    seq, width = z.shape
    return pl.pallas_call(
        _cumsum_kernel,
        grid=(width // LANES,),
        in_specs=[pl.BlockSpec((seq, LANES), lambda j: (0, j)),
                  pl.BlockSpec((1, LANES), lambda j: (0, j))],
        out_specs=pl.BlockSpec((seq, LANES), lambda j: (0, j)),
        out_shape=jax.ShapeDtypeStruct((seq, width), F32),
        compiler_params=_cparams(("parallel",)),
        name="cumsum",
    )(z, bias)


FOX_BIAS_COLS = 8


def _fox_bias_kernel(c_ref, ck_ref, cq_ref):
    c = c_ref[...]
    lane = lax.broadcasted_iota(jnp.int32, c.shape, 1)
    for k in range(FOX_BIAS_COLS):
        li = (pl.program_id(0) * FOX_BIAS_COLS + k) % LANES
        col = jnp.sum(jnp.where(lane == li, c, 0.0), axis=1, keepdims=True)
        hi = col.astype(BF16).astype(F32)
        mid = (col - hi).astype(BF16).astype(F32)
        lo = col - hi - mid
        k_piece = jnp.where(lane == 0, hi, jnp.where(lane == 1, mid, lo))
        q_piece = jnp.where(lane == 3, hi, jnp.where(lane == 4, mid, lo))
        ck_ref[k] = jnp.where(lane < 3, k_piece, jnp.where(lane < 6, 1.0, 0.0)).astype(BF16)
        cq_ref[k] = jnp.where(lane < 3, -1.0, jnp.where(lane < 6, q_piece, 0.0)).astype(BF16)


def _fox_bias(c, n):
    seq = c.shape[0]
    ospec = pl.BlockSpec((FOX_BIAS_COLS, seq, LANES), lambda j: (j, 0, 0))
    oshape = jax.ShapeDtypeStruct((n, seq, LANES), BF16)
    return pl.pallas_call(
        _fox_bias_kernel,
        grid=(n // FOX_BIAS_COLS,),
        in_specs=[pl.BlockSpec((seq, LANES), lambda j: (0, j * FOX_BIAS_COLS // LANES))],
        out_specs=[ospec, ospec],
        out_shape=[oshape, oshape],
        compiler_params=_cparams(("parallel",)),
        name="fox_bias",
    )(c)


def _rope_tables(seq):
    inv = ROPE_THETA ** (-jnp.arange(0, HEAD_DIM, 2, dtype=F32) / HEAD_DIM)
    ang = jnp.arange(seq, dtype=F32)[:, None] * inv[None, :]
    cos, sin = jnp.cos(ang), jnp.sin(ang)
    cos_t = jnp.tile(cos, (1, LANES // (HEAD_DIM // 2)))
    sin_t = jnp.tile(jnp.concatenate([-sin, sin], axis=1), (1, LANES // HEAD_DIM))
    return cos_t, sin_t


def _overlap_t(seq):
    n_cmp = (seq - NSA_CMP_LEN) // NSA_CMP_STRIDE + 1
    n_slc = seq // NSA_SLC_LEN
    cs = np.arange(n_cmp) * NSA_CMP_STRIDE
    ce = cs + NSA_CMP_LEN
    ss = np.arange(n_slc) * NSA_SLC_LEN
    se = ss + NSA_SLC_LEN
    ov = np.clip(np.minimum(ce[:, None], se[None, :]) - np.maximum(cs[:, None], ss[None, :]), 0, None)
    ov = (ov / NSA_CMP_LEN).astype(np.float32)
    ovt = np.zeros((n_slc, LANES), np.float32)
    ovt[:, :n_cmp] = ov.T
    return jnp.asarray(ovt, dtype=BF16)


def _expand_mat_t(seq, tk):
    key = np.arange(seq)
    e = ((key // NSA_SLC_LEN)[:, None] == np.arange(LANES)[None, :]).astype(np.float32)
    return jnp.asarray(e.reshape(seq // tk, tk, LANES), dtype=BF16)


def _compress_weights(pe, w1, w2):
    g = NSA_GROUPS
    half = NSA_CMP_LEN // 2
    eye = jnp.eye(g, dtype=F32)
    w1r = w1.reshape(NSA_CMP_LEN, HEAD_DIM, NSA_CMP_HIDDEN)

    def big(part):
        return jnp.einsum('ldj,gh->lgdhj', part, eye).reshape(half * g * HEAD_DIM, g * NSA_CMP_HIDDEN)

    w1big = jnp.concatenate([big(w1r[:half]), big(w1r[half:])], axis=1).astype(BF16)
    w2big = jnp.einsum('jd,gh->gjhd', w2, eye).reshape(g * NSA_CMP_HIDDEN, g * HEAD_DIM).astype(BF16)

    def pebig(part):
        return jnp.broadcast_to(part[:, None, :], (half, g, HEAD_DIM)).reshape(-1)

    pe2 = jnp.stack([pebig(pe[:half]), pebig(pe[half:])], axis=0)
    pe2 = jnp.concatenate([pe2, jnp.zeros((6, pe2.shape[1]), F32)], axis=0)
    return pe2, w1big, w2big


def _pad_cols(w, width):
    return jnp.concatenate([w, jnp.zeros((w.shape[0], width - w.shape[1]), w.dtype)], axis=1)


ATT_T = 256


def _nsa_mixer(x2, bsz, seq, gain, w_in, ck_pe, ck_w1, ck_w2, cv_pe, cv_w1, cv_w2,
               rope_tabs, ovt, emat_t):
    qw = N_HEADS * HEAD_DIM
    gd = NSA_GROUPS * HEAD_DIM

    def kvcols(c, s):
        lo = qw + (c * 2 + s) * gd
        return w_in[:, lo:lo + gd]

    w = jnp.concatenate([w_in[:, :qw]] + [kvcols(c, 0) for c in range(3)] + [kvcols(0, 1)],
                        axis=1).astype(BF16)
    wt = jnp.concatenate([kvcols(1, 1), kvcols(2, 1), _pad_cols(w_in[:, qw + 6 * gd:], LANES)],
                         axis=1).T.astype(BF16)
    scale = HEAD_DIM ** -0.5 * LOG2E
    row_groups = ((0, qw, True, scale, False), (qw, gd, True, 1.0, True),
                  (qw + gd, gd, True, 1.0, False), (qw + 2 * gd, gd, True, 1.0, False),
                  (qw + 3 * gd, gd, False, 1.0, True))
    t_groups = ((0, gd, ATT_T), (gd, gd, ATT_T), (2 * gd, LANES, ATT_T))
    q, k0, k1, k2, v0, v1t, v2t, gates_t = _proj(
        x2, gain, w, row_groups, (BF16, F32, BF16, BF16, F32), seq, wt=wt, t_groups=t_groups,
        t_dtypes=(BF16, BF16, F32), rope_tabs=rope_tabs)
    r3 = lambda a: a.reshape(bsz, seq, a.shape[-1])
    q, k1, k2 = map(r3, (q, k1, k2))

    planes = lambda a: a.reshape(a.shape[0], bsz, seq, LANES)
    xk, xv = planes(k0), planes(v0)
    pek, wk1, wk2 = _compress_weights(ck_pe, ck_w1, ck_w2)
    pev, wv1, wv2 = _compress_weights(cv_pe, cv_w1, cv_w2)
    kc, vct = _compress(xk, xv, pek, pev, wk1, wv1, wk2, wv2.T)

    o_cmp, nsel = _cmp_attn(q, kc, vct, gates_t, ovt)
    grp = tuple(h // 4 for h in range(8))
    common = dict(nt=4, nkv=1, tq=ATT_T, tk=ATT_T, head_slot=(0,) * 8, head_half=grp,
                  out_dtype=BF16, gates_t=gates_t)
    o_slc = _flash(q, k1, v1t, head_group=grp, nsel=nsel, emat=emat_t, gate_branch=1, **common)
    o_win = _banded(q, k2, v2t, nt=4, tq=ATT_T, window=NSA_WINDOW, head_half=grp, out_dtype=BF16,
                    gates_t=gates_t, gate_branch=2)
    return [o_cmp, o_slc, o_win]


def _swa_mixer(x2, bsz, seq, gain, w_in, sinks, rope_tabs):
    qw = N_HEADS * HEAD_DIM
    scale = HEAD_DIM ** -0.5 * LOG2E
    row_groups = ((0, qw, True, scale, False), (qw, LANES, True, 1.0, False))
    w = w_in[:, :qw + LANES].astype(BF16)
    wt = w_in[:, qw + LANES:].T.astype(BF16)
    q, k, vt = _proj(x2, gain, w, row_groups, (BF16, BF16), seq, wt=wt,
                     t_groups=((0, LANES, SWA_WINDOW),), t_dtypes=(BF16,), rope_tabs=rope_tabs)
    r3 = lambda a: a.reshape(bsz, seq, a.shape[-1])
    o = _banded(r3(q), r3(k), vt, nt=8, tq=SWA_WINDOW, window=SWA_WINDOW,
                head_half=tuple(h // 8 for h in range(16)), out_dtype=BF16, sinks=sinks)
    return [o]


def _fox_mixer(x2, bsz, seq, gain, w_in, b_f):
    qw = N_HEADS * HEAD_DIM
    scale = HEAD_DIM ** -0.5 * LOG2E
    w = jnp.concatenate([w_in[:, :2 * qw], _pad_cols(w_in[:, 3 * qw:], LANES)], axis=1).astype(BF16)
    wt = w_in[:, 2 * qw:3 * qw].T.astype(BF16)
    row_groups = ((0, qw, False, scale, False), (qw, qw, False, 1.0, False),
                  (2 * qw, LANES, False, 1.0, False))
    q, k, f, vt = _proj(x2, gain, w, row_groups, (BF16, BF16, F32), seq, wt=wt,
                        t_groups=((0, qw, ATT_T),), t_dtypes=(BF16,))
    r3 = lambda a: a.reshape(bsz, seq, a.shape[-1])

    bh = bsz * N_HEADS
    bhp = -(-bh // LANES) * LANES
    z = f.reshape(bsz, seq, LANES)[:, :, :N_HEADS].transpose(1, 0, 2).reshape(seq, bh)
    z = _pad_cols(z, bhp)
    bias = _pad_cols(jnp.tile(b_f, bsz).reshape(1, bh), bhp)
    c = _cumsum(z, bias)
    ck, cq = _fox_bias(c, bh)
    r4 = lambda a: a.reshape(bsz, N_HEADS, seq, LANES)
    o = _flash(r3(q), r3(k), vt, nt=4, nkv=4, tq=ATT_T, tk=ATT_T,
               head_slot=tuple(h // 2 for h in range(8)), head_half=(0, 1) * 4, out_dtype=BF16,
               ck=r4(ck), cq=r4(cq))
    return [o]


def kernel(x, ffn1_norm, ffn1_w_gu, ffn1_w_down, mix_norm, ffn2_norm, ffn2_w_gu, ffn2_w_down,
           nsa_w_in, nsa_ck_pe, nsa_ck_w1, nsa_ck_w2, nsa_cv_pe, nsa_cv_w1, nsa_cv_w2, nsa_w_out,
           swa_w_in, swa_sinks, swa_w_out, fox_w_in, fox_b_f, fox_w_out, final_norm):
    bsz, seq, _ = x.shape
    depth = ffn1_norm.shape[0]
    rope_tabs = _rope_tables(seq)
    ovt = _overlap_t(seq)
    emat_t = _expand_mat_t(seq, ATT_T)
    x2 = x.reshape(bsz * seq, D_MODEL)
    for i in range(depth):
        kind, j = i % N_MIXERS, i // N_MIXERS
        x2 = _ffn(x2, ffn1_norm[i], ffn1_w_gu, ffn1_w_down, i)
        if kind == 0:
            branches = _nsa_mixer(x2, bsz, seq, mix_norm[i], nsa_w_in[j], nsa_ck_pe[j],
                                  nsa_ck_w1[j], nsa_ck_w2[j], nsa_cv_pe[j], nsa_cv_w1[j],
                                  nsa_cv_w2[j], rope_tabs, ovt, emat_t)
            w_out = nsa_w_out
        elif kind == 1:
            branches = _swa_mixer(x2, bsz, seq, mix_norm[i], swa_w_in[j], swa_sinks[j], rope_tabs)
            w_out = swa_w_out
        else:
            branches = _fox_mixer(x2, bsz, seq, mix_norm[i], fox_w_in[j], fox_b_f[j])
            w_out = fox_w_out
        x2 = _ffn(x2, ffn2_norm[i], ffn2_w_gu, ffn2_w_down, i, branches=branches, w_out=w_out,
                  w_out_layer=j, final_gain=final_norm if i == depth - 1 else None)
    return x2.reshape(bsz, seq, D_MODEL)
```

```python
import functools

import numpy as np
import jax
import jax.numpy as jnp
from jax import lax
from jax.experimental import pallas as pl
from jax.experimental.pallas import tpu as pltpu

D_MODEL = 1024
HEAD_DIM = 64
N_HEADS = 16
D_FF = 2816
RMS_EPS = 1e-6
ROPE_THETA = 10000.0
NEG = -1e30
SEL_BIG = float(2 ** 100)
LOG2E = 1.4426950408889634

NSA_GROUPS = 4
NSA_CMP_LEN = 32
NSA_CMP_STRIDE = 16
NSA_CMP_HIDDEN = 128
NSA_SLC_LEN = 64
NSA_TOPK = 16
NSA_WINDOW = 512
NSA_FORCE_BONUS = 1e4
SWA_WINDOW = 128
N_MIXERS = 3

LANES = 128
HALF = 64
VMEM_LIMIT = 56 * 1024 * 1024

F32 = jnp.float32
BF16 = jnp.bfloat16


def _nt_dot(a, b):
    return lax.dot_general(a, b, (((1,), (1,)), ((), ())), preferred_element_type=F32)


def _dot(a, b):
    return jnp.dot(a, b, preferred_element_type=F32)


def _rms(x, g):
    ms = jnp.mean(x * x, axis=-1, keepdims=True)
    return x * lax.rsqrt(ms + RMS_EPS) * g


def _cparams(sem):
    return pltpu.CompilerParams(dimension_semantics=sem, vmem_limit_bytes=VMEM_LIMIT)


FFN_TM = 512
FFN_TF = 256
FFN_NC = 16


def _ffn_kernel(*refs, n_in, final):
    it = iter(refs)
    x_ref = next(it)
    o_refs = [next(it) for _ in range(n_in)]
    g_ref = next(it)
    wgu_ref = next(it)
    wd_ref = next(it)
    wo_ref = next(it) if n_in else None
    fg_ref = next(it) if final else None
    out_ref = next(it)
    wgu_s = next(it)
    wd_s = next(it)
    wo_s = next(it) if n_in else None

    j = pl.program_id(0)

    @pl.when(j < FFN_NC)
    def _():
        for src, dst in ((wgu_ref, wgu_s), (wd_ref, wd_s)) + (((wo_ref, wo_s),) if n_in else ()):
            rows = src.shape[0]
            dst[pl.ds(pl.multiple_of(j * rows, rows), rows), :] = src[...].astype(BF16)

    @pl.when(j >= FFN_NC)
    def _():
        x = x_ref[...]
        if n_in:
            o = o_refs[0][...].astype(F32)
            for r in o_refs[1:]:
                o = o + r[...].astype(F32)
            x = x + lax.dot_general(o.astype(BF16), wo_s[...], (((0,), (0,)), ((), ())),
                                    preferred_element_type=F32)
        hb = _rms(x, g_ref[...]).astype(BF16)
        acc = jnp.zeros(x.shape, F32)
        for f in range(D_FF // FFN_TF):
            lo = f * FFN_TF
            g = _dot(hb, wgu_s[:, lo:lo + FFN_TF])
            u = _dot(hb, wgu_s[:, D_FF + lo:D_FF + lo + FFN_TF])
            a = (g * jax.nn.sigmoid(g)) * u
            acc = acc + _dot(a.astype(BF16), wd_s[lo:lo + FFN_TF, :])
        y = x + 0.5 * acc
        if final:
            y = _rms(y, fg_ref[...])
        out_ref[...] = y


def _ffn(x2, gain, wgu, wd, layer, branches=(), w_out=None, w_out_layer=0, final_gain=None):
    n = x2.shape[0]
    n_in = len(branches)
    final = final_gain is not None
    slab = lambda lyr: (lambda j: (lyr, jnp.minimum(j, FFN_NC - 1), 0))
    row = pl.BlockSpec((FFN_TM, D_MODEL), lambda j: (jnp.maximum(j - FFN_NC, 0), 0))
    vec = pl.BlockSpec((1, D_MODEL), lambda j: (0, 0))
    nblk = branches[0].shape[2] // FFN_TM if n_in else 1
    brow = pl.BlockSpec((None, D_MODEL, FFN_TM),
                        lambda j: (jnp.maximum(j - FFN_NC, 0) // nblk, 0,
                                   jnp.maximum(j - FFN_NC, 0) % nblk))
    in_specs = [row] + [brow] * n_in + [
        vec,
        pl.BlockSpec((None, D_MODEL // FFN_NC, 2 * D_FF), slab(layer)),
        pl.BlockSpec((None, D_FF // FFN_NC, D_MODEL), slab(layer)),
    ]
    args = [x2, *branches, gain.reshape(1, D_MODEL), wgu, wd]
    scratch = [pltpu.VMEM((D_MODEL, 2 * D_FF), BF16), pltpu.VMEM((D_FF, D_MODEL), BF16)]
    if n_in:
        in_specs.append(pl.BlockSpec((None, D_MODEL // FFN_NC, D_MODEL), slab(w_out_layer)))
        args.append(w_out)
        scratch.append(pltpu.VMEM((D_MODEL, D_MODEL), BF16))
    if final:
        in_specs.append(vec)
        args.append(final_gain.reshape(1, D_MODEL))
    return pl.pallas_call(
        functools.partial(_ffn_kernel, n_in=n_in, final=final),
        grid=(FFN_NC + n // FFN_TM,),
        in_specs=in_specs,
        out_specs=row,
        out_shape=jax.ShapeDtypeStruct((n, D_MODEL), F32),
        scratch_shapes=scratch,
        compiler_params=_cparams(("arbitrary",)),
        name="ffn",
    )(*args)


PROJ_TM = 512
PROJ_CH = 256


def _rope_tile(y, cos_t, sin_t, first_half):
    rot = jnp.where(first_half, pltpu.roll(y, 96, 1), pltpu.roll(y, 32, 1))
    return y * cos_t + rot * sin_t


def _proj_kernel(*refs, row_groups, t_groups, use_rope):
    it = iter(refs)
    x_ref = next(it)
    g_ref = next(it)
    w_ref = next(it)
    wt_ref = next(it) if t_groups else None
    if use_rope:
        cos_t = next(it)[...]
        sin_t = next(it)[...]
        lane = lax.broadcasted_iota(jnp.int32, (1, LANES), 1)
        first_half = (lane % HALF) < (HALF // 2)
    o_refs = list(it)
    hb = _rms(x_ref[...], g_ref[...]).astype(BF16)
    tm = hb.shape[0]
    for (c0, width, rope, scale, planar), o_ref in zip(row_groups, o_refs):
        for t0 in range(0, width, PROJ_CH):
            ch = min(PROJ_CH, width - t0)
            y = _dot(hb, w_ref[:, c0 + t0:c0 + t0 + ch])
            for l0 in range(0, ch, LANES):
                yt = y[:, l0:l0 + LANES]
                if rope:
                    yt = _rope_tile(yt, cos_t, sin_t, first_half)
                if scale != 1.0:
                    yt = yt * scale
                if planar:
                    o_ref[(t0 + l0) // LANES] = yt.astype(o_ref.dtype)
                else:
                    o_ref[:, t0 + l0:t0 + l0 + LANES] = yt.astype(o_ref.dtype)
    for (c0, width, ck), o_ref in zip(t_groups, o_refs[len(row_groups):]):
        for t0 in range(0, width, PROJ_CH):
            ch = min(PROJ_CH, width - t0)
            yt = _nt_dot(wt_ref[c0 + t0:c0 + t0 + ch, :], hb)
            for s0 in range(0, tm, ck):
                o_ref[0, s0 // ck, t0:t0 + ch, :] = yt[:, s0:s0 + ck].astype(o_ref.dtype)


def _proj(x2, gain, w, row_groups, row_dtypes, seq, wt=None, t_groups=(), t_dtypes=(),
          rope_tabs=None):
    n = x2.shape[0]
    bsz = n // seq
    nblk = seq // PROJ_TM
    use_rope = rope_tabs is not None
    resident = dict(pipeline_mode=pl.Buffered(1))
    in_specs = [
        pl.BlockSpec((PROJ_TM, D_MODEL), lambda i: (i, 0)),
        pl.BlockSpec((1, D_MODEL), lambda i: (0, 0)),
        pl.BlockSpec(w.shape, lambda i: (0, 0), **resident),
    ]
    args = [x2, gain.reshape(1, D_MODEL), w]
    if t_groups:
        in_specs.append(pl.BlockSpec(wt.shape, lambda i: (0, 0), **resident))
        args.append(wt)
    if use_rope:
        in_specs += [pl.BlockSpec((PROJ_TM, LANES), lambda i: (i % nblk, 0))] * 2
        args += list(rope_tabs)
    out_specs, out_shape = [], []
    for (_, width, _, _, planar), dt in zip(row_groups, row_dtypes):
        if planar:
            out_specs.append(pl.BlockSpec((width // LANES, PROJ_TM, LANES), lambda i: (0, i, 0)))
            out_shape.append(jax.ShapeDtypeStruct((width // LANES, n, LANES), dt))
        else:
            out_specs.append(pl.BlockSpec((PROJ_TM, width), lambda i: (i, 0)))
            out_shape.append(jax.ShapeDtypeStruct((n, width), dt))
    for (_, width, ck), dt in zip(t_groups, t_dtypes):
        out_specs.append(pl.BlockSpec((1, PROJ_TM // ck, width, ck),
                                      lambda i: (i // nblk, i % nblk, 0, 0)))
        out_shape.append(jax.ShapeDtypeStruct((bsz, seq // ck, width, ck), dt))
    return pl.pallas_call(
        functools.partial(_proj_kernel, row_groups=row_groups, t_groups=t_groups,
                          use_rope=use_rope),
        grid=(n // PROJ_TM,),
        in_specs=in_specs,
        out_specs=out_specs,
        out_shape=out_shape,
        compiler_params=_cparams(("parallel",)),
        name="proj",
    )(*args)


def _flash_kernel(*refs, nt, t, use_sel, use_fox, gate_branch, head_slot, head_half, head_group):
    it = iter(refs)
    q_ref = next(it)
    k_ref = next(it)
    vt_ref = next(it)
    if use_sel:
        nsel_ref = next(it)
        e_ref = next(it)
    if use_fox:
        ck_ref = next(it)
        cq_ref = next(it)
    if gate_branch is not None:
        gt_ref = next(it)
    o_ref = next(it)
    qal_ref = next(it)
    m_ref = next(it)
    acc_ref = next(it)
    s_ref = next(it)
    p_ref = next(it)
    a_ref = next(it)

    hg = pl.program_id(1)
    i = pl.program_id(2)
    nh = 2 * nt
    tq = tk = t

    lane = lax.broadcasted_iota(jnp.int32, (1, LANES), 1)
    half = lane // HALF
    for tt in range(nt):
        qt = q_ref[0, :, tt * LANES:(tt + 1) * LANES].astype(F32)
        qr = pltpu.roll(qt, HALF, 1) if any(head_half[2 * tt + a] != a for a in range(2)) else None
        for a in range(2):
            h = 2 * tt + a
            src = qt if head_half[h] == a else qr
            qal_ref[h, :, :LANES] = jnp.where(half == head_half[h], src, 0.0).astype(BF16)
            if use_sel:
                qal_ref[h, :, LANES:] = nsel_ref[0, head_group[h]]
            if use_fox:
                qal_ref[h, :, LANES:] = cq_ref[0, h]

    m_ref[...] = jnp.full(m_ref.shape, NEG, F32)
    acc_ref[...] = jnp.zeros(acc_ref.shape, F32)

    row_half = lax.broadcasted_iota(jnp.int32, (LANES, 1), 0) // HALF
    q0 = i * tq
    colrow = (lax.broadcasted_iota(jnp.int32, (tk, tq), 1)
              - lax.broadcasted_iota(jnp.int32, (tk, tq), 0))

    def scores(c, h):
        k0 = pl.multiple_of(c * tk, tk)
        sl = head_slot[h]
        kc = k_ref[0, pl.ds(k0, tk), sl * LANES:(sl + 1) * LANES]
        if use_sel:
            kc = jnp.concatenate([kc, e_ref[c]], axis=1)
        if use_fox:
            kc = jnp.concatenate([kc, ck_ref[0, h, pl.ds(k0, tk), :]], axis=1)
        return _nt_dot(kc, qal_ref[h])

    def values_t(c):
        made = {}
        for h in range(nh):
            key = (head_slot[h], head_half[h])
            if key not in made:
                vt = vt_ref[0, c, key[0] * LANES:(key[0] + 1) * LANES, :]
                made[key] = jnp.where(row_half == key[1], vt, jnp.ones_like(vt))
        return [made[(head_slot[h], head_half[h])] for h in range(nh)]

    def accumulate(vth, par, h):
        acc_ref[h] = a_ref[par, h] * acc_ref[h] + _dot(vth[h], p_ref[par, h])

    for h in range(nh):
        s_ref[0, h] = scores(0, h)
    p_ref[1] = jnp.zeros(p_ref.shape[1:], BF16)
    a_ref[...] = jnp.ones(a_ref.shape, F32)

    def trip(c, par, masked):
        k0 = pl.multiple_of(c * tk, tk)
        if masked:
            mask = colrow + (q0 - k0) >= 0
        c_next = jnp.minimum(c + 1, i)
        vth = values_t(jnp.maximum(c - 1, 0))
        for h in range(nh):
            accumulate(vth, 1 - par, h)
            s_ref[1 - par, h] = scores(c_next, h)
            s = s_ref[par, h]
            if masked:
                s = jnp.where(mask, s, NEG)
            m_old = m_ref[h]
            m_new = jnp.maximum(m_old, jnp.max(s, axis=0, keepdims=True))
            a_ref[par, h] = jnp.exp2(m_old - m_new)
            p_ref[par, h] = jnp.exp2(s - m_new).astype(BF16)
            m_ref[h] = m_new

    def pair(n, carry):
        trip(2 * n, 0, False)
        trip(2 * n + 1, 1, False)
        return carry

    lax.fori_loop(0, i // 2, pair, 0)
    pl.when(i % 2 == 1)(functools.partial(trip, i - 1, 0, False))

    def last(par):
        trip(i, par, True)
        vth = values_t(i)
        for h in range(nh):
            accumulate(vth, par, h)

    pl.when(i % 2 == 0)(functools.partial(last, 0))
    pl.when(i % 2 == 1)(functools.partial(last, 1))

    for tt in range(nt):
        outs = []
        for a in range(2):
            h = 2 * tt + a
            acc = acc_ref[h]
            kh = head_half[h]
            l = acc[(1 - kh) * HALF:(1 - kh) * HALF + 1]
            out = acc[kh * HALF:(kh + 1) * HALF]
            out = out * (1.0 / l)
            if gate_branch is not None:
                gi = gate_branch * N_HEADS + hg * nh + h
                out = out * jax.nn.sigmoid(gt_ref[0, 0, pl.ds(gi, 1), :])
            outs.append(out)
        tile_t = jnp.concatenate(outs, axis=0)
        o_ref[0, tt * LANES:(tt + 1) * LANES, :] = tile_t.astype(o_ref.dtype)


def _flash(q, k, vt, *, nt, nkv, t, head_slot, head_half, out_dtype, head_group=None,
           nsel=None, emat=None, ck=None, cq=None, gates_t=None, gate_branch=None):
    bsz, seq, qw = q.shape
    n_hg = qw // (nt * LANES)
    tq = tk = t
    nq = seq // tq
    nk = seq // tk
    nh = 2 * nt
    use_sel = nsel is not None
    use_fox = ck is not None
    kw = 2 * LANES if (use_sel or use_fox) else LANES
    in_specs = [
        pl.BlockSpec((1, tq, nt * LANES), lambda b, g, i: (b, i, g)),
        pl.BlockSpec((1, seq, nkv * LANES), lambda b, g, i: (b, 0, g)),
        pl.BlockSpec((1, nk, nkv * LANES, tk), lambda b, g, i: (b, 0, g, 0)),
    ]
    args = [q, k, vt]
    if use_sel:
        ngrp = max(head_group) + 1
        in_specs += [
            pl.BlockSpec((1, ngrp, tq, LANES), lambda b, g, i: (b, g, i, 0)),
            pl.BlockSpec(emat.shape, lambda b, g, i: (0, 0, 0)),
        ]
        args += [nsel, emat]
    if use_fox:
        in_specs += [
            pl.BlockSpec((1, nh, seq, LANES), lambda b, g, i: (b, g, 0, 0)),
            pl.BlockSpec((1, nh, tq, LANES), lambda b, g, i: (b, g, i, 0)),
        ]
        args += [ck, cq]
    if gate_branch is not None:
        in_specs.append(pl.BlockSpec((1, 1, LANES, tq), lambda b, g, i: (b, i, 0, 0)))
        args.append(gates_t)
    kern = functools.partial(
        _flash_kernel, nt=nt, t=t, use_sel=use_sel, use_fox=use_fox, gate_branch=gate_branch,
        head_slot=head_slot, head_half=head_half, head_group=head_group)
    return pl.pallas_call(
        kern,
        grid=(bsz, n_hg, nq),
        in_specs=in_specs,
        out_specs=pl.BlockSpec((1, nt * LANES, tq), lambda b, g, i: (b, g, i)),
        out_shape=jax.ShapeDtypeStruct((bsz, qw, seq), out_dtype),
        scratch_shapes=[
            pltpu.VMEM((nh, tq, kw), BF16),
            pltpu.VMEM((nh, 1, tq), F32),
            pltpu.VMEM((nh, LANES, tq), F32),
            pltpu.VMEM((2, nh, tk, tq), F32),
            pltpu.VMEM((2, nh, tk, tq), BF16),
            pltpu.VMEM((2, nh, 1, tq), F32),
        ],
        compiler_params=_cparams(("parallel", "parallel", "arbitrary")),
        name="flash",
    )(*args)


def _banded_kernel(*refs, nt, tq, window, use_sink, gate_branch, head_half):
    it = iter(refs)
    q_ref = next(it)
    k_ref = next(it)
    vt_ref = next(it)
    if use_sink:
        sink_ref = next(it)
    if gate_branch is not None:
        gt_ref = next(it)
    o_ref = next(it)
    qal_ref = next(it)
    s_ref = next(it)

    hg = pl.program_id(1)
    i = pl.program_id(2)
    nh = 2 * nt
    span = window + tq
    q0 = i * tq
    k_start = pl.multiple_of(jnp.maximum(q0 - window, 0), tq)

    lane = lax.broadcasted_iota(jnp.int32, (1, LANES), 1)
    half = lane // HALF
    for tt in range(nt):
        qt = q_ref[0, :, tt * LANES:(tt + 1) * LANES].astype(F32)
        qr = pltpu.roll(qt, HALF, 1) if any(head_half[2 * tt + a] != a for a in range(2)) else None
        for a in range(2):
            h = 2 * tt + a
            src = qt if head_half[h] == a else qr
            qal_ref[h] = jnp.where(half == head_half[h], src, 0.0).astype(BF16)

    kspan = k_ref[0, pl.ds(k_start, span), :]
    for h in range(nh):
        s_ref[h] = _nt_dot(kspan, qal_ref[h])

    d = (q0 - k_start) + (lax.broadcasted_iota(jnp.int32, (span, tq), 1)
                          - lax.broadcasted_iota(jnp.int32, (span, tq), 0))
    mask = (d >= 0) & (d < window)
    c0 = k_start // tq
    vt = jnp.concatenate([vt_ref[0, c0 + j] for j in range(span // tq)], axis=1)
    row_half = lax.broadcasted_iota(jnp.int32, (LANES, 1), 0) // HALF
    vth = {kh: jnp.where(row_half == kh, vt, jnp.ones_like(vt)) for kh in set(head_half)}

    for tt in range(nt):
        outs = []
        for a in range(2):
            h = 2 * tt + a
            kh = head_half[h]
            s = jnp.where(mask, s_ref[h], NEG)
            m = jnp.max(s, axis=0, keepdims=True)
            p = jnp.exp2(s - m).astype(BF16)
            acc = _dot(vth[kh], p)
            l = acc[(1 - kh) * HALF:(1 - kh) * HALF + 1]
            out = acc[kh * HALF:(kh + 1) * HALF]
            if use_sink:
                sk = sink_ref[hg * nh + h] * LOG2E
                m2 = jnp.maximum(m, sk)
                f = jnp.exp2(m - m2)
                l = l * f + jnp.exp2(sk - m2)
                out = out * f
            out = out * (1.0 / l)
            if gate_branch is not None:
                gi = gate_branch * N_HEADS + hg * nh + h
                out = out * jax.nn.sigmoid(gt_ref[0, 0, pl.ds(gi, 1), :])
            outs.append(out)
        tile_t = jnp.concatenate(outs, axis=0)
        o_ref[0, tt * LANES:(tt + 1) * LANES, :] = tile_t.astype(o_ref.dtype)


def _banded(q, k, vt, *, nt, tq, window, head_half, out_dtype, sinks=None, gates_t=None,
            gate_branch=None):
    bsz, seq, qw = q.shape
    n_hg = qw // (nt * LANES)
    nh = 2 * nt
    use_sink = sinks is not None
    in_specs = [
        pl.BlockSpec((1, tq, nt * LANES), lambda b, g, i: (b, i, g)),
        pl.BlockSpec((1, seq, LANES), lambda b, g, i: (b, 0, g)),
        pl.BlockSpec((1, seq // tq, LANES, tq), lambda b, g, i: (b, 0, g, 0)),
    ]
    args = [q, k, vt]
    if use_sink:
        in_specs.append(pl.BlockSpec(memory_space=pltpu.SMEM))
        args.append(sinks)
    if gate_branch is not None:
        in_specs.append(pl.BlockSpec((1, 1, LANES, tq), lambda b, g, i: (b, i, 0, 0)))
        args.append(gates_t)
    kern = functools.partial(_banded_kernel, nt=nt, tq=tq, window=window, use_sink=use_sink,
                             gate_branch=gate_branch, head_half=head_half)
    return pl.pallas_call(
        kern,
        grid=(bsz, n_hg, seq // tq),
        in_specs=in_specs,
        out_specs=pl.BlockSpec((1, nt * LANES, tq), lambda b, g, i: (b, g, i)),
        out_shape=jax.ShapeDtypeStruct((bsz, qw, seq), out_dtype),
        scratch_shapes=[
            pltpu.VMEM((nh, tq, LANES), BF16),
            pltpu.VMEM((nh, window + tq, tq), F32),
        ],
        compiler_params=_cparams(("parallel", "parallel", "arbitrary")),
        name="banded",
    )(*args)


N_CHUNK16 = 128


def _compress_hidden(x_ref, pe_ref, w1_ref):
    hid = NSA_GROUPS * NSA_CMP_HIDDEN
    x = jnp.concatenate([x_ref[t, 0, pl.ds(l, N_CHUNK16, stride=NSA_CMP_STRIDE), :]
                         for l in range(NSA_CMP_STRIDE) for t in range(x_ref.shape[0])], axis=1)
    top = _dot((x + pe_ref[0:1, :]).astype(BF16), w1_ref[:, :hid])
    bot = _dot((x + pe_ref[1:2, :]).astype(BF16), w1_ref[:, hid:])
    h1 = top + pltpu.roll(bot, N_CHUNK16 - 1, 0)
    return jax.nn.gelu(h1, approximate=True).astype(BF16)


def _compress_kernel(xk_ref, xv_ref, pek_ref, pev_ref, wk1_ref, wv1_ref, wk2_ref, wv2t_ref,
                     kc_ref, vct_ref):
    kc_ref[0] = _dot(_compress_hidden(xk_ref, pek_ref, wk1_ref), wk2_ref[...]).astype(kc_ref.dtype)
    vct_ref[0] = _nt_dot(wv2t_ref[...], _compress_hidden(xv_ref, pev_ref, wv1_ref)).astype(vct_ref.dtype)


def _compress(xk, xv, pek, pev, wk1, wv1, wk2, wv2t):
    nplane, bsz, seq, _ = xk.shape
    gd = nplane * LANES
    xspec = pl.BlockSpec((nplane, 1, seq, LANES), lambda b: (0, b, 0, 0))
    full = lambda a: pl.BlockSpec(a.shape, lambda b: (0,) * a.ndim)
    return pl.pallas_call(
        _compress_kernel,
        grid=(bsz,),
        in_specs=[xspec, xspec, full(pek), full(pev), full(wk1), full(wv1), full(wk2), full(wv2t)],
        out_specs=[pl.BlockSpec((1, N_CHUNK16, gd), lambda b: (b, 0, 0)),
                   pl.BlockSpec((1, gd, N_CHUNK16), lambda b: (b, 0, 0))],
        out_shape=[jax.ShapeDtypeStruct((bsz, N_CHUNK16, gd), BF16),
                   jax.ShapeDtypeStruct((bsz, gd, N_CHUNK16), BF16)],
        compiler_params=_cparams(("parallel",)),
        name="compress",
    )(xk, xv, pek, pev, wk1, wv1, wk2, wv2t)


CMP_TQ = 256
N_SLC = 32


def _cmp_kernel(q_ref, kc_ref, vct_ref, gt_ref, ovt_ref, o_ref, nsel_ref, qal_ref):
    tq = CMP_TQ
    hg = pl.program_id(1)
    i = pl.program_id(2)
    lane = lax.broadcasted_iota(jnp.int32, (1, LANES), 1)
    half = lane // HALF
    for tt in range(4):
        qt = q_ref[0, :, tt * LANES:(tt + 1) * LANES].astype(F32)
        qr = pltpu.roll(qt, HALF, 1)
        for a in range(2):
            kh = tt // 2
            qal_ref[2 * tt + a] = jnp.where(half == kh, qt if a == kh else qr, 0.0).astype(BF16)
    kc = kc_ref[0]
    vct = vct_ref[0]
    t_row = i * tq + lax.broadcasted_iota(jnp.int32, (1, tq), 1)
    n_col = lax.broadcasted_iota(jnp.int32, (N_CHUNK16, 1), 0)
    n_cmp = (N_CHUNK16 * NSA_CMP_STRIDE - NSA_CMP_LEN) // NSA_CMP_STRIDE + 1
    valid = (n_col * NSA_CMP_STRIDE + (NSA_CMP_LEN - 1) <= t_row) & (n_col < n_cmp)

    scores = [_nt_dot(kc, qal_ref[h]) for h in range(8)]
    probs = []
    for h in range(8):
        s = jnp.where(valid, scores[h], NEG)
        m = jnp.max(s, axis=0, keepdims=True)
        e = jnp.where(valid, jnp.exp2(s - m), 0.0)
        l = jnp.sum(e, axis=0, keepdims=True)
        probs.append(e * (1.0 / jnp.where(l > 0.0, l, 1.0)))
    outs = []
    for h in range(8):
        kh = h // 4
        out = _dot(vct, probs[h].astype(BF16))[kh * HALF:(kh + 1) * HALF]
        outs.append(out * jax.nn.sigmoid(gt_ref[0, 0, pl.ds(hg * 8 + h, 1), :]))
    for tt in range(4):
        tile_t = jnp.concatenate(outs[2 * tt:2 * tt + 2], axis=0)
        o_ref[0, tt * LANES:(tt + 1) * LANES, :] = tile_t.astype(o_ref.dtype)

    ovt = ovt_ref[...]
    j = lax.broadcasted_iota(jnp.int32, (N_SLC, 1), 0)
    tb = jnp.right_shift(t_row, 6)
    forced = (j == 0) | (j == tb) | (j == tb - 1)
    for grp in range(2):
        pg = probs[4 * grp:4 * grp + 4]
        psum = (pg[0] + pg[1]) + (pg[2] + pg[3])
        p_hi = psum.astype(BF16)
        p_lo = (psum - p_hi.astype(F32)).astype(BF16)
        imp = _dot(ovt, p_hi) + _dot(ovt, p_lo)
        imp = jnp.where(j > tb, NEG, jnp.where(forced, NSA_FORCE_BONUS, imp))
        cnt = jnp.zeros((N_SLC, tq), jnp.int32)
        for jp in range(N_SLC):
            row = imp[jp:jp + 1, :]
            beats = (row > imp) | ((row == imp) & (jp < j))
            cnt = cnt + beats.astype(jnp.int32)
        sel_t = (cnt < NSA_TOPK).astype(F32)
        sel_t = jnp.concatenate([sel_t, jnp.ones((LANES - N_SLC, tq), F32)], axis=0)
        nsel_ref[0, grp] = ((sel_t.T - 1.0) * SEL_BIG).astype(nsel_ref.dtype)


def _cmp_attn(q, kc, vct, gates_t, ovt):
    bsz, seq, _ = q.shape
    tq = CMP_TQ
    gw = 4 * LANES
    return pl.pallas_call(
        _cmp_kernel,
        grid=(bsz, NSA_GROUPS // 2, seq // tq),
        in_specs=[
            pl.BlockSpec((1, tq, gw), lambda b, g, i: (b, i, g)),
            pl.BlockSpec((1, N_CHUNK16, LANES), lambda b, g, i: (b, 0, g)),
            pl.BlockSpec((1, LANES, N_CHUNK16), lambda b, g, i: (b, g, 0)),
            pl.BlockSpec((1, 1, LANES, tq), lambda b, g, i: (b, i, 0, 0)),
            pl.BlockSpec(ovt.shape, lambda b, g, i: (0, 0)),
        ],
        out_specs=[
            pl.BlockSpec((1, gw, tq), lambda b, g, i: (b, g, i)),
            pl.BlockSpec((1, 2, tq, LANES), lambda b, g, i: (b, g, i, 0)),
        ],
        out_shape=[
            jax.ShapeDtypeStruct((bsz, D_MODEL, seq), BF16),
            jax.ShapeDtypeStruct((bsz, NSA_GROUPS, seq, LANES), BF16),
        ],
        scratch_shapes=[pltpu.VMEM((8, tq, LANES), BF16)],
        compiler_params=_cparams(("parallel", "parallel", "arbitrary")),
        name="cmp_attn",
    )(q, kc, vct, gates_t, ovt)


CS_BLK = 256


def _cumsum_kernel(z_ref, b_ref, c_ref):
    seq, width = z_ref.shape
    tri = (lax.broadcasted_iota(jnp.int32, (CS_BLK, CS_BLK), 0)
           >= lax.broadcasted_iota(jnp.int32, (CS_BLK, CS_BLK), 1)).astype(BF16)
    carry = jnp.zeros((1, width), F32)
    for blk in range(seq // CS_BLK):
        x = jax.nn.log_sigmoid(z_ref[blk * CS_BLK:(blk + 1) * CS_BLK, :] + b_ref[...])
        hi = x.astype(BF16)
        r1 = x - hi.astype(F32)
        mid = r1.astype(BF16)
        lo = (r1 - mid.astype(F32)).astype(BF16)
        cs = _dot(tri, hi) + _dot(tri, mid) + _dot(tri, lo) + carry
        c_ref[blk * CS_BLK:(blk + 1) * CS_BLK, :] = cs * LOG2E
        carry = cs[CS_BLK - 1:CS_BLK, :]


def _cumsum(z, bias):
    seq, width = z.shape
    return pl.pallas_call(
        _cumsum_kernel,
        grid=(width // LANES,),
        in_specs=[pl.BlockSpec((seq, LANES), lambda j: (0, j)),
                  pl.BlockSpec((1, LANES), lambda j: (0, j))],
        out_specs=pl.BlockSpec((seq, LANES), lambda j: (0, j)),
        out_shape=jax.ShapeDtypeStruct((seq, width), F32),
        compiler_params=_cparams(("parallel",)),
        name="cumsum",
    )(z, bias)


FOX_BIAS_COLS = 8


def _fox_bias_kernel(c_ref, ck_ref, cq_ref):
    c = c_ref[...]
    lane = lax.broadcasted_iota(jnp.int32, c.shape, 1)
    for k in range(FOX_BIAS_COLS):
        li = (pl.program_id(0) * FOX_BIAS_COLS + k) % LANES
        col = jnp.sum(jnp.where(lane == li, c, 0.0), axis=1, keepdims=True)
        hi = col.astype(BF16).astype(F32)
        mid = (col - hi).astype(BF16).astype(F32)
        lo = col - hi - mid
        k_piece = jnp.where(lane == 0, hi, jnp.where(lane == 1, mid, lo))
        q_piece = jnp.where(lane == 3, hi, jnp.where(lane == 4, mid, lo))
        ck_ref[k] = jnp.where(lane < 3, k_piece, jnp.where(lane < 6, 1.0, 0.0)).astype(BF16)
        cq_ref[k] = jnp.where(lane < 3, -1.0, jnp.where(lane < 6, q_piece, 0.0)).astype(BF16)


def _fox_bias(c, n):
    seq = c.shape[0]
    ospec = pl.BlockSpec((FOX_BIAS_COLS, seq, LANES), lambda j: (j, 0, 0))
    oshape = jax.ShapeDtypeStruct((n, seq, LANES), BF16)
    return pl.pallas_call(
        _fox_bias_kernel,
        grid=(n // FOX_BIAS_COLS,),
        in_specs=[pl.BlockSpec((seq, LANES), lambda j: (0, j * FOX_BIAS_COLS // LANES))],
        out_specs=[ospec, ospec],
        out_shape=[oshape, oshape],
        compiler_params=_cparams(("parallel",)),
        name="fox_bias",
    )(c)


def _rope_tables(seq):
    inv = ROPE_THETA ** (-jnp.arange(0, HEAD_DIM, 2, dtype=F32) / HEAD_DIM)
    ang = jnp.arange(seq, dtype=F32)[:, None] * inv[None, :]
    cos, sin = jnp.cos(ang), jnp.sin(ang)
    cos_t = jnp.tile(cos, (1, LANES // (HEAD_DIM // 2)))
    sin_t = jnp.tile(jnp.concatenate([-sin, sin], axis=1), (1, LANES // HEAD_DIM))
    return cos_t, sin_t


def _overlap_t(seq):
    n_cmp = (seq - NSA_CMP_LEN) // NSA_CMP_STRIDE + 1
    n_slc = seq // NSA_SLC_LEN
    cs = np.arange(n_cmp) * NSA_CMP_STRIDE
    ce = cs + NSA_CMP_LEN
    ss = np.arange(n_slc) * NSA_SLC_LEN
    se = ss + NSA_SLC_LEN
    ov = np.clip(np.minimum(ce[:, None], se[None, :]) - np.maximum(cs[:, None], ss[None, :]), 0, None)
    ov = (ov / NSA_CMP_LEN).astype(np.float32)
    ovt = np.zeros((n_slc, LANES), np.float32)
    ovt[:, :n_cmp] = ov.T
    return jnp.asarray(ovt, dtype=BF16)


def _expand_mat_t(seq, tk):
    key = np.arange(seq)
    e = ((key // NSA_SLC_LEN)[:, None] == np.arange(LANES)[None, :]).astype(np.float32)
    return jnp.asarray(e.reshape(seq // tk, tk, LANES), dtype=BF16)


def _compress_weights(pe, w1, w2):
    g = NSA_GROUPS
    half = NSA_CMP_LEN // 2
    eye = jnp.eye(g, dtype=F32)
    w1r = w1.reshape(NSA_CMP_LEN, HEAD_DIM, NSA_CMP_HIDDEN)

    def big(part):
        return jnp.einsum('ldj,gh->lgdhj', part, eye).reshape(half * g * HEAD_DIM, g * NSA_CMP_HIDDEN)

    w1big = jnp.concatenate([big(w1r[:half]), big(w1r[half:])], axis=1).astype(BF16)
    w2big = jnp.einsum('jd,gh->gjhd', w2, eye).reshape(g * NSA_CMP_HIDDEN, g * HEAD_DIM).astype(BF16)

    def pebig(part):
        return jnp.broadcast_to(part[:, None, :], (half, g, HEAD_DIM)).reshape(-1)

    pe2 = jnp.stack([pebig(pe[:half]), pebig(pe[half:])], axis=0)
    pe2 = jnp.concatenate([pe2, jnp.zeros((6, pe2.shape[1]), F32)], axis=0)
    return pe2, w1big, w2big


def _pad_cols(w, width):
    return jnp.concatenate([w, jnp.zeros((w.shape[0], width - w.shape[1]), w.dtype)], axis=1)


ATT_T = 256


def _nsa_mixer(x2, bsz, seq, gain, w_in, ck_pe, ck_w1, ck_w2, cv_pe, cv_w1, cv_w2,
               rope_tabs, ovt, emat_t):
    qw = N_HEADS * HEAD_DIM
    gd = NSA_GROUPS * HEAD_DIM

    def kvcols(c, s):
        lo = qw + (c * 2 + s) * gd
        return w_in[:, lo:lo + gd]

    w = jnp.concatenate([w_in[:, :qw]] + [kvcols(c, 0) for c in range(3)] + [kvcols(0, 1)],
                        axis=1).astype(BF16)
    wt = jnp.concatenate([kvcols(1, 1), kvcols(2, 1), _pad_cols(w_in[:, qw + 6 * gd:], LANES)],
                         axis=1).T.astype(BF16)
    scale = HEAD_DIM ** -0.5 * LOG2E
    row_groups = ((0, qw, True, scale, False), (qw, gd, True, 1.0, True),
                  (qw + gd, gd, True, 1.0, False), (qw + 2 * gd, gd, True, 1.0, False),
                  (qw + 3 * gd, gd, False, 1.0, True))
    t_groups = ((0, gd, ATT_T), (gd, gd, ATT_T), (2 * gd, LANES, ATT_T))
    q, k0, k1, k2, v0, v1t, v2t, gates_t = _proj(
        x2, gain, w, row_groups, (BF16, F32, BF16, BF16, F32), seq, wt=wt, t_groups=t_groups,
        t_dtypes=(BF16, BF16, F32), rope_tabs=rope_tabs)
    r3 = lambda a: a.reshape(bsz, seq, a.shape[-1])
    q, k1, k2 = map(r3, (q, k1, k2))

    planes = lambda a: a.reshape(a.shape[0], bsz, seq, LANES)
    xk, xv = planes(k0), planes(v0)
    pek, wk1, wk2 = _compress_weights(ck_pe, ck_w1, ck_w2)
    pev, wv1, wv2 = _compress_weights(cv_pe, cv_w1, cv_w2)
    kc, vct = _compress(xk, xv, pek, pev, wk1, wv1, wk2, wv2.T)

    o_cmp, nsel = _cmp_attn(q, kc, vct, gates_t, ovt)
    grp = tuple(h // 4 for h in range(8))
    o_slc = _flash(q, k1, v1t, nt=4, nkv=1, t=ATT_T, head_slot=(0,) * 8, head_half=grp,
                   out_dtype=BF16, head_group=grp, nsel=nsel, emat=emat_t, gates_t=gates_t,
                   gate_branch=1)
    o_win = _banded(q, k2, v2t, nt=4, tq=ATT_T, window=NSA_WINDOW, head_half=grp, out_dtype=BF16,
                    gates_t=gates_t, gate_branch=2)
    return [o_cmp, o_slc, o_win]


def _swa_mixer(x2, bsz, seq, gain, w_in, sinks, rope_tabs):
    qw = N_HEADS * HEAD_DIM
    scale = HEAD_DIM ** -0.5 * LOG2E
    row_groups = ((0, qw, True, scale, False), (qw, LANES, True, 1.0, False))
    w = w_in[:, :qw + LANES].astype(BF16)
    wt = w_in[:, qw + LANES:].T.astype(BF16)
    q, k, vt = _proj(x2, gain, w, row_groups, (BF16, BF16), seq, wt=wt,
                     t_groups=((0, LANES, SWA_WINDOW),), t_dtypes=(BF16,), rope_tabs=rope_tabs)
    r3 = lambda a: a.reshape(bsz, seq, a.shape[-1])
    o = _banded(r3(q), r3(k), vt, nt=8, tq=SWA_WINDOW, window=SWA_WINDOW,
                head_half=tuple(h // 8 for h in range(16)), out_dtype=BF16, sinks=sinks)
    return [o]


def _fox_mixer(x2, bsz, seq, gain, w_in, b_f):
    qw = N_HEADS * HEAD_DIM
    scale = HEAD_DIM ** -0.5 * LOG2E
    w = jnp.concatenate([w_in[:, :2 * qw], _pad_cols(w_in[:, 3 * qw:], LANES)], axis=1).astype(BF16)
    wt = w_in[:, 2 * qw:3 * qw].T.astype(BF16)
    row_groups = ((0, qw, False, scale, False), (qw, qw, False, 1.0, False),
                  (2 * qw, LANES, False, 1.0, False))
    q, k, f, vt = _proj(x2, gain, w, row_groups, (BF16, BF16, F32), seq, wt=wt,
                        t_groups=((0, qw, ATT_T),), t_dtypes=(BF16,))
    r3 = lambda a: a.reshape(bsz, seq, a.shape[-1])

    bh = bsz * N_HEADS
    bhp = -(-bh // LANES) * LANES
    z = f.reshape(bsz, seq, LANES)[:, :, :N_HEADS].transpose(1, 0, 2).reshape(seq, bh)
    z = _pad_cols(z, bhp)
    bias = _pad_cols(jnp.tile(b_f, bsz).reshape(1, bh), bhp)
    c = _cumsum(z, bias)
    ck, cq = _fox_bias(c, bh)
    r4 = lambda a: a.reshape(bsz, N_HEADS, seq, LANES)
    o = _flash(r3(q), r3(k), vt, nt=4, nkv=4, t=ATT_T,
               head_slot=tuple(h // 2 for h in range(8)), head_half=(0, 1) * 4, out_dtype=BF16,
               ck=r4(ck), cq=r4(cq))
    return [o]


def kernel(x, ffn1_norm, ffn1_w_gu, ffn1_w_down, mix_norm, ffn2_norm, ffn2_w_gu, ffn2_w_down,
           nsa_w_in, nsa_ck_pe, nsa_ck_w1, nsa_ck_w2, nsa_cv_pe, nsa_cv_w1, nsa_cv_w2, nsa_w_out,
           swa_w_in, swa_sinks, swa_w_out, fox_w_in, fox_b_f, fox_w_out, final_norm):
    bsz, seq, _ = x.shape
    depth = ffn1_norm.shape[0]
    rope_tabs = _rope_tables(seq)
    ovt = _overlap_t(seq)
    emat_t = _expand_mat_t(seq, ATT_T)
    x2 = x.reshape(bsz * seq, D_MODEL)
    for i in range(depth):
        kind, j = i % N_MIXERS, i // N_MIXERS
        x2 = _ffn(x2, ffn1_norm[i], ffn1_w_gu, ffn1_w_down, i)
        if kind == 0:
            branches = _nsa_mixer(x2, bsz, seq, mix_norm[i], nsa_w_in[j], nsa_ck_pe[j],
                                  nsa_ck_w1[j], nsa_ck_w2[j], nsa_cv_pe[j], nsa_cv_w1[j],
                                  nsa_cv_w2[j], rope_tabs, ovt, emat_t)
            w_out = nsa_w_out
        elif kind == 1:
            branches = _swa_mixer(x2, bsz, seq, mix_norm[i], swa_w_in[j], swa_sinks[j], rope_tabs)
            w_out = swa_w_out
        else:
            branches = _fox_mixer(x2, bsz, seq, mix_norm[i], fox_w_in[j], fox_b_f[j])
            w_out = fox_w_out
        x2 = _ffn(x2, ffn2_norm[i], ffn2_w_gu, ffn2_w_down, i, branches=branches, w_out=w_out,
                  w_out_layer=j, final_gain=final_norm if i == depth - 1 else None)
    return x2.reshape(bsz, seq, D_MODEL)
```

```python
import functools

import numpy as np
import jax
import jax.numpy as jnp
from jax import lax
from jax.experimental import pallas as pl
from jax.experimental.pallas import tpu as pltpu

D_MODEL = 1024
HEAD_DIM = 64
N_HEADS = 16
D_FF = 2816
RMS_EPS = 1e-6
ROPE_THETA = 10000.0
NEG = -1e30
SEL_BIG = float(2 ** 100)
LOG2E = 1.4426950408889634

NSA_GROUPS = 4
NSA_CMP_LEN = 32
NSA_CMP_STRIDE = 16
NSA_CMP_HIDDEN = 128
NSA_SLC_LEN = 64
NSA_TOPK = 16
NSA_WINDOW = 512
NSA_FORCE_BONUS = 1e4
SWA_WINDOW = 128
N_MIXERS = 3

LANES = 128
HALF = 64
VMEM_LIMIT = 56 * 1024 * 1024

F32 = jnp.float32
BF16 = jnp.bfloat16


def _nt_dot(a, b):
    return lax.dot_general(a, b, (((1,), (1,)), ((), ())), preferred_element_type=F32)


def _dot(a, b):
    return jnp.dot(a, b, preferred_element_type=F32)


def _rms(x, g):
    ms = jnp.mean(x * x, axis=-1, keepdims=True)
    return x * lax.rsqrt(ms + RMS_EPS) * g


def _cparams(sem):
    return pltpu.CompilerParams(dimension_semantics=sem, vmem_limit_bytes=VMEM_LIMIT)


FFN_TM = 512
FFN_TF = 256
FFN_NC = 8


def _ffn_kernel(*refs, n_in, final):
    it = iter(refs)
    x_ref = next(it)
    o_refs = [next(it) for _ in range(n_in)]
    g_ref = next(it)
    wgu_ref = next(it)
    wd_ref = next(it)
    wo_ref = next(it) if n_in else None
    fg_ref = next(it) if final else None
    out_ref = next(it)
    wgu_s = next(it)
    wd_s = next(it)
    wo_s = next(it) if n_in else None

    j = pl.program_id(0)

    @pl.when(j < FFN_NC)
    def _():
        for src, dst in ((wgu_ref, wgu_s), (wd_ref, wd_s)) + (((wo_ref, wo_s),) if n_in else ()):
            rows = src.shape[0]
            dst[pl.ds(pl.multiple_of(j * rows, rows), rows), :] = src[...].astype(BF16)

    @pl.when(j >= FFN_NC)
    def _():
        x = x_ref[...]
        if n_in:
            o = o_refs[0][...].astype(F32)
            for r in o_refs[1:]:
                o = o + r[...].astype(F32)
            x = x + lax.dot_general(o.astype(BF16), wo_s[...], (((0,), (0,)), ((), ())),
                                    preferred_element_type=F32)
        hb = _rms(x, g_ref[...]).astype(BF16)
        acc = jnp.zeros(x.shape, F32)
        for f in range(D_FF // FFN_TF):
            lo = f * FFN_TF
            g = _dot(hb, wgu_s[:, lo:lo + FFN_TF])
            u = _dot(hb, wgu_s[:, D_FF + lo:D_FF + lo + FFN_TF])
            a = (g * jax.nn.sigmoid(g)) * u
            acc = acc + _dot(a.astype(BF16), wd_s[lo:lo + FFN_TF, :])
        y = x + 0.5 * acc
        if final:
            y = _rms(y, fg_ref[...])
        out_ref[...] = y


def _ffn(x2, gain, wgu, wd, layer, branches=(), w_out=None, w_out_layer=0, final_gain=None):
    n = x2.shape[0]
    n_in = len(branches)
    final = final_gain is not None
    slab = lambda lyr: (lambda j: (lyr, jnp.minimum(j, FFN_NC - 1), 0))
    row = pl.BlockSpec((FFN_TM, D_MODEL), lambda j: (jnp.maximum(j - FFN_NC, 0), 0))
    vec = pl.BlockSpec((1, D_MODEL), lambda j: (0, 0))
    nblk = branches[0].shape[2] // FFN_TM if n_in else 1
    brow = pl.BlockSpec((None, D_MODEL, FFN_TM),
                        lambda j: (jnp.maximum(j - FFN_NC, 0) // nblk, 0,
                                   jnp.maximum(j - FFN_NC, 0) % nblk))
    in_specs = [row] + [brow] * n_in + [
        vec,
        pl.BlockSpec((None, D_MODEL // FFN_NC, 2 * D_FF), slab(layer)),
        pl.BlockSpec((None, D_FF // FFN_NC, D_MODEL), slab(layer)),
    ]
    args = [x2, *branches, gain.reshape(1, D_MODEL), wgu, wd]
    scratch = [pltpu.VMEM((D_MODEL, 2 * D_FF), BF16), pltpu.VMEM((D_FF, D_MODEL), BF16)]
    if n_in:
        in_specs.append(pl.BlockSpec((None, D_MODEL // FFN_NC, D_MODEL), slab(w_out_layer)))
        args.append(w_out)
        scratch.append(pltpu.VMEM((D_MODEL, D_MODEL), BF16))
    if final:
        in_specs.append(vec)
        args.append(final_gain.reshape(1, D_MODEL))
    return pl.pallas_call(
        functools.partial(_ffn_kernel, n_in=n_in, final=final),
        grid=(FFN_NC + n // FFN_TM,),
        in_specs=in_specs,
        out_specs=row,
        out_shape=jax.ShapeDtypeStruct((n, D_MODEL), F32),
        scratch_shapes=scratch,
        compiler_params=_cparams(("arbitrary",)),
        name="ffn",
    )(*args)


PROJ_TM = 512
PROJ_CH = 256


def _rope_tile(y, cos_t, sin_t, first_half):
    rot = jnp.where(first_half, pltpu.roll(y, 96, 1), pltpu.roll(y, 32, 1))
    return y * cos_t + rot * sin_t


def _proj_kernel(*refs, row_groups, t_groups, use_rope):
    it = iter(refs)
    x_ref = next(it)
    g_ref = next(it)
    w_ref = next(it)
    wt_ref = next(it) if t_groups else None
    if use_rope:
        cos_t = next(it)[...]
        sin_t = next(it)[...]
        lane = lax.broadcasted_iota(jnp.int32, (1, LANES), 1)
        first_half = (lane % HALF) < (HALF // 2)
    o_refs = list(it)
    hb = _rms(x_ref[...], g_ref[...]).astype(BF16)
    tm = hb.shape[0]
    for (c0, width, rope, scale, planar), o_ref in zip(row_groups, o_refs):
        for t0 in range(0, width, PROJ_CH):
            ch = min(PROJ_CH, width - t0)
            y = _dot(hb, w_ref[:, c0 + t0:c0 + t0 + ch])
            for l0 in range(0, ch, LANES):
                yt = y[:, l0:l0 + LANES]
                if rope:
                    yt = _rope_tile(yt, cos_t, sin_t, first_half)
                if scale != 1.0:
                    yt = yt * scale
                if planar:
                    o_ref[(t0 + l0) // LANES] = yt.astype(o_ref.dtype)
                else:
                    o_ref[:, t0 + l0:t0 + l0 + LANES] = yt.astype(o_ref.dtype)
    for (c0, width, ck), o_ref in zip(t_groups, o_refs[len(row_groups):]):
        for t0 in range(0, width, PROJ_CH):
            ch = min(PROJ_CH, width - t0)
            yt = _nt_dot(wt_ref[c0 + t0:c0 + t0 + ch, :], hb)
            for s0 in range(0, tm, ck):
                o_ref[0, s0 // ck, t0:t0 + ch, :] = yt[:, s0:s0 + ck].astype(o_ref.dtype)


def _proj(x2, gain, w, row_groups, row_dtypes, seq, wt=None, t_groups=(), t_dtypes=(),
          rope_tabs=None):
    n = x2.shape[0]
    bsz = n // seq
    nblk = seq // PROJ_TM
    use_rope = rope_tabs is not None
    resident = dict(pipeline_mode=pl.Buffered(1))
    in_specs = [
        pl.BlockSpec((PROJ_TM, D_MODEL), lambda i: (i, 0)),
        pl.BlockSpec((1, D_MODEL), lambda i: (0, 0)),
        pl.BlockSpec(w.shape, lambda i: (0, 0), **resident),
    ]
    args = [x2, gain.reshape(1, D_MODEL), w]
    if t_groups:
        in_specs.append(pl.BlockSpec(wt.shape, lambda i: (0, 0), **resident))
        args.append(wt)
    if use_rope:
        in_specs += [pl.BlockSpec((PROJ_TM, LANES), lambda i: (i % nblk, 0))] * 2
        args += list(rope_tabs)
    out_specs, out_shape = [], []
    for (_, width, _, _, planar), dt in zip(row_groups, row_dtypes):
        if planar:
            out_specs.append(pl.BlockSpec((width // LANES, PROJ_TM, LANES), lambda i: (0, i, 0)))
            out_shape.append(jax.ShapeDtypeStruct((width // LANES, n, LANES), dt))
        else:
            out_specs.append(pl.BlockSpec((PROJ_TM, width), lambda i: (i, 0)))
            out_shape.append(jax.ShapeDtypeStruct((n, width), dt))
    for (_, width, ck), dt in zip(t_groups, t_dtypes):
        out_specs.append(pl.BlockSpec((1, PROJ_TM // ck, width, ck),
                                      lambda i: (i // nblk, i % nblk, 0, 0)))
        out_shape.append(jax.ShapeDtypeStruct((bsz, seq // ck, width, ck), dt))
    return pl.pallas_call(
        functools.partial(_proj_kernel, row_groups=row_groups, t_groups=t_groups,
                          use_rope=use_rope),
        grid=(n // PROJ_TM,),
        in_specs=in_specs,
        out_specs=out_specs,
        out_shape=out_shape,
        compiler_params=_cparams(("parallel",)),
        name="proj",
    )(*args)


def _flash_kernel(*refs, nt, t, use_sel, use_fox, gate_branch, head_slot, head_half, head_group):
    it = iter(refs)
    q_ref = next(it)
    k_ref = next(it)
    vt_ref = next(it)
    if use_sel:
        nsel_ref = next(it)
        e_ref = next(it)
    if use_fox:
        ck_ref = next(it)
        cq_ref = next(it)
    if gate_branch is not None:
        gt_ref = next(it)
    o_ref = next(it)
    qal_ref = next(it)
    m_ref = next(it)
    acc_ref = next(it)
    s_ref = next(it)
    p_ref = next(it)
    a_ref = next(it)

    hg = pl.program_id(1)
    i = pl.program_id(2)
    nh = 2 * nt
    tq = tk = t

    lane = lax.broadcasted_iota(jnp.int32, (1, LANES), 1)
    half = lane // HALF
    for tt in range(nt):
        qt = q_ref[0, :, tt * LANES:(tt + 1) * LANES].astype(F32)
        qr = pltpu.roll(qt, HALF, 1) if any(head_half[2 * tt + a] != a for a in range(2)) else None
        for a in range(2):
            h = 2 * tt + a
            src = qt if head_half[h] == a else qr
            qal_ref[h, :, :LANES] = jnp.where(half == head_half[h], src, 0.0).astype(BF16)
            if use_sel:
                qal_ref[h, :, LANES:] = nsel_ref[0, head_group[h]]
            if use_fox:
                qal_ref[h, :, LANES:] = cq_ref[0, h]

    m_ref[...] = jnp.full(m_ref.shape, NEG, F32)
    acc_ref[...] = jnp.zeros(acc_ref.shape, F32)

    row_half = lax.broadcasted_iota(jnp.int32, (LANES, 1), 0) // HALF
    q0 = i * tq
    colrow = (lax.broadcasted_iota(jnp.int32, (tk, tq), 1)
              - lax.broadcasted_iota(jnp.int32, (tk, tq), 0))

    def scores(c, h):
        k0 = pl.multiple_of(c * tk, tk)
        sl = head_slot[h]
        kc = k_ref[0, pl.ds(k0, tk), sl * LANES:(sl + 1) * LANES]
        if use_sel:
            kc = jnp.concatenate([kc, e_ref[c]], axis=1)
        if use_fox:
            kc = jnp.concatenate([kc, ck_ref[0, h, pl.ds(k0, tk), :]], axis=1)
        return _nt_dot(kc, qal_ref[h])

    def values_t(c):
        made = {}
        for h in range(nh):
            key = (head_slot[h], head_half[h])
            if key not in made:
                vt = vt_ref[0, c, key[0] * LANES:(key[0] + 1) * LANES, :]
                made[key] = jnp.where(row_half == key[1], vt, jnp.ones_like(vt))
        return [made[(head_slot[h], head_half[h])] for h in range(nh)]

    def accumulate(vth, par, h):
        acc_ref[h] = a_ref[par, h] * acc_ref[h] + _dot(vth[h], p_ref[par, h])

    for h in range(nh):
        s_ref[0, h] = scores(0, h)
    p_ref[1] = jnp.zeros(p_ref.shape[1:], BF16)
    a_ref[...] = jnp.ones(a_ref.shape, F32)

    def trip(c, par, masked):
        k0 = pl.multiple_of(c * tk, tk)
        if masked:
            mask = colrow + (q0 - k0) >= 0
        c_next = jnp.minimum(c + 1, i)
        vth = values_t(jnp.maximum(c - 1, 0))
        for h in range(nh):
            accumulate(vth, 1 - par, h)
            s_ref[1 - par, h] = scores(c_next, h)
            s = s_ref[par, h]
            if masked:
                s = jnp.where(mask, s, NEG)
            m_old = m_ref[h]
            m_new = jnp.maximum(m_old, jnp.max(s, axis=0, keepdims=True))
            a_ref[par, h] = jnp.exp2(m_old - m_new)
            p_ref[par, h] = jnp.exp2(s - m_new).astype(BF16)
            m_ref[h] = m_new

    def pair(n, carry):
        trip(2 * n, 0, False)
        trip(2 * n + 1, 1, False)
        return carry

    lax.fori_loop(0, i // 2, pair, 0)
    pl.when(i % 2 == 1)(functools.partial(trip, i - 1, 0, False))

    def last(par):
        trip(i, par, True)
        vth = values_t(i)
        for h in range(nh):
            accumulate(vth, par, h)

    pl.when(i % 2 == 0)(functools.partial(last, 0))
    pl.when(i % 2 == 1)(functools.partial(last, 1))

    for tt in range(nt):
        outs = []
        for a in range(2):
            h = 2 * tt + a
            acc = acc_ref[h]
            kh = head_half[h]
            l = acc[(1 - kh) * HALF:(1 - kh) * HALF + 1]
            out = acc[kh * HALF:(kh + 1) * HALF]
            out = out * (1.0 / l)
            if gate_branch is not None:
                gi = gate_branch * N_HEADS + hg * nh + h
                out = out * jax.nn.sigmoid(gt_ref[0, 0, pl.ds(gi, 1), :])
            outs.append(out)
        tile_t = jnp.concatenate(outs, axis=0)
        o_ref[0, tt * LANES:(tt + 1) * LANES, :] = tile_t.astype(o_ref.dtype)


def _flash(q, k, vt, *, nt, nkv, t, head_slot, head_half, out_dtype, head_group=None,
           nsel=None, emat=None, ck=None, cq=None, gates_t=None, gate_branch=None):
    bsz, seq, qw = q.shape
    n_hg = qw // (nt * LANES)
    tq = tk = t
    nq = seq // tq
    nk = seq // tk
    nh = 2 * nt
    use_sel = nsel is not None
    use_fox = ck is not None
    kw = 2 * LANES if (use_sel or use_fox) else LANES
    in_specs = [
        pl.BlockSpec((1, tq, nt * LANES), lambda b, g, i: (b, i, g)),
        pl.BlockSpec((1, seq, nkv * LANES), lambda b, g, i: (b, 0, g)),
        pl.BlockSpec((1, nk, nkv * LANES, tk), lambda b, g, i: (b, 0, g, 0)),
    ]
    args = [q, k, vt]
    if use_sel:
        ngrp = max(head_group) + 1
        in_specs += [
            pl.BlockSpec((1, ngrp, tq, LANES), lambda b, g, i: (b, g, i, 0)),
            pl.BlockSpec(emat.shape, lambda b, g, i: (0, 0, 0)),
        ]
        args += [nsel, emat]
    if use_fox:
        in_specs += [
            pl.BlockSpec((1, nh, seq, LANES), lambda b, g, i: (b, g, 0, 0)),
            pl.BlockSpec((1, nh, tq, LANES), lambda b, g, i: (b, g, i, 0)),
        ]
        args += [ck, cq]
    if gate_branch is not None:
        in_specs.append(pl.BlockSpec((1, 1, LANES, tq), lambda b, g, i: (b, i, 0, 0)))
        args.append(gates_t)
    kern = functools.partial(
        _flash_kernel, nt=nt, t=t, use_sel=use_sel, use_fox=use_fox, gate_branch=gate_branch,
        head_slot=head_slot, head_half=head_half, head_group=head_group)
    return pl.pallas_call(
        kern,
        grid=(bsz, n_hg, nq),
        in_specs=in_specs,
        out_specs=pl.BlockSpec((1, nt * LANES, tq), lambda b, g, i: (b, g, i)),
        out_shape=jax.ShapeDtypeStruct((bsz, qw, seq), out_dtype),
        scratch_shapes=[
            pltpu.VMEM((nh, tq, kw), BF16),
            pltpu.VMEM((nh, 1, tq), F32),
            pltpu.VMEM((nh, LANES, tq), F32),
            pltpu.VMEM((2, nh, tk, tq), F32),
            pltpu.VMEM((2, nh, tk, tq), BF16),
            pltpu.VMEM((2, nh, 1, tq), F32),
        ],
        compiler_params=_cparams(("parallel", "parallel", "arbitrary")),
        name="flash",
    )(*args)


def _banded_kernel(*refs, nt, tq, window, use_sink, gate_branch, head_half):
    it = iter(refs)
    q_ref = next(it)
    k_ref = next(it)
    vt_ref = next(it)
    if use_sink:
        sink_ref = next(it)
    if gate_branch is not None:
        gt_ref = next(it)
    o_ref = next(it)
    qal_ref = next(it)
    s_ref = next(it)

    hg = pl.program_id(1)
    i = pl.program_id(2)
    nh = 2 * nt
    span = window + tq
    q0 = i * tq
    k_start = pl.multiple_of(jnp.maximum(q0 - window, 0), tq)

    lane = lax.broadcasted_iota(jnp.int32, (1, LANES), 1)
    half = lane // HALF
    for tt in range(nt):
        qt = q_ref[0, :, tt * LANES:(tt + 1) * LANES].astype(F32)
        qr = pltpu.roll(qt, HALF, 1) if any(head_half[2 * tt + a] != a for a in range(2)) else None
        for a in range(2):
            h = 2 * tt + a
            src = qt if head_half[h] == a else qr
            qal_ref[h] = jnp.where(half == head_half[h], src, 0.0).astype(BF16)

    kspan = k_ref[0, pl.ds(k_start, span), :]
    for h in range(nh):
        s_ref[h] = _nt_dot(kspan, qal_ref[h])

    d = (q0 - k_start) + (lax.broadcasted_iota(jnp.int32, (span, tq), 1)
                          - lax.broadcasted_iota(jnp.int32, (span, tq), 0))
    mask = (d >= 0) & (d < window)
    c0 = k_start // tq
    vt = jnp.concatenate([vt_ref[0, c0 + j] for j in range(span // tq)], axis=1)
    row_half = lax.broadcasted_iota(jnp.int32, (LANES, 1), 0) // HALF
    vth = {kh: jnp.where(row_half == kh, vt, jnp.ones_like(vt)) for kh in set(head_half)}

    for tt in range(nt):
        outs = []
        for a in range(2):
            h = 2 * tt + a
            kh = head_half[h]
            s = jnp.where(mask, s_ref[h], NEG)
            m = jnp.max(s, axis=0, keepdims=True)
            p = jnp.exp2(s - m).astype(BF16)
            acc = _dot(vth[kh], p)
            l = acc[(1 - kh) * HALF:(1 - kh) * HALF + 1]
            out = acc[kh * HALF:(kh + 1) * HALF]
            if use_sink:
                sk = sink_ref[hg * nh + h] * LOG2E
                m2 = jnp.maximum(m, sk)
                f = jnp.exp2(m - m2)
                l = l * f + jnp.exp2(sk - m2)
                out = out * f
            out = out * (1.0 / l)
            if gate_branch is not None:
                gi = gate_branch * N_HEADS + hg * nh + h
                out = out * jax.nn.sigmoid(gt_ref[0, 0, pl.ds(gi, 1), :])
            outs.append(out)
        tile_t = jnp.concatenate(outs, axis=0)
        o_ref[0, tt * LANES:(tt + 1) * LANES, :] = tile_t.astype(o_ref.dtype)


def _banded(q, k, vt, *, nt, tq, window, head_half, out_dtype, sinks=None, gates_t=None,
            gate_branch=None):
    bsz, seq, qw = q.shape
    n_hg = qw // (nt * LANES)
    nh = 2 * nt
    use_sink = sinks is not None
    in_specs = [
        pl.BlockSpec((1, tq, nt * LANES), lambda b, g, i: (b, i, g)),
        pl.BlockSpec((1, seq, LANES), lambda b, g, i: (b, 0, g)),
        pl.BlockSpec((1, seq // tq, LANES, tq), lambda b, g, i: (b, 0, g, 0)),
    ]
    args = [q, k, vt]
    if use_sink:
        in_specs.append(pl.BlockSpec(memory_space=pltpu.SMEM))
        args.append(sinks)
    if gate_branch is not None:
        in_specs.append(pl.BlockSpec((1, 1, LANES, tq), lambda b, g, i: (b, i, 0, 0)))
        args.append(gates_t)
    kern = functools.partial(_banded_kernel, nt=nt, tq=tq, window=window, use_sink=use_sink,
                             gate_branch=gate_branch, head_half=head_half)
    return pl.pallas_call(
        kern,
        grid=(bsz, n_hg, seq // tq),
        in_specs=in_specs,
        out_specs=pl.BlockSpec((1, nt * LANES, tq), lambda b, g, i: (b, g, i)),
        out_shape=jax.ShapeDtypeStruct((bsz, qw, seq), out_dtype),
        scratch_shapes=[
            pltpu.VMEM((nh, tq, LANES), BF16),
            pltpu.VMEM((nh, window + tq, tq), F32),
        ],
        compiler_params=_cparams(("parallel", "parallel", "arbitrary")),
        name="banded",
    )(*args)


N_CHUNK16 = 128


def _compress_hidden(x_ref, pe_ref, w1_ref):
    hid = NSA_GROUPS * NSA_CMP_HIDDEN
    x = jnp.concatenate([x_ref[t, 0, pl.ds(l, N_CHUNK16, stride=NSA_CMP_STRIDE), :]
                         for l in range(NSA_CMP_STRIDE) for t in range(x_ref.shape[0])], axis=1)
    top = _dot((x + pe_ref[0:1, :]).astype(BF16), w1_ref[:, :hid])
    bot = _dot((x + pe_ref[1:2, :]).astype(BF16), w1_ref[:, hid:])
    h1 = top + pltpu.roll(bot, N_CHUNK16 - 1, 0)
    return jax.nn.gelu(h1, approximate=True).astype(BF16)


def _compress_kernel(xk_ref, xv_ref, pek_ref, pev_ref, wk1_ref, wv1_ref, wk2_ref, wv2t_ref,
                     kc_ref, vct_ref):
    kc_ref[0] = _dot(_compress_hidden(xk_ref, pek_ref, wk1_ref), wk2_ref[...]).astype(kc_ref.dtype)
    vct_ref[0] = _nt_dot(wv2t_ref[...], _compress_hidden(xv_ref, pev_ref, wv1_ref)).astype(vct_ref.dtype)


def _compress(xk, xv, pek, pev, wk1, wv1, wk2, wv2t):
    nplane, bsz, seq, _ = xk.shape
    gd = nplane * LANES
    xspec = pl.BlockSpec((nplane, 1, seq, LANES), lambda b: (0, b, 0, 0))
    full = lambda a: pl.BlockSpec(a.shape, lambda b: (0,) * a.ndim)
    return pl.pallas_call(
        _compress_kernel,
        grid=(bsz,),
        in_specs=[xspec, xspec, full(pek), full(pev), full(wk1), full(wv1), full(wk2), full(wv2t)],
        out_specs=[pl.BlockSpec((1, N_CHUNK16, gd), lambda b: (b, 0, 0)),
                   pl.BlockSpec((1, gd, N_CHUNK16), lambda b: (b, 0, 0))],
        out_shape=[jax.ShapeDtypeStruct((bsz, N_CHUNK16, gd), BF16),
                   jax.ShapeDtypeStruct((bsz, gd, N_CHUNK16), BF16)],
        compiler_params=_cparams(("parallel",)),
        name="compress",
    )(xk, xv, pek, pev, wk1, wv1, wk2, wv2t)


CMP_TQ = 256
N_SLC = 32


def _cmp_kernel(q_ref, kc_ref, vct_ref, gt_ref, ovt_ref, o_ref, nsel_ref, qal_ref):
    tq = CMP_TQ
    hg = pl.program_id(1)
    i = pl.program_id(2)
    lane = lax.broadcasted_iota(jnp.int32, (1, LANES), 1)
    half = lane // HALF
    for tt in range(4):
        qt = q_ref[0, :, tt * LANES:(tt + 1) * LANES].astype(F32)
        qr = pltpu.roll(qt, HALF, 1)
        for a in range(2):
            kh = tt // 2
            qal_ref[2 * tt + a] = jnp.where(half == kh, qt if a == kh else qr, 0.0).astype(BF16)
    kc = kc_ref[0]
    vct = vct_ref[0]
    t_row = i * tq + lax.broadcasted_iota(jnp.int32, (1, tq), 1)
    n_col = lax.broadcasted_iota(jnp.int32, (N_CHUNK16, 1), 0)
    n_cmp = (N_CHUNK16 * NSA_CMP_STRIDE - NSA_CMP_LEN) // NSA_CMP_STRIDE + 1
    valid = (n_col * NSA_CMP_STRIDE + (NSA_CMP_LEN - 1) <= t_row) & (n_col < n_cmp)

    scores = [_nt_dot(kc, qal_ref[h]) for h in range(8)]
    probs = []
    for h in range(8):
        s = jnp.where(valid, scores[h], NEG)
        m = jnp.max(s, axis=0, keepdims=True)
        e = jnp.where(valid, jnp.exp2(s - m), 0.0)
        l = jnp.sum(e, axis=0, keepdims=True)
        probs.append(e * (1.0 / jnp.where(l > 0.0, l, 1.0)))
    outs = []
    for h in range(8):
        kh = h // 4
        out = _dot(vct, probs[h].astype(BF16))[kh * HALF:(kh + 1) * HALF]
        outs.append(out * jax.nn.sigmoid(gt_ref[0, 0, pl.ds(hg * 8 + h, 1), :]))
    for tt in range(4):
        tile_t = jnp.concatenate(outs[2 * tt:2 * tt + 2], axis=0)
        o_ref[0, tt * LANES:(tt + 1) * LANES, :] = tile_t.astype(o_ref.dtype)

    ovt = ovt_ref[...]
    j = lax.broadcasted_iota(jnp.int32, (N_SLC, 1), 0)
    tb = jnp.right_shift(t_row, 6)
    forced = (j == 0) | (j == tb) | (j == tb - 1)
    for grp in range(2):
        pg = probs[4 * grp:4 * grp + 4]
        psum = (pg[0] + pg[1]) + (pg[2] + pg[3])
        p_hi = psum.astype(BF16)
        p_lo = (psum - p_hi.astype(F32)).astype(BF16)
        imp = _dot(ovt, p_hi) + _dot(ovt, p_lo)
        imp = jnp.where(j > tb, NEG, jnp.where(forced, NSA_FORCE_BONUS, imp))
        cnt = jnp.zeros((N_SLC, tq), jnp.int32)
        for jp in range(N_SLC):
            row = imp[jp:jp + 1, :]
            beats = (row > imp) | ((row == imp) & (jp < j))
            cnt = cnt + beats.astype(jnp.int32)
        sel_t = (cnt < NSA_TOPK).astype(F32)
        sel_t = jnp.concatenate([sel_t, jnp.ones((LANES - N_SLC, tq), F32)], axis=0)
        nsel_ref[0, grp] = ((sel_t.T - 1.0) * SEL_BIG).astype(nsel_ref.dtype)


def _cmp_attn(q, kc, vct, gates_t, ovt):
    bsz, seq, _ = q.shape
    tq = CMP_TQ
    gw = 4 * LANES
    return pl.pallas_call(
        _cmp_kernel,
        grid=(bsz, NSA_GROUPS // 2, seq // tq),
        in_specs=[
            pl.BlockSpec((1, tq, gw), lambda b, g, i: (b, i, g)),
            pl.BlockSpec((1, N_CHUNK16, LANES), lambda b, g, i: (b, 0, g)),
            pl.BlockSpec((1, LANES, N_CHUNK16), lambda b, g, i: (b, g, 0)),
            pl.BlockSpec((1, 1, LANES, tq), lambda b, g, i: (b, i, 0, 0)),
            pl.BlockSpec(ovt.shape, lambda b, g, i: (0, 0)),
        ],
        out_specs=[
            pl.BlockSpec((1, gw, tq), lambda b, g, i: (b, g, i)),
            pl.BlockSpec((1, 2, tq, LANES), lambda b, g, i: (b, g, i, 0)),
        ],
        out_shape=[
            jax.ShapeDtypeStruct((bsz, D_MODEL, seq), BF16),
            jax.ShapeDtypeStruct((bsz, NSA_GROUPS, seq, LANES), BF16),
        ],
        scratch_shapes=[pltpu.VMEM((8, tq, LANES), BF16)],
        compiler_params=_cparams(("parallel", "parallel", "arbitrary")),
        name="cmp_attn",
    )(q, kc, vct, gates_t, ovt)


CS_BLK = 256


def _cumsum_kernel(z_ref, b_ref, c_ref):
    seq, width = z_ref.shape
    tri = (lax.broadcasted_iota(jnp.int32, (CS_BLK, CS_BLK), 0)
           >= lax.broadcasted_iota(jnp.int32, (CS_BLK, CS_BLK), 1)).astype(BF16)
    carry = jnp.zeros((1, width), F32)
    for blk in range(seq // CS_BLK):
        x = jax.nn.log_sigmoid(z_ref[blk * CS_BLK:(blk + 1) * CS_BLK, :] + b_ref[...])
        hi = x.astype(BF16)
        r1 = x - hi.astype(F32)
        mid = r1.astype(BF16)
        lo = (r1 - mid.astype(F32)).astype(BF16)
        cs = _dot(tri, hi) + _dot(tri, mid) + _dot(tri, lo) + carry
        c_ref[blk * CS_BLK:(blk + 1) * CS_BLK, :] = cs * LOG2E
        carry = cs[CS_BLK - 1:CS_BLK, :]


def _cumsum(z, bias):
    seq, width = z.shape
    return pl.pallas_call(
        _cumsum_kernel,
        grid=(width // LANES,),
        in_specs=[pl.BlockSpec((seq, LANES), lambda j: (0, j)),
                  pl.BlockSpec((1, LANES), lambda j: (0, j))],
        out_specs=pl.BlockSpec((seq, LANES), lambda j: (0, j)),
        out_shape=jax.ShapeDtypeStruct((seq, width), F32),
        compiler_params=_cparams(("parallel",)),
        name="cumsum",
    )(z, bias)


FOX_BIAS_COLS = 8


def _fox_bias_kernel(c_ref, ck_ref, cq_ref):
    c = c_ref[...]
    lane = lax.broadcasted_iota(jnp.int32, c.shape, 1)
    for k in range(FOX_BIAS_COLS):
        li = (pl.program_id(0) * FOX_BIAS_COLS + k) % LANES
        col = jnp.sum(jnp.where(lane == li, c, 0.0), axis=1, keepdims=True)
        hi = col.astype(BF16).astype(F32)
        mid = (col - hi).astype(BF16).astype(F32)
        lo = col - hi - mid
        k_piece = jnp.where(lane == 0, hi, jnp.where(lane == 1, mid, lo))
        q_piece = jnp.where(lane == 3, hi, jnp.where(lane == 4, mid, lo))
        ck_ref[k] = jnp.where(lane < 3, k_piece, jnp.where(lane < 6, 1.0, 0.0)).astype(BF16)
        cq_ref[k] = jnp.where(lane < 3, -1.0, jnp.where(lane < 6, q_piece, 0.0)).astype(BF16)


def _fox_bias(c, n):
    seq = c.shape[0]
    ospec = pl.BlockSpec((FOX_BIAS_COLS, seq, LANES), lambda j: (j, 0, 0))
    oshape = jax.ShapeDtypeStruct((n, seq, LANES), BF16)
    return pl.pallas_call(
        _fox_bias_kernel,
        grid=(n // FOX_BIAS_COLS,),
        in_specs=[pl.BlockSpec((seq, LANES), lambda j: (0, j * FOX_BIAS_COLS // LANES))],
        out_specs=[ospec, ospec],
        out_shape=[oshape, oshape],
        compiler_params=_cparams(("parallel",)),
        name="fox_bias",
    )(c)


def _rope_tables(seq):
    inv = ROPE_THETA ** (-jnp.arange(0, HEAD_DIM, 2, dtype=F32) / HEAD_DIM)
    ang = jnp.arange(seq, dtype=F32)[:, None] * inv[None, :]
    cos, sin = jnp.cos(ang), jnp.sin(ang)
    cos_t = jnp.tile(cos, (1, LANES // (HEAD_DIM // 2)))
    sin_t = jnp.tile(jnp.concatenate([-sin, sin], axis=1), (1, LANES // HEAD_DIM))
    return cos_t, sin_t


def _overlap_t(seq):
    n_cmp = (seq - NSA_CMP_LEN) // NSA_CMP_STRIDE + 1
    n_slc = seq // NSA_SLC_LEN
    cs = np.arange(n_cmp) * NSA_CMP_STRIDE
    ce = cs + NSA_CMP_LEN
    ss = np.arange(n_slc) * NSA_SLC_LEN
    se = ss + NSA_SLC_LEN
    ov = np.clip(np.minimum(ce[:, None], se[None, :]) - np.maximum(cs[:, None], ss[None, :]), 0, None)
    ov = (ov / NSA_CMP_LEN).astype(np.float32)
    ovt = np.zeros((n_slc, LANES), np.float32)
    ovt[:, :n_cmp] = ov.T
    return jnp.asarray(ovt, dtype=BF16)


def _expand_mat_t(seq, tk):
    key = np.arange(seq)
    e = ((key // NSA_SLC_LEN)[:, None] == np.arange(LANES)[None, :]).astype(np.float32)
    return jnp.asarray(e.reshape(seq // tk, tk, LANES), dtype=BF16)


def _compress_weights(pe, w1, w2):
    g = NSA_GROUPS
    half = NSA_CMP_LEN // 2
    eye = jnp.eye(g, dtype=F32)
    w1r = w1.reshape(NSA_CMP_LEN, HEAD_DIM, NSA_CMP_HIDDEN)

    def big(part):
        return jnp.einsum('ldj,gh->lgdhj', part, eye).reshape(half * g * HEAD_DIM, g * NSA_CMP_HIDDEN)

    w1big = jnp.concatenate([big(w1r[:half]), big(w1r[half:])], axis=1).astype(BF16)
    w2big = jnp.einsum('jd,gh->gjhd', w2, eye).reshape(g * NSA_CMP_HIDDEN, g * HEAD_DIM).astype(BF16)

    def pebig(part):
        return jnp.broadcast_to(part[:, None, :], (half, g, HEAD_DIM)).reshape(-1)

    pe2 = jnp.stack([pebig(pe[:half]), pebig(pe[half:])], axis=0)
    pe2 = jnp.concatenate([pe2, jnp.zeros((6, pe2.shape[1]), F32)], axis=0)
    return pe2, w1big, w2big


def _pad_cols(w, width):
    return jnp.concatenate([w, jnp.zeros((w.shape[0], width - w.shape[1]), w.dtype)], axis=1)


ATT_T = 256


def _nsa_mixer(x2, bsz, seq, gain, w_in, ck_pe, ck_w1, ck_w2, cv_pe, cv_w1, cv_w2,
               rope_tabs, ovt, emat_t):
    qw = N_HEADS * HEAD_DIM
    gd = NSA_GROUPS * HEAD_DIM

    def kvcols(c, s):
        lo = qw + (c * 2 + s) * gd
        return w_in[:, lo:lo + gd]

    w = jnp.concatenate([w_in[:, :qw]] + [kvcols(c, 0) for c in range(3)] + [kvcols(0, 1)],
                        axis=1).astype(BF16)
    wt = jnp.concatenate([kvcols(1, 1), kvcols(2, 1), _pad_cols(w_in[:, qw + 6 * gd:], LANES)],
                         axis=1).T.astype(BF16)
    scale = HEAD_DIM ** -0.5 * LOG2E
    row_groups = ((0, qw, True, scale, False), (qw, gd, True, 1.0, True),
                  (qw + gd, gd, True, 1.0, False), (qw + 2 * gd, gd, True, 1.0, False),
                  (qw + 3 * gd, gd, False, 1.0, True))
    t_groups = ((0, gd, ATT_T), (gd, gd, ATT_T), (2 * gd, LANES, ATT_T))
    q, k0, k1, k2, v0, v1t, v2t, gates_t = _proj(
        x2, gain, w, row_groups, (BF16, F32, BF16, BF16, F32), seq, wt=wt, t_groups=t_groups,
        t_dtypes=(BF16, BF16, F32), rope_tabs=rope_tabs)
    r3 = lambda a: a.reshape(bsz, seq, a.shape[-1])
    q, k1, k2 = map(r3, (q, k1, k2))

    planes = lambda a: a.reshape(a.shape[0], bsz, seq, LANES)
    xk, xv = planes(k0), planes(v0)
    pek, wk1, wk2 = _compress_weights(ck_pe, ck_w1, ck_w2)
    pev, wv1, wv2 = _compress_weights(cv_pe, cv_w1, cv_w2)
    kc, vct = _compress(xk, xv, pek, pev, wk1, wv1, wk2, wv2.T)

    o_cmp, nsel = _cmp_attn(q, kc, vct, gates_t, ovt)
    grp = tuple(h // 4 for h in range(8))
    o_slc = _flash(q, k1, v1t, nt=4, nkv=1, t=ATT_T, head_slot=(0,) * 8, head_half=grp,
                   out_dtype=BF16, head_group=grp, nsel=nsel, emat=emat_t, gates_t=gates_t,
                   gate_branch=1)
    o_win = _banded(q, k2, v2t, nt=4, tq=ATT_T, window=NSA_WINDOW, head_half=grp, out_dtype=BF16,
                    gates_t=gates_t, gate_branch=2)
    return [o_cmp, o_slc, o_win]


def _swa_mixer(x2, bsz, seq, gain, w_in, sinks, rope_tabs):
    qw = N_HEADS * HEAD_DIM
    scale = HEAD_DIM ** -0.5 * LOG2E
    row_groups = ((0, qw, True, scale, False), (qw, LANES, True, 1.0, False))
    w = w_in[:, :qw + LANES].astype(BF16)
    wt = w_in[:, qw + LANES:].T.astype(BF16)
    q, k, vt = _proj(x2, gain, w, row_groups, (BF16, BF16), seq, wt=wt,
                     t_groups=((0, LANES, SWA_WINDOW),), t_dtypes=(BF16,), rope_tabs=rope_tabs)
    r3 = lambda a: a.reshape(bsz, seq, a.shape[-1])
    o = _banded(r3(q), r3(k), vt, nt=8, tq=SWA_WINDOW, window=SWA_WINDOW,
                head_half=tuple(h // 8 for h in range(16)), out_dtype=BF16, sinks=sinks)
    return [o]


def _fox_mixer(x2, bsz, seq, gain, w_in, b_f):
    qw = N_HEADS * HEAD_DIM
    scale = HEAD_DIM ** -0.5 * LOG2E
    w = jnp.concatenate([w_in[:, :2 * qw], _pad_cols(w_in[:, 3 * qw:], LANES)], axis=1).astype(BF16)
    wt = w_in[:, 2 * qw:3 * qw].T.astype(BF16)
    row_groups = ((0, qw, False, scale, False), (qw, qw, False, 1.0, False),
                  (2 * qw, LANES, False, 1.0, False))
    q, k, f, vt = _proj(x2, gain, w, row_groups, (BF16, BF16, F32), seq, wt=wt,
                        t_groups=((0, qw, ATT_T),), t_dtypes=(BF16,))
    r3 = lambda a: a.reshape(bsz, seq, a.shape[-1])

    bh = bsz * N_HEADS
    bhp = -(-bh // LANES) * LANES
    z = f.reshape(bsz, seq, LANES)[:, :, :N_HEADS].transpose(1, 0, 2).reshape(seq, bh)
    z = _pad_cols(z, bhp)
    bias = _pad_cols(jnp.tile(b_f, bsz).reshape(1, bh), bhp)
    c = _cumsum(z, bias)
    ck, cq = _fox_bias(c, bh)
    r4 = lambda a: a.reshape(bsz, N_HEADS, seq, LANES)
    o = _flash(r3(q), r3(k), vt, nt=4, nkv=4, t=ATT_T,
               head_slot=tuple(h // 2 for h in range(8)), head_half=(0, 1) * 4, out_dtype=BF16,
               ck=r4(ck), cq=r4(cq))
    return [o]


def kernel(x, ffn1_norm, ffn1_w_gu, ffn1_w_down, mix_norm, ffn2_norm, ffn2_w_gu, ffn2_w_down,
           nsa_w_in, nsa_ck_pe, nsa_ck_w1, nsa_ck_w2, nsa_cv_pe, nsa_cv_w1, nsa_cv_w2, nsa_w_out,
           swa_w_in, swa_sinks, swa_w_out, fox_w_in, fox_b_f, fox_w_out, final_norm):
    bsz, seq, _ = x.shape
    depth = ffn1_norm.shape[0]
    rope_tabs = _rope_tables(seq)
    ovt = _overlap_t(seq)
    emat_t = _expand_mat_t(seq, ATT_T)
    x2 = x.reshape(bsz * seq, D_MODEL)
    for i in range(depth):
        kind, j = i % N_MIXERS, i // N_MIXERS
        x2 = _ffn(x2, ffn1_norm[i], ffn1_w_gu, ffn1_w_down, i)
        if kind == 0:
            branches = _nsa_mixer(x2, bsz, seq, mix_norm[i], nsa_w_in[j], nsa_ck_pe[j],
                                  nsa_ck_w1[j], nsa_ck_w2[j], nsa_cv_pe[j], nsa_cv_w1[j],
                                  nsa_cv_w2[j], rope_tabs, ovt, emat_t)
            w_out = nsa_w_out
        elif kind == 1:
            branches = _swa_mixer(x2, bsz, seq, mix_norm[i], swa_w_in[j], swa_sinks[j], rope_tabs)
            w_out = swa_w_out
        else:
            branches = _fox_mixer(x2, bsz, seq, mix_norm[i], fox_w_in[j], fox_b_f[j])
            w_out = fox_w_out
        x2 = _ffn(x2, ffn2_norm[i], ffn2_w_gu, ffn2_w_down, i, branches=branches, w_out=w_out,
                  w_out_layer=j, final_gain=final_norm if i == depth - 1 else None)
    return x2.reshape(bsz, seq, D_MODEL)
```
